```python
import jax, jax.numpy as jnp
from jax import lax
import numpy as np

D_MODEL = 4096
BATCH = 2
SEQ = 4096
DEPTH = 1

HEAD_DIM = 64
MIX_WIDTH = D_MODEL
RWKV_WIDTH = 3 * D_MODEL // 8
FOX_WIDTH = 3 * D_MODEL // 8
MEM_WIDTH = D_MODEL // 4
RWKV_HEADS = RWKV_WIDTH // HEAD_DIM
FOX_HEADS = FOX_WIDTH // HEAD_DIM
MEM_HEADS = 4
MEM_HEAD_DIM = MEM_WIDTH // MEM_HEADS
N_MEM = 256
DECAY_LORA = 128
ICLR_LORA = 128
Q_BLOCK = 128
RMS_EPS = 1e-6
GN_EPS = 64e-5

RWKV_SHIFT_SPLITS = [RWKV_WIDTH, RWKV_WIDTH, RWKV_WIDTH, DECAY_LORA, ICLR_LORA]
RWKV_SHIFT_WIDTH = 3 * RWKV_WIDTH + DECAY_LORA + ICLR_LORA
REST_SPLITS = [RWKV_WIDTH, FOX_WIDTH, FOX_WIDTH, FOX_WIDTH, FOX_HEADS, FOX_WIDTH, MEM_WIDTH, MEM_WIDTH]
REST_WIDTH = RWKV_WIDTH + 4 * FOX_WIDTH + FOX_HEADS + 2 * MEM_WIDTH
IN_WIDTH = RWKV_SHIFT_WIDTH + REST_WIDTH

kernel_name = "hymba_rwkv7_fox_memxattn_layer"


def _offsets(sizes):
    return [int(s) for s in np.cumsum(sizes)[:-1]]


def rms_norm(x, g):
    xf = x.astype(jnp.float32)
    y = xf * lax.rsqrt(jnp.mean(xf * xf, axis=-1, keepdims=True) + RMS_EPS)
    return (y * g.astype(jnp.float32)).astype(x.dtype)


def token_shift(p, mu):
    prev = jnp.pad(p, ((0, 0), (1, 0), (0, 0)))[:, :-1]
    return p + (prev - p) * mu


def rwkv7_branch(r, k, v, wl, al, w0, w_decay_up, a0, w_iclr_up, k_k, k_a, r_k, ln_x_w, ln_x_b):
    B, T, _ = r.shape
    H, N = RWKV_HEADS, HEAD_DIM
    f32 = jnp.float32
    w_pre = -jax.nn.softplus(-(w0 + jnp.tanh(wl) @ w_decay_up).astype(f32)) - 0.5
    decay = jnp.exp(-jnp.exp(w_pre))
    alpha = jax.nn.sigmoid((a0 + al @ w_iclr_up).astype(f32))
    kk = (k * k_k).astype(f32).reshape(B, T, H, N)
    kk = kk * lax.rsqrt(jnp.maximum(jnp.sum(kk * kk, axis=-1, keepdims=True), 1e-24))
    k_mod = k.astype(f32) * (1.0 + (alpha - 1.0) * k_a)

    def heads(z):
        return z.astype(f32).reshape(B, T, H, N)

    r_h, k_h, v_h, w_h, al_h = heads(r), heads(k_mod), heads(v), heads(decay), heads(alpha)
    a_h = -kk
    b_h = kk * al_h

    def step(S, inp):
        r_t, w_t, k_t, v_t, a_t, b_t = inp
        Sa = jnp.einsum('bhij,bhj->bhi', S, a_t)
        S = S * w_t[:, :, None, :] + Sa[..., None] * b_t[:, :, None, :] + v_t[..., None] * k_t[:, :, None, :]
        y_t = jnp.einsum('bhij,bhj->bhi', S, r_t)
        return S, y_t

    tm = lambda z: jnp.moveaxis(z, 1, 0)
    S0 = jnp.zeros((B, H, N, N), f32)
    _, y = lax.scan(step, S0, (tm(r_h), tm(w_h), tm(k_h), tm(v_h), tm(a_h), tm(b_h)))
    y = jnp.moveaxis(y, 0, 1)
    mean = jnp.mean(y, axis=-1, keepdims=True)
    var = jnp.mean(jnp.square(y - mean), axis=-1, keepdims=True)
    y = (y - mean) * lax.rsqrt(var + GN_EPS) * ln_x_w.astype(f32).reshape(H, N) + ln_x_b.astype(f32).reshape(H, N)
    bonus = jnp.sum(r_h * k_h * r_k.astype(f32), axis=-1, keepdims=True) * v_h
    return (y + bonus).reshape(B, T, H * N).astype(r.dtype)


def fox_attention(q, k, v, log_f):
    B, H, T, D = q.shape
    n_blk = T // Q_BLOCK
    cum = jnp.cumsum(log_f.astype(jnp.float32), axis=-1)
    kpos = jnp.arange(T)
    scale = D ** -0.5

    def one_block(i):
        start = i * Q_BLOCK
        qb = lax.dynamic_slice_in_dim(q, start, Q_BLOCK, axis=2)
        cb = lax.dynamic_slice_in_dim(cum, start, Q_BLOCK, axis=2)
        s = jnp.einsum('bhqd,bhkd->bhqk', qb, k).astype(jnp.float32) * scale + cb[..., :, None] - cum[..., None, :]
        qpos = start + jnp.arange(Q_BLOCK)
        s = jnp.where(kpos[None, :] <= qpos[:, None], s, -jnp.inf)
        p = jax.nn.softmax(s, axis=-1)
        return jnp.einsum('bhqk,bhkd->bhqd', p.astype(v.dtype), v)

    out = lax.map(one_block, jnp.arange(n_blk))
    return jnp.moveaxis(out, 0, 2).reshape(B, H, T, D)


def memory_cross_attention(q, mem, g_mem, w_mem_kv):
    B, T, _ = q.shape
    M = mem.shape[1]
    mkv = rms_norm(mem, g_mem) @ w_mem_kv
    mk, mv = jnp.split(mkv, 2, axis=-1)
    qh = q.reshape(B, T, MEM_HEADS, MEM_HEAD_DIM)
    mk = mk.reshape(B, M, MEM_HEADS, MEM_HEAD_DIM)
    mv = mv.reshape(B, M, MEM_HEADS, MEM_HEAD_DIM)
    s = jnp.einsum('bthd,bmhd->bhtm', qh, mk).astype(jnp.float32) * (MEM_HEAD_DIM ** -0.5)
    p = jax.nn.softmax(s, axis=-1)
    o = jnp.einsum('bhtm,bmhd->bthd', p.astype(mv.dtype), mv)
    return o.reshape(B, T, MEM_WIDTH)


def hybrid_layer(x, mem, g_pre, w_in, mu_rwkv, w0, w_decay_up, a0, w_iclr_up, k_k, k_a, r_k,
                 ln_x_w, ln_x_b, b_f, g_mem, w_mem_kv, w_out, g_post):
    B, T, _ = x.shape
    h = rms_norm(x, g_pre)
    p = h @ w_in
    p_shift = token_shift(p[..., :RWKV_SHIFT_WIDTH], mu_rwkv)
    r, k, v, wl, al = jnp.split(p_shift, _offsets(RWKV_SHIFT_SPLITS), axis=-1)
    g_rwkv, fq, fk, fv, f_logit, g_fox, mq, g_mq = jnp.split(p[..., RWKV_SHIFT_WIDTH:], _offsets(REST_SPLITS), axis=-1)

    y_rwkv = rwkv7_branch(r, k, v, wl, al, w0, w_decay_up, a0, w_iclr_up, k_k, k_a, r_k, ln_x_w, ln_x_b)

    to_heads = lambda z: z.reshape(B, T, FOX_HEADS, HEAD_DIM).transpose(0, 2, 1, 3)
    log_f = jax.nn.log_sigmoid((f_logit + b_f).astype(jnp.float32)).transpose(0, 2, 1)
    y_fox = fox_attention(to_heads(fq), to_heads(fk), to_heads(fv), log_f)
    y_fox = y_fox.transpose(0, 2, 1, 3).reshape(B, T, FOX_WIDTH)

    y_mem = memory_cross_attention(mq, mem, g_mem, w_mem_kv)

    y = jnp.concatenate([y_rwkv * jax.nn.silu(g_rwkv),
                         y_fox * jax.nn.silu(g_fox),
                         y_mem * jax.nn.silu(g_mq)], axis=-1)
    y = y @ w_out
    return x + rms_norm(y, g_post)


def setup_inputs(seed: int = 0) -> dict:
    key = jax.random.key(seed)
    ks = jax.random.split(key, 20)
    f32 = jnp.float32
    nrm = lambda k, shape, s: jax.random.normal(k, shape, f32) * s
    n = jnp.arange(RWKV_WIDTH, dtype=f32) / (RWKV_WIDTH - 1)
    return {
        "x": nrm(ks[0], (BATCH, SEQ, D_MODEL), 1.0),
        "mem": nrm(ks[1], (BATCH, N_MEM, D_MODEL), 1.0),
        "g_pre": 1.0 + nrm(ks[2], (DEPTH, D_MODEL), 0.02),
        "w_in": nrm(ks[3], (DEPTH, D_MODEL, IN_WIDTH), D_MODEL ** -0.5),
        "mu_rwkv": jax.random.uniform(ks[4], (DEPTH, RWKV_SHIFT_WIDTH), f32),
        "w0": (-5.5 + 5.0 * n ** 0.85)[None, :] + nrm(ks[5], (DEPTH, RWKV_WIDTH), 0.1),
        "w_decay_up": nrm(ks[6], (DEPTH, DECAY_LORA, RWKV_WIDTH), DECAY_LORA ** -0.5),
        "a0": nrm(ks[7], (DEPTH, RWKV_WIDTH), 0.1),
        "w_iclr_up": nrm(ks[8], (DEPTH, ICLR_LORA, RWKV_WIDTH), ICLR_LORA ** -0.5),
        "k_k": 0.85 + nrm(ks[9], (DEPTH, RWKV_WIDTH), 0.02),
        "k_a": 1.0 + nrm(ks[10], (DEPTH, RWKV_WIDTH), 0.02),
        "r_k": -0.04 + nrm(ks[11], (DEPTH, RWKV_HEADS, HEAD_DIM), 0.02),
        "ln_x_w": 1.0 + nrm(ks[12], (DEPTH, RWKV_WIDTH), 0.02),
        "ln_x_b": nrm(ks[13], (DEPTH, RWKV_WIDTH), 0.02),
        "b_f": 2.0 + nrm(ks[14], (DEPTH, FOX_HEADS), 0.5),
        "g_mem": 1.0 + nrm(ks[15], (DEPTH, D_MODEL), 0.02),
        "w_mem_kv": nrm(ks[16], (DEPTH, D_MODEL, 2 * MEM_WIDTH), D_MODEL ** -0.5),
        "w_out": nrm(ks[17], (DEPTH, MIX_WIDTH, D_MODEL), MIX_WIDTH ** -0.5),
        "g_post": 1.0 + nrm(ks[18], (DEPTH, D_MODEL), 0.02),
    }


def reference(x, mem, g_pre, w_in, mu_rwkv, w0, w_decay_up, a0, w_iclr_up, k_k, k_a, r_k,
              ln_x_w, ln_x_b, b_f, g_mem, w_mem_kv, w_out, g_post):
    for l in range(DEPTH):
        x = hybrid_layer(x, mem, g_pre[l], w_in[l], mu_rwkv[l], w0[l], w_decay_up[l], a0[l],
                         w_iclr_up[l], k_k[l], k_a[l], r_k[l], ln_x_w[l], ln_x_b[l], b_f[l],
                         g_mem[l], w_mem_kv[l], w_out[l], g_post[l])
    return x
```

```python
import functools

import jax
import jax.numpy as jnp
from jax import lax
from jax.experimental import pallas as pl
from jax.experimental.pallas import tpu as pltpu

HEAD_DIM = 64
MEM_HEADS = 4
RMS_EPS = 1e-6
GN_EPS = 64e-5

LANES = 128
UNIT_HEADS = 4
UNIT = UNIT_HEADS * HEAD_DIM
RWKV_CHUNK = 64
LORA_BLOCK = 512
VMEM_LIMIT_BYTES = 56 * 1024 * 1024

F32 = jnp.float32
BF16 = jnp.bfloat16

_NN = (((1,), (0,)), ((), ()))
_NT = (((1,), (1,)), ((), ()))
_TN = (((0,), (0,)), ((), ()))


def _dot(a, b, dims=_NN):
    return lax.dot_general(a, b, dims, preferred_element_type=F32)


def _split_bf16(x, n):
    parts = []
    rem = x
    for _ in range(n):
        p = rem.astype(BF16)
        parts.append(p)
        rem = rem - p.astype(F32)
    return parts


def _dot_split(a, b_bf16, n, dims=_NN):
    acc = None
    for p in _split_bf16(a, n):
        d = _dot(p, b_bf16, dims)
        acc = d if acc is None else acc + d
    return acc


def _dot_f32(a, b):
    a_hi, a_lo = _split_bf16(a, 2)
    b_hi, b_lo = _split_bf16(b, 2)
    return _dot(a_hi, b_hi) + (_dot(a_lo, b_hi) + _dot(a_hi, b_lo))


def _sigmoid(x):
    return 1.0 / (1.0 + jnp.exp(-x))


def _softplus(x):
    return jnp.maximum(x, 0.0) + jnp.log1p(jnp.exp(-jnp.abs(x)))


def _params(*semantics):
    return pltpu.CompilerParams(dimension_semantics=semantics, vmem_limit_bytes=VMEM_LIMIT_BYTES)


def _rmsnorm_kernel(x_ref, g_ref, o_ref):
    x = x_ref[...]
    ms = jnp.mean(x * x, axis=-1, keepdims=True)
    o_ref[...] = (x * lax.rsqrt(ms + RMS_EPS) * g_ref[...]).astype(o_ref.dtype)


def _rmsnorm(x2d, g, out_dtype, tm=256):
    m, d = x2d.shape
    tm = min(tm, m)
    assert m % tm == 0
    return pl.pallas_call(
        _rmsnorm_kernel,
        grid=(m // tm,),
        in_specs=[pl.BlockSpec((tm, d), lambda i: (i, 0)), pl.BlockSpec((1, d), lambda i: (0, 0))],
        out_specs=pl.BlockSpec((tm, d), lambda i: (i, 0)),
        out_shape=jax.ShapeDtypeStruct((m, d), out_dtype),
        compiler_params=_params("parallel"),
        name="rmsnorm",
    )(x2d, g.reshape(1, d))


def _matmul_kernel(*refs, n_in):
    o_ref = refs[2 * n_in]
    acc = None
    for x_ref, w_ref in zip(refs[:n_in], refs[n_in:2 * n_in]):
        d = _dot(x_ref[...], w_ref[...])
        acc = d if acc is None else acc + d
    o_ref[...] = acc.astype(o_ref.dtype)


def _matmul(xs, ws, out_dtype, tm, tn, name):
    m = xs[0].shape[0]
    n = ws[0].shape[1]
    tm = min(tm, m)
    tn = min(tn, n)
    assert m % tm == 0 and n % tn == 0, (m, tm, n, tn)
    in_specs = [pl.BlockSpec((tm, x.shape[1]), lambda j, i: (i, 0)) for x in xs]
    in_specs += [pl.BlockSpec((w.shape[0], tn), lambda j, i: (0, j)) for w in ws]
    return pl.pallas_call(
        functools.partial(_matmul_kernel, n_in=len(xs)),
        grid=(n // tn, m // tm),
        in_specs=in_specs,
        out_specs=pl.BlockSpec((tm, tn), lambda j, i: (i, j)),
        out_shape=jax.ShapeDtypeStruct((m, n), out_dtype),
        compiler_params=_params("parallel", "parallel"),
        name=name,
    )(*xs, *ws)


def _post_kernel(x_ref, z_ref, g_ref, o_ref):
    z = z_ref[...]
    ms = jnp.mean(z * z, axis=-1, keepdims=True)
    o_ref[...] = x_ref[...] + z * lax.rsqrt(ms + RMS_EPS) * g_ref[...]


def _post(x2d, z2d, g, tm=256):
    m, d = x2d.shape
    tm = min(tm, m)
    assert m % tm == 0
    row = pl.BlockSpec((tm, d), lambda i: (i, 0))
    return pl.pallas_call(
        _post_kernel,
        grid=(m // tm,),
        in_specs=[row, row, pl.BlockSpec((1, d), lambda i: (0, 0))],
        out_specs=row,
        out_shape=jax.ShapeDtypeStruct((m, d), x2d.dtype),
        compiler_params=_params("parallel"),
        name="post_norm_residual",
    )(x2d, z2d, g.reshape(1, d))


def _head_ones(n):
    row = lax.broadcasted_iota(jnp.int32, (n, n), 0) // HEAD_DIM
    col = lax.broadcasted_iota(jnp.int32, (n, n), 1) // HEAD_DIM
    return jnp.where(row == col, 1.0, 0.0).astype(BF16)


def _rwkv_prep_kernel(r_ref, k_ref, v_ref, lora_ref, mu_r_ref, mu_k_ref, mu_v_ref, mu_lora_ref,
                      w0_ref, wdu_ref, a0_ref, wiu_ref, kk_ref, ka_ref,
                      r_o, k_o, v_o, lw_o, a_o, b_o,
                      carry_r, carry_k, carry_v, carry_lora):
    @pl.when(pl.program_id(2) == 0)
    def _():
        for c in (carry_r, carry_k, carry_v, carry_lora):
            c[...] = jnp.zeros_like(c)

    tm = r_ref.shape[1]
    first_row = lax.broadcasted_iota(jnp.int32, (tm, 1), 0) == 0

    def shift(x, carry_ref, mu):
        prev = jnp.where(first_row, carry_ref[...], pltpu.roll(x, 1, 0))
        carry_ref[...] = x[tm - 1:tm, :]
        return x + (prev - x) * mu

    r = shift(r_ref[0], carry_r, mu_r_ref[...])
    k = shift(k_ref[0], carry_k, mu_k_ref[...])
    v = shift(v_ref[0], carry_v, mu_v_ref[...])
    lora = shift(lora_ref[0][:, :2 * LANES], carry_lora, mu_lora_ref[...])
    wl = lora[:, :LANES]
    al = lora[:, LANES:]

    w_pre = -_softplus(-(w0_ref[...] + _dot_f32(jnp.tanh(wl), wdu_ref[...]))) - 0.5
    log_decay = -jnp.exp(w_pre)
    alpha = _sigmoid(a0_ref[...] + _dot_f32(al, wiu_ref[...]))

    kk = k * kk_ref[...]
    ss = _dot_split(kk * kk, _head_ones(LANES), 2)
    kk = kk * lax.rsqrt(jnp.maximum(ss, 1e-24))

    r_o[0] = r
    k_o[0] = k * (1.0 + (alpha - 1.0) * ka_ref[...])
    v_o[0] = v
    lw_o[0] = log_decay
    a_o[0] = -kk
    b_o[0] = kk * alpha


def _rwkv_prep(pa3, off_r, off_lora, mu, w0, wdu, a0, wiu, k_k, k_a, c, tm=512):
    b, t, _ = pa3.shape
    tm = min(tm, t)
    assert t % tm == 0 and c % LANES == 0 and off_r % LANES == 0 and off_lora % LORA_BLOCK == 0
    nslab = c // LANES
    rb, lb = off_r // LANES, off_lora // LORA_BLOCK

    def col(base):
        return pl.BlockSpec((1, tm, LANES), lambda bi, s, i: (bi, i, base + s))

    def vec(base=0):
        return pl.BlockSpec((1, LANES), lambda bi, s, i: (0, base + s))

    lora_w = pl.BlockSpec((wdu.shape[0], LANES), lambda bi, s, i: (0, s))
    out = pl.BlockSpec((1, tm, LANES), lambda bi, s, i: (bi, i, s))
    mu_rkv = mu[:3 * c].reshape(1, 3 * c)
    mu_lora = mu[3 * c:].reshape(1, 2 * LANES)
    row = lambda z: z.reshape(1, c)
    return pl.pallas_call(
        _rwkv_prep_kernel,
        grid=(b, nslab, t // tm),
        in_specs=[col(rb), col(rb + nslab), col(rb + 2 * nslab),
                  pl.BlockSpec((1, tm, LORA_BLOCK), lambda bi, s, i: (bi, i, lb)),
                  vec(0), vec(nslab), vec(2 * nslab),
                  pl.BlockSpec((1, 2 * LANES), lambda bi, s, i: (0, 0)),
                  vec(), lora_w, vec(), lora_w, vec(), vec()],
        out_specs=[out] * 6,
        out_shape=[jax.ShapeDtypeStruct((b, t, c), F32)] * 6,
        scratch_shapes=[pltpu.VMEM((1, LANES), F32)] * 3 + [pltpu.VMEM((1, 2 * LANES), F32)],
        compiler_params=_params("parallel", "parallel", "arbitrary"),
        name="rwkv_prep",
    )(pa3, pa3, pa3, pa3, mu_rkv, mu_rkv, mu_rkv, mu_lora,
      row(w0), wdu, row(a0), wiu, row(k_k), row(k_a))


def _rwkv_unit(r, k, v, lw, cl, a, b, g, rk, lnw, lnb, s_ref, masks):
    bd_rows, strict, incl, eye, bd_state, ones_bd = masks
    L = r.shape[0]
    n_sq = L.bit_length() - 2
    assert 2 ** (n_sq + 1) == L

    cl_last = cl[L - 1:L, :]
    e_pos = jnp.exp(cl)
    e_neg = jnp.exp(-cl)
    e_rem = jnp.exp(cl_last - cl)
    ah = a * jnp.exp(cl - lw)
    rh = r * e_pos
    bh = b * e_neg
    kh = k * e_neg
    bt = b * e_rem
    kt = k * e_rem

    def rep(x):
        return jnp.concatenate([x] * UNIT_HEADS, axis=0)

    def stack(x):
        return jnp.where(bd_rows, rep(x), 0.0)

    def unstack(xs):
        out = xs[0:L]
        for h in range(1, UNIT_HEADS):
            out = out + xs[h * L:(h + 1) * L]
        return out

    lhs = jnp.concatenate([stack(ah), stack(rh)], axis=0).astype(BF16)
    a_b = _dot(lhs, rep(bh).astype(BF16), _NT)
    a_k = _dot(lhs, rep(kh).astype(BF16), _NT)
    hl = UNIT_HEADS * L
    n_ab = jnp.where(strict, a_b[:hl], 0.0)
    a_ak = jnp.where(strict, a_k[:hl], 0.0).astype(BF16)
    a_rb = jnp.where(incl, a_b[hl:], 0.0).astype(BF16)
    a_rk = jnp.where(incl, a_k[hl:], 0.0).astype(BF16)

    t_inv = eye + n_ab
    pw = n_ab
    for _ in range(n_sq):
        pw_b = pw.astype(BF16)
        pw = _dot(pw_b, pw_b)
        t_inv = t_inv + _dot(t_inv.astype(BF16), pw.astype(BF16))

    s0 = s_ref[...]
    xr = _dot(jnp.concatenate([ah, rh], axis=0).astype(BF16), s0.astype(BF16), _NT)
    v_s = stack(v).astype(BF16)
    x_s = stack(xr[:L]) + _dot(a_ak, v_s)
    u_s = _dot(t_inv.astype(BF16), x_s.astype(BF16))
    y_s = stack(xr[L:]) + _dot(a_rb, u_s.astype(BF16)) + _dot(a_rk, v_s)
    u = unstack(u_s)
    y = unstack(y_s)

    upd = _dot(jnp.concatenate([u, v], axis=0).astype(BF16),
               jnp.concatenate([bt, kt], axis=0).astype(BF16), _TN)
    s_ref[...] = jnp.where(bd_state, s0 * e_pos[L - 1:L, :] + upd, 0.0)

    def headsum(x):
        return _dot_split(x, ones_bd, 2)

    inv_n = 1.0 / HEAD_DIM
    d = y - headsum(y) * inv_n
    var = headsum(d * d) * inv_n
    yn = d * lax.rsqrt(var + GN_EPS) * lnw + lnb
    bonus = headsum(r * k * rk) * v
    return (yn + bonus) * (g * _sigmoid(g))


def _rwkv_scan_kernel(r_ref, k_ref, v_ref, lw_ref, a_ref, b_ref, g_ref, rk_ref, lnw_ref, lnb_ref,
                      o_ref, s_ref, *, n_units):
    @pl.when(pl.program_id(2) == 0)
    def _():
        s_ref[...] = jnp.zeros_like(s_ref)

    L = r_ref.shape[1]
    hl = UNIT_HEADS * L

    def blk(shape, d0, d1):
        return (lax.broadcasted_iota(jnp.int32, shape, 0) // d0) == (lax.broadcasted_iota(jnp.int32, shape, 1) // d1)

    t_row = lax.broadcasted_iota(jnp.int32, (hl, hl), 0)
    t_col = lax.broadcasted_iota(jnp.int32, (hl, hl), 1)
    same = blk((hl, hl), L, L)
    bd_state = blk((UNIT, UNIT), HEAD_DIM, HEAD_DIM)
    masks = (blk((hl, UNIT), L, HEAD_DIM),
             same & (t_col < t_row),
             same & (t_col <= t_row),
             jnp.where(t_row == t_col, 1.0, 0.0),
             bd_state,
             jnp.where(bd_state, 1.0, 0.0).astype(BF16))

    tril = jnp.where(lax.broadcasted_iota(jnp.int32, (L, L), 0) >= lax.broadcasted_iota(jnp.int32, (L, L), 1),
                     1.0, 0.0).astype(BF16)
    lw_all = lw_ref[0]
    lw_hi, lw_lo = _split_bf16(lw_all, 2)
    cl_all = _dot(tril, lw_hi) + _dot(tril, lw_lo)

    for u in range(n_units):
        sl = slice(u * UNIT, (u + 1) * UNIT)
        out = _rwkv_unit(r_ref[0, :, sl], k_ref[0, :, sl], v_ref[0, :, sl], lw_all[:, sl], cl_all[:, sl],
                         a_ref[0, :, sl], b_ref[0, :, sl], g_ref[0, :, sl],
                         rk_ref[:, sl], lnw_ref[:, sl], lnb_ref[:, sl], s_ref.at[u], masks)
        o_ref[0, :, sl] = out.astype(o_ref.dtype)


def _rwkv_scan(prep, pa3, off_g, r_k, ln_w, ln_b, n_units=2):
    b, t, c = prep[0].shape
    L = RWKV_CHUNK
    w = n_units * UNIT
    assert t % L == 0 and c % w == 0 and off_g % w == 0
    gb = off_g // w
    blk = pl.BlockSpec((1, L, w), lambda bi, gi, ci: (bi, ci, gi))
    vec = pl.BlockSpec((1, w), lambda bi, gi, ci: (0, gi))
    return pl.pallas_call(
        functools.partial(_rwkv_scan_kernel, n_units=n_units),
        grid=(b, c // w, t // L),
        in_specs=[blk] * 6 + [pl.BlockSpec((1, L, w), lambda bi, gi, ci: (bi, ci, gb + gi)), vec, vec, vec],
        out_specs=blk,
        out_shape=jax.ShapeDtypeStruct((b, t, c), BF16),
        scratch_shapes=[pltpu.VMEM((n_units, UNIT, UNIT), F32)],
        compiler_params=_params("parallel", "parallel", "arbitrary"),
        name="rwkv_scan",
    )(*prep, pa3, r_k.reshape(1, c), ln_w.reshape(1, c), ln_b.reshape(1, c))


def _fox_prefix_kernel(f_ref, bf_ref, o_ref, carry_ref):
    @pl.when(pl.program_id(1) == 0)
    def _():
        carry_ref[...] = jnp.zeros_like(carry_ref)

    x = f_ref[0][:, 2 * LANES:3 * LANES] + bf_ref[...]
    log_f = -_softplus(-x)
    tc = x.shape[0]
    tri = jnp.where(lax.broadcasted_iota(jnp.int32, (tc, tc), 0) <= lax.broadcasted_iota(jnp.int32, (tc, tc), 1),
                    1.0, 0.0).astype(BF16)
    c = _dot_split(log_f, tri, 3, _TN) + carry_ref[...]
    carry_ref[...] = c[:, tc - 1:tc]
    o_ref[0] = c[:o_ref.shape[1]]


def _fox_prefix(pa3, off_lora, b_f, tc=512):
    b, t, _ = pa3.shape
    fh = b_f.shape[0]
    tc = min(tc, t)
    assert t % tc == 0 and fh % 8 == 0 and fh <= LANES
    bf = jnp.zeros((1, LANES), F32).at[0, :fh].set(b_f)
    lb = off_lora // LORA_BLOCK
    return pl.pallas_call(
        _fox_prefix_kernel,
        grid=(b, t // tc),
        in_specs=[pl.BlockSpec((1, tc, LORA_BLOCK), lambda bi, i: (bi, i, lb)),
                  pl.BlockSpec((1, LANES), lambda bi, i: (0, 0))],
        out_specs=pl.BlockSpec((1, fh, tc), lambda bi, i: (bi, 0, i)),
        out_shape=jax.ShapeDtypeStruct((b, fh, t), F32),
        scratch_shapes=[pltpu.VMEM((LANES, 1), F32)],
        compiler_params=_params("parallel", "arbitrary"),
        name="fox_prefix",
    )(pa3, bf)


def _fox_attn_kernel(q_ref, k_ref, v_ref, c_ref, g_ref, o_ref, *, tk):
    qi = pl.program_id(2)
    tq = q_ref.shape[1]
    n_sub = tq // tk
    scale = HEAD_DIM ** -0.5
    head0 = lax.broadcasted_iota(jnp.int32, (1, LANES), 1) < HEAD_DIM
    q = q_ref[0] * scale
    zero = jnp.zeros_like(q)
    qs = (jnp.where(head0, q, zero), jnp.where(head0, zero, q))

    def tile(j, carry, masked):
        start = pl.multiple_of(j * tk, tk)
        kj = k_ref[0, pl.ds(start, tk), :]
        vj = v_ref[0, pl.ds(start, tk), :]
        cj = c_ref[0, 0, :, pl.ds(start, tk)]
        new = []
        for h in range(2):
            m, l, acc = carry[h]
            s = _dot(qs[h], kj, _NT) - cj[h:h + 1, :]
            if masked is not None:
                s = jnp.where(masked, s, -jnp.inf)
            m_new = jnp.maximum(m, jnp.max(s, axis=1, keepdims=True))
            p = jnp.exp(s - m_new)
            alpha = jnp.exp(m - m_new)
            l = alpha * l + jnp.sum(p, axis=1, keepdims=True)
            acc = alpha * acc + _dot(p.astype(BF16), vj)
            new.append((m_new, l, acc))
        return tuple(new)

    init = tuple((jnp.full((tq, 1), -jnp.inf, F32), jnp.zeros((tq, 1), F32), jnp.zeros((tq, LANES), F32))
                 for _ in range(2))
    carry = lax.fori_loop(0, qi * n_sub, lambda j, c: tile(j, c, None), init)
    row = lax.broadcasted_iota(jnp.int32, (tq, tk), 0)
    col = lax.broadcasted_iota(jnp.int32, (tq, tk), 1)
    for d in range(n_sub):
        carry = tile(qi * n_sub + d, carry, col + d * tk <= row)

    (_, l0, acc0), (_, l1, acc1) = carry
    out = jnp.where(head0, acc0 / l0, acc1 / l1)
    g = g_ref[0]
    o_ref[0] = (out * (g * _sigmoid(g))).astype(o_ref.dtype)


def _fox_attn(pb3, off_q, off_k, off_v, c4, pa3, off_g, fw, tq=256, tk=256):
    b, t, _ = pb3.shape
    tq = min(tq, t)
    tk = min(tk, tq)
    assert t % tq == 0 and tq % tk == 0 and fw % LANES == 0
    npair = fw // LANES
    qb, kb, vb, gb = off_q // LANES, off_k // LANES, off_v // LANES, off_g // LANES
    return pl.pallas_call(
        functools.partial(_fox_attn_kernel, tk=tk),
        grid=(b, npair, t // tq),
        in_specs=[pl.BlockSpec((1, tq, LANES), lambda bi, p, i: (bi, i, qb + p)),
                  pl.BlockSpec((1, t, LANES), lambda bi, p, i: (bi, 0, kb + p)),
                  pl.BlockSpec((1, t, LANES), lambda bi, p, i: (bi, 0, vb + p)),
                  pl.BlockSpec((1, 1, 2, t), lambda bi, p, i: (bi, p, 0, 0)),
                  pl.BlockSpec((1, tq, LANES), lambda bi, p, i: (bi, i, gb + p))],
        out_specs=pl.BlockSpec((1, tq, LANES), lambda bi, p, i: (bi, i, p)),
        out_shape=jax.ShapeDtypeStruct((b, t, fw), BF16),
        compiler_params=_params("parallel", "parallel", "arbitrary"),
        name="fox_attn",
    )(pb3, pb3, pb3, c4, pa3)


def _mem_attn_kernel(q_ref, kv_ref, g_ref, o_ref):
    mw = q_ref.shape[2]
    hd = mw // MEM_HEADS
    scale = hd ** -0.5
    outs = []
    for h in range(MEM_HEADS):
        q = q_ref[0, :, h * hd:(h + 1) * hd]
        mk = kv_ref[0, :, h * hd:(h + 1) * hd]
        mv = kv_ref[0, :, mw + h * hd:mw + (h + 1) * hd]
        s = _dot(q, mk, _NT) * scale
        p = jnp.exp(s - jnp.max(s, axis=1, keepdims=True))
        l = jnp.sum(p, axis=1, keepdims=True)
        outs.append(_dot(p.astype(BF16), mv) / l)
    g = g_ref[0]
    o_ref[0] = (jnp.concatenate(outs, axis=1) * (g * _sigmoid(g))).astype(o_ref.dtype)


def _mem_attn(pb3, off_q, mkv3, pa3, off_g, mw, tm=512):
    b, t, _ = pb3.shape
    tm = min(tm, t)
    assert t % tm == 0 and off_q % mw == 0 and off_g % mw == 0 and (mw // MEM_HEADS) % LANES == 0
    n_mem = mkv3.shape[1]
    qb, gb = off_q // mw, off_g // mw
    return pl.pallas_call(
        _mem_attn_kernel,
        grid=(b, t // tm),
        in_specs=[pl.BlockSpec((1, tm, mw), lambda bi, i: (bi, i, qb)),
                  pl.BlockSpec((1, n_mem, 2 * mw), lambda bi, i: (bi, 0, 0)),
                  pl.BlockSpec((1, tm, mw), lambda bi, i: (bi, i, gb))],
        out_specs=pl.BlockSpec((1, tm, mw), lambda bi, i: (bi, i, 0)),
        out_shape=jax.ShapeDtypeStruct((b, t, mw), BF16),
        compiler_params=_params("parallel", "parallel"),
        name="mem_attn",
    )(pb3, mkv3, pa3)


def _layer(x, mem, g_pre, w_in, mu_rwkv, w0, w_decay_up, a0, w_iclr_up, k_k, k_a, r_k,
           ln_x_w, ln_x_b, b_f, g_mem, w_mem_kv, w_out, g_post):
    b, t, d = x.shape
    n_mem = mem.shape[1]
    c = w0.shape[0]
    fh = b_f.shape[0]
    fw = fh * HEAD_DIM
    mw = w_mem_kv.shape[1] // 2
    lora = w_decay_up.shape[0]
    assert lora == LANES and w_iclr_up.shape[0] == LANES
    assert w_in.shape[1] == 4 * c + 2 * lora + 4 * fw + fh + 2 * mw

    o = 0
    cols = {}
    for name, width in (("r", c), ("k", c), ("v", c), ("wl", lora), ("al", lora), ("g_rwkv", c), ("fq", fw),
                        ("fk", fw), ("fv", fw), ("f", fh), ("g_fox", fw), ("mq", mw), ("g_mq", mw)):
        cols[name] = w_in[:, o:o + width]
        o += width

    pad = jnp.zeros((d, LORA_BLOCK - 2 * lora - fh), w_in.dtype)
    names_a = ("g_rwkv", "r", "k", "v", "g_mq", "g_fox", "wl", "al", "f")
    w_a = jnp.concatenate([cols[n] for n in names_a] + [pad], axis=1).astype(BF16)
    w_b = jnp.concatenate([cols[n] for n in ("mq", "fq", "fk", "fv")], axis=1).astype(BF16)
    off_a = {"g_rwkv": 0, "r": c, "g_mq": 4 * c, "g_fox": 4 * c + mw, "lora": 4 * c + mw + fw}
    off_b = {"mq": 0, "fq": mw, "fk": mw + fw, "fv": mw + 2 * fw}

    x2 = x.reshape(b * t, d)
    h = _rmsnorm(x2, g_pre, BF16)
    pa3 = _matmul([h], [w_a], F32, 1024, 1024, "in_proj_a").reshape(b, t, -1)
    pb3 = _matmul([h], [w_b], BF16, 1024, 512, "in_proj_b").reshape(b, t, -1)

    prep = _rwkv_prep(pa3, off_a["r"], off_a["lora"], mu_rwkv, w0, w_decay_up, a0, w_iclr_up, k_k, k_a, c)
    y_rwkv = _rwkv_scan(prep, pa3, off_a["g_rwkv"], r_k.reshape(-1), ln_x_w, ln_x_b)

    c4 = _fox_prefix(pa3, off_a["lora"], b_f).reshape(b, fh // 2, 2, t)
    y_fox = _fox_attn(pb3, off_b["fq"], off_b["fk"], off_b["fv"], c4, pa3, off_a["g_fox"], fw)

    hm = _rmsnorm(mem.reshape(b * n_mem, d), g_mem, BF16)
    mkv3 = _matmul([hm], [w_mem_kv.astype(BF16)], BF16, 512, 1024, "mem_kv_proj").reshape(b, n_mem, 2 * mw)
    y_mem = _mem_attn(pb3, off_b["mq"], mkv3, pa3, off_a["g_mq"], mw)

    w_o = w_out.astype(BF16)
    z = _matmul([y_rwkv.reshape(b * t, c), y_fox.reshape(b * t, fw), y_mem.reshape(b * t, mw)],
                [w_o[:c], w_o[c:c + fw], w_o[c + fw:]], F32, 1024, 1024, "out_proj")
    return _post(x2, z, g_post).reshape(b, t, d)


def kernel(x, mem, g_pre, w_in, mu_rwkv, w0, w_decay_up, a0, w_iclr_up, k_k, k_a, r_k, ln_x_w, ln_x_b, b_f,
           g_mem, w_mem_kv, w_out, g_post):
    for l in range(g_pre.shape[0]):
        x = _layer(x, mem, g_pre[l], w_in[l], mu_rwkv[l], w0[l], w_decay_up[l], a0[l], w_iclr_up[l], k_k[l],
                   k_a[l], r_k[l], ln_x_w[l], ln_x_b[l], b_f[l], g_mem[l], w_mem_kv[l], w_out[l], g_post[l])
    return x
```

```python
import functools

import jax
import jax.numpy as jnp
from jax import lax
from jax.experimental import pallas as pl
from jax.experimental.pallas import tpu as pltpu

HEAD_DIM = 64
MEM_HEADS = 4
RMS_EPS = 1e-6
GN_EPS = 64e-5

LANES = 128
UNIT_HEADS = 4
UNIT = UNIT_HEADS * HEAD_DIM
RWKV_CHUNK = 64
LORA_BLOCK = 512
VMEM_LIMIT_BYTES = 56 * 1024 * 1024

F32 = jnp.float32
BF16 = jnp.bfloat16

_NN = (((1,), (0,)), ((), ()))
_NT = (((1,), (1,)), ((), ()))
_TN = (((0,), (0,)), ((), ()))


def _dot(a, b, dims=_NN):
    return lax.dot_general(a, b, dims, preferred_element_type=F32)


def _split_bf16(x, n):
    parts = []
    rem = x
    for _ in range(n):
        p = rem.astype(BF16)
        parts.append(p)
        rem = rem - p.astype(F32)
    return parts


def _dot_split(a, b_bf16, n, dims=_NN):
    acc = None
    for p in _split_bf16(a, n):
        d = _dot(p, b_bf16, dims)
        acc = d if acc is None else acc + d
    return acc


def _dot_f32(a, b):
    a_hi, a_lo = _split_bf16(a, 2)
    b_hi, b_lo = _split_bf16(b, 2)
    return _dot(a_hi, b_hi) + (_dot(a_lo, b_hi) + _dot(a_hi, b_lo))


def _sigmoid(x):
    return 1.0 / (1.0 + jnp.exp(-x))


def _softplus(x):
    return jnp.maximum(x, 0.0) + jnp.log1p(jnp.exp(-jnp.abs(x)))


def _params(*semantics):
    return pltpu.CompilerParams(dimension_semantics=semantics, vmem_limit_bytes=VMEM_LIMIT_BYTES)


def _rmsnorm_kernel(x_ref, g_ref, o_ref):
    x = x_ref[...]
    ms = jnp.mean(x * x, axis=-1, keepdims=True)
    o_ref[...] = (x * lax.rsqrt(ms + RMS_EPS) * g_ref[...]).astype(o_ref.dtype)


def _rmsnorm(x2d, g, out_dtype, tm=256):
    m, d = x2d.shape
    tm = min(tm, m)
    assert m % tm == 0
    return pl.pallas_call(
        _rmsnorm_kernel,
        grid=(m // tm,),
        in_specs=[pl.BlockSpec((tm, d), lambda i: (i, 0)), pl.BlockSpec((1, d), lambda i: (0, 0))],
        out_specs=pl.BlockSpec((tm, d), lambda i: (i, 0)),
        out_shape=jax.ShapeDtypeStruct((m, d), out_dtype),
        compiler_params=_params("parallel"),
        name="rmsnorm",
    )(x2d, g.reshape(1, d))


def _matmul_kernel(*refs, n_in):
    o_ref = refs[2 * n_in]
    acc = None
    for x_ref, w_ref in zip(refs[:n_in], refs[n_in:2 * n_in]):
        d = _dot(x_ref[...], w_ref[...])
        acc = d if acc is None else acc + d
    o_ref[...] = acc.astype(o_ref.dtype)


def _matmul(xs, ws, out_dtype, tm, tn, name):
    m = xs[0].shape[0]
    n = ws[0].shape[1]
    tm = min(tm, m)
    tn = min(tn, n)
    assert m % tm == 0 and n % tn == 0, (m, tm, n, tn)
    in_specs = [pl.BlockSpec((tm, x.shape[1]), lambda j, i: (i, 0)) for x in xs]
    in_specs += [pl.BlockSpec((w.shape[0], tn), lambda j, i: (0, j)) for w in ws]
    return pl.pallas_call(
        functools.partial(_matmul_kernel, n_in=len(xs)),
        grid=(n // tn, m // tm),
        in_specs=in_specs,
        out_specs=pl.BlockSpec((tm, tn), lambda j, i: (i, j)),
        out_shape=jax.ShapeDtypeStruct((m, n), out_dtype),
        compiler_params=_params("parallel", "parallel"),
        name=name,
    )(*xs, *ws)


def _matmul_t_kernel(wt_ref, x_ref, o_ref):
    o_ref[...] = _dot(wt_ref[...], x_ref[...], _NT).astype(o_ref.dtype)


def _matmul_t(wt, x, out_dtype, tm, name):
    n, k = wt.shape
    m = x.shape[0]
    tm = min(tm, m)
    assert m % tm == 0
    return pl.pallas_call(
        _matmul_t_kernel,
        grid=(m // tm,),
        in_specs=[pl.BlockSpec((n, k), lambda i: (0, 0)), pl.BlockSpec((tm, k), lambda i: (i, 0))],
        out_specs=pl.BlockSpec((n, tm), lambda i: (0, i)),
        out_shape=jax.ShapeDtypeStruct((n, m), out_dtype),
        compiler_params=_params("parallel"),
        name=name,
    )(wt, x)


def _post_kernel(x_ref, z_ref, g_ref, o_ref):
    z = z_ref[...]
    ms = jnp.mean(z * z, axis=-1, keepdims=True)
    o_ref[...] = x_ref[...] + z * lax.rsqrt(ms + RMS_EPS) * g_ref[...]


def _post(x2d, z2d, g, tm=256):
    m, d = x2d.shape
    tm = min(tm, m)
    assert m % tm == 0
    row = pl.BlockSpec((tm, d), lambda i: (i, 0))
    return pl.pallas_call(
        _post_kernel,
        grid=(m // tm,),
        in_specs=[row, row, pl.BlockSpec((1, d), lambda i: (0, 0))],
        out_specs=row,
        out_shape=jax.ShapeDtypeStruct((m, d), x2d.dtype),
        compiler_params=_params("parallel"),
        name="post_norm_residual",
    )(x2d, z2d, g.reshape(1, d))


def _head_ones(n):
    row = lax.broadcasted_iota(jnp.int32, (n, n), 0) // HEAD_DIM
    col = lax.broadcasted_iota(jnp.int32, (n, n), 1) // HEAD_DIM
    return jnp.where(row == col, 1.0, 0.0).astype(BF16)


def _rwkv_prep_kernel(r_ref, k_ref, v_ref, lora_ref, mu_r_ref, mu_k_ref, mu_v_ref, mu_lora_ref,
                      w0_ref, wdu_ref, a0_ref, wiu_ref, kk_ref, ka_ref,
                      r_o, k_o, v_o, lw_o, a_o, b_o,
                      carry_r, carry_k, carry_v, carry_lora):
    @pl.when(pl.program_id(2) == 0)
    def _():
        for c in (carry_r, carry_k, carry_v, carry_lora):
            c[...] = jnp.zeros_like(c)

    tm = r_ref.shape[1]
    first_row = lax.broadcasted_iota(jnp.int32, (tm, 1), 0) == 0

    def shift(x, carry_ref, mu):
        prev = jnp.where(first_row, carry_ref[...], pltpu.roll(x, 1, 0))
        carry_ref[...] = x[tm - 1:tm, :]
        return x + (prev - x) * mu

    r = shift(r_ref[0], carry_r, mu_r_ref[...])
    k = shift(k_ref[0], carry_k, mu_k_ref[...])
    v = shift(v_ref[0], carry_v, mu_v_ref[...])
    lora = shift(lora_ref[0][:, :2 * LANES], carry_lora, mu_lora_ref[...])
    wl = lora[:, :LANES]
    al = lora[:, LANES:]

    w_pre = -_softplus(-(w0_ref[...] + _dot_f32(jnp.tanh(wl), wdu_ref[...]))) - 0.5
    log_decay = -jnp.exp(w_pre)
    alpha = _sigmoid(a0_ref[...] + _dot_f32(al, wiu_ref[...]))

    kk = k * kk_ref[...]
    ss = _dot_split(kk * kk, _head_ones(LANES), 2)
    kk = kk * lax.rsqrt(jnp.maximum(ss, 1e-24))

    r_o[0] = r
    k_o[0] = k * (1.0 + (alpha - 1.0) * ka_ref[...])
    v_o[0] = v
    lw_o[0] = log_decay
    a_o[0] = -kk
    b_o[0] = kk * alpha


def _rwkv_prep(pa3, off_r, off_lora, mu, w0, wdu, a0, wiu, k_k, k_a, c, tm=512):
    b, t, _ = pa3.shape
    tm = min(tm, t)
    assert t % tm == 0 and c % LANES == 0 and off_r % LANES == 0 and off_lora % LORA_BLOCK == 0
    nslab = c // LANES
    rb, lb = off_r // LANES, off_lora // LORA_BLOCK

    def col(base):
        return pl.BlockSpec((1, tm, LANES), lambda bi, s, i: (bi, i, base + s))

    def vec(base=0):
        return pl.BlockSpec((1, LANES), lambda bi, s, i: (0, base + s))

    lora_w = pl.BlockSpec((wdu.shape[0], LANES), lambda bi, s, i: (0, s))
    out = pl.BlockSpec((1, tm, LANES), lambda bi, s, i: (bi, i, s))
    mu_rkv = mu[:3 * c].reshape(1, 3 * c)
    mu_lora = mu[3 * c:].reshape(1, 2 * LANES)
    row = lambda z: z.reshape(1, c)
    return pl.pallas_call(
        _rwkv_prep_kernel,
        grid=(b, nslab, t // tm),
        in_specs=[col(rb), col(rb + nslab), col(rb + 2 * nslab),
                  pl.BlockSpec((1, tm, LORA_BLOCK), lambda bi, s, i: (bi, i, lb)),
                  vec(0), vec(nslab), vec(2 * nslab),
                  pl.BlockSpec((1, 2 * LANES), lambda bi, s, i: (0, 0)),
                  vec(), lora_w, vec(), lora_w, vec(), vec()],
        out_specs=[out] * 6,
        out_shape=[jax.ShapeDtypeStruct((b, t, c), F32)] * 6,
        scratch_shapes=[pltpu.VMEM((1, LANES), F32)] * 3 + [pltpu.VMEM((1, 2 * LANES), F32)],
        compiler_params=_params("parallel", "parallel", "arbitrary"),
        name="rwkv_prep",
    )(pa3, pa3, pa3, pa3, mu_rkv, mu_rkv, mu_rkv, mu_lora,
      row(w0), wdu, row(a0), wiu, row(k_k), row(k_a))


def _rwkv_scan_kernel(r_ref, k_ref, v_ref, lw_ref, a_ref, b_ref, g_ref, rk_ref, lnw_ref, lnb_ref,
                      o_ref, s_ref, *, n_units):
    @pl.when(pl.program_id(2) == 0)
    def _():
        s_ref[...] = jnp.zeros_like(s_ref)

    L = r_ref.shape[1]
    hl = UNIT_HEADS * L
    n_sq = L.bit_length() - 2
    assert 2 ** (n_sq + 1) == L
    units = range(n_units)

    def blk(shape, d0, d1):
        return (lax.broadcasted_iota(jnp.int32, shape, 0) // d0) == (lax.broadcasted_iota(jnp.int32, shape, 1) // d1)

    t_row = lax.broadcasted_iota(jnp.int32, (hl, hl), 0)
    t_col = lax.broadcasted_iota(jnp.int32, (hl, hl), 1)
    same = blk((hl, hl), L, L)
    strict = same & (t_col < t_row)
    incl = same & (t_col <= t_row)
    eye = jnp.where(t_row == t_col, 1.0, 0.0)
    bd_rows = blk((hl, UNIT), L, HEAD_DIM)
    bd_state = blk((UNIT, UNIT), HEAD_DIM, HEAD_DIM)
    ones_bd = jnp.where(bd_state, 1.0, 0.0).astype(BF16)
    tril = jnp.where(lax.broadcasted_iota(jnp.int32, (L, L), 0) >= lax.broadcasted_iota(jnp.int32, (L, L), 1),
                     1.0, 0.0).astype(BF16)

    def rep(x):
        return jnp.concatenate([x] * UNIT_HEADS, axis=0)

    def stack(x):
        return jnp.where(bd_rows, rep(x), 0.0)

    def unstack(xs):
        out = xs[0:L]
        for h in range(1, UNIT_HEADS):
            out = out + xs[h * L:(h + 1) * L]
        return out

    def headsum(x):
        return _dot_split(x, ones_bd, 2)

    def lanes(ref, u):
        return ref[0, :, u * UNIT:(u + 1) * UNIT]

    lw_all = lw_ref[0]
    lw_hi, lw_lo = _split_bf16(lw_all, 2)
    cl_all = _dot(tril, lw_hi) + _dot(tril, lw_lo)

    ah, rh, e_last, lhs, rhs_b, rhs_k, upd_rhs = [], [], [], [], [], [], []
    for u in units:
        sl = slice(u * UNIT, (u + 1) * UNIT)
        cl = cl_all[:, sl]
        e_pos = jnp.exp(cl)
        e_neg = jnp.exp(-cl)
        e_rem = jnp.exp(cl[L - 1:L, :] - cl)
        a, b, k = lanes(a_ref, u), lanes(b_ref, u), lanes(k_ref, u)
        ah.append(a * jnp.exp(cl - lw_all[:, sl]))
        rh.append(lanes(r_ref, u) * e_pos)
        e_last.append(e_pos[L - 1:L, :])
        lhs.append(jnp.concatenate([stack(ah[u]), stack(rh[u])], axis=0).astype(BF16))
        rhs_b.append(rep(b * e_neg).astype(BF16))
        rhs_k.append(rep(k * e_neg).astype(BF16))
        upd_rhs.append(jnp.concatenate([b * e_rem, k * e_rem], axis=0).astype(BF16))

    a_b = [_dot(lhs[u], rhs_b[u], _NT) for u in units]
    a_k = [_dot(lhs[u], rhs_k[u], _NT) for u in units]
    pw = [jnp.where(strict, a_b[u][:hl], 0.0) for u in units]
    a_ak = [jnp.where(strict, a_k[u][:hl], 0.0).astype(BF16) for u in units]
    a_rb = [jnp.where(incl, a_b[u][hl:], 0.0).astype(BF16) for u in units]
    a_rk = [jnp.where(incl, a_k[u][hl:], 0.0).astype(BF16) for u in units]

    t_inv = [eye + pw[u] for u in units]
    for _ in range(n_sq):
        pw = [_dot(pw[u].astype(BF16), pw[u].astype(BF16)) for u in units]
        t_inv = [t_inv[u] + _dot(t_inv[u].astype(BF16), pw[u].astype(BF16)) for u in units]

    s0 = [s_ref[u] for u in units]
    xr = [_dot(jnp.concatenate([ah[u], rh[u]], axis=0).astype(BF16), s0[u].astype(BF16), _NT) for u in units]
    v = [lanes(v_ref, u) for u in units]
    v_s = [stack(v[u]).astype(BF16) for u in units]
    x_s = [stack(xr[u][:L]) + _dot(a_ak[u], v_s[u]) for u in units]
    u_s = [_dot(t_inv[u].astype(BF16), x_s[u].astype(BF16)) for u in units]
    y_s = [stack(xr[u][L:]) + _dot(a_rb[u], u_s[u].astype(BF16)) + _dot(a_rk[u], v_s[u]) for u in units]
    for u in units:
        upd = _dot(jnp.concatenate([unstack(u_s[u]), v[u]], axis=0).astype(BF16), upd_rhs[u], _TN)
        s_ref[u] = jnp.where(bd_state, s0[u] * e_last[u] + upd, 0.0)

    inv_n = 1.0 / HEAD_DIM
    y = [unstack(y_s[u]) for u in units]
    d = [y[u] - headsum(y[u]) * inv_n for u in units]
    var = [headsum(d[u] * d[u]) * inv_n for u in units]
    bonus = [headsum(lanes(r_ref, u) * lanes(k_ref, u) * rk_ref[:, u * UNIT:(u + 1) * UNIT]) * v[u] for u in units]
    for u in units:
        sl = slice(u * UNIT, (u + 1) * UNIT)
        yn = d[u] * lax.rsqrt(var[u] + GN_EPS) * lnw_ref[:, sl] + lnb_ref[:, sl]
        g = lanes(g_ref, u)
        o_ref[0, :, sl] = ((yn + bonus[u]) * (g * _sigmoid(g))).astype(o_ref.dtype)


def _rwkv_scan(prep, pa3, off_g, r_k, ln_w, ln_b, n_units=6):
    b, t, c = prep[0].shape
    L = RWKV_CHUNK
    w = n_units * UNIT
    assert t % L == 0 and c % w == 0 and off_g % w == 0
    gb = off_g // w
    blk = pl.BlockSpec((1, L, w), lambda bi, gi, ci: (bi, ci, gi))
    vec = pl.BlockSpec((1, w), lambda bi, gi, ci: (0, gi))
    return pl.pallas_call(
        functools.partial(_rwkv_scan_kernel, n_units=n_units),
        grid=(b, c // w, t // L),
        in_specs=[blk] * 6 + [pl.BlockSpec((1, L, w), lambda bi, gi, ci: (bi, ci, gb + gi)), vec, vec, vec],
        out_specs=blk,
        out_shape=jax.ShapeDtypeStruct((b, t, c), BF16),
        scratch_shapes=[pltpu.VMEM((n_units, UNIT, UNIT), F32)],
        compiler_params=_params("parallel", "parallel", "arbitrary"),
        name="rwkv_scan",
    )(*prep, pa3, r_k.reshape(1, c), ln_w.reshape(1, c), ln_b.reshape(1, c))


LOG2E = 1.4426950408889634
FOX_BIAS_PARTS = 3


def _fox_prefix_kernel(f_ref, bf_ref, o_ref, carry_ref):
    @pl.when(pl.program_id(1) == 0)
    def _():
        carry_ref[...] = jnp.zeros_like(carry_ref)

    x = f_ref[0][:, 2 * LANES:3 * LANES] + bf_ref[...]
    log_f = -_softplus(-x)
    tc = x.shape[0]
    tril = jnp.where(lax.broadcasted_iota(jnp.int32, (tc, tc), 0) >= lax.broadcasted_iota(jnp.int32, (tc, tc), 1),
                     1.0, 0.0).astype(BF16)
    c = carry_ref[...]
    for part in _split_bf16(log_f, 3):
        c = c + _dot(tril, part)
    carry_ref[...] = c[tc - 1:tc, :]

    fw = o_ref.shape[2]
    src = lax.broadcasted_iota(jnp.int32, (LANES, fw), 0)
    dst = lax.broadcasted_iota(jnp.int32, (LANES, fw), 1)
    lane = dst % LANES
    head = 2 * (dst // LANES) + jnp.where(lane < HEAD_DIM, 1, 0)
    out = None
    for i, part in enumerate(_split_bf16(c * LOG2E, FOX_BIAS_PARTS)):
        sel = jnp.where((src == head) & (lane % HEAD_DIM == i), 1.0, 0.0).astype(BF16)
        d = _dot(part, sel)
        out = d if out is None else out + d
    o_ref[0] = out.astype(o_ref.dtype)


def _fox_prefix(pa3, off_lora, b_f, fw, tc=512):
    b, t, _ = pa3.shape
    fh = b_f.shape[0]
    tc = min(tc, t)
    assert t % tc == 0 and fh <= LANES and fw == fh * HEAD_DIM
    bf = jnp.zeros((1, LANES), F32).at[0, :fh].set(b_f)
    lb = off_lora // LORA_BLOCK
    return pl.pallas_call(
        _fox_prefix_kernel,
        grid=(b, t // tc),
        in_specs=[pl.BlockSpec((1, tc, LORA_BLOCK), lambda bi, i: (bi, i, lb)),
                  pl.BlockSpec((1, LANES), lambda bi, i: (0, 0))],
        out_specs=pl.BlockSpec((1, tc, fw), lambda bi, i: (bi, i, 0)),
        out_shape=jax.ShapeDtypeStruct((b, t, fw), BF16),
        scratch_shapes=[pltpu.VMEM((1, LANES), F32)],
        compiler_params=_params("parallel", "arbitrary"),
        name="fox_prefix",
    )(pa3, bf)


def _fox_attn_kernel(q_ref, k_ref, kb_ref, vt_ref, g_ref, o_ref, *, tk):
    qi = pl.program_id(2)
    tq = q_ref.shape[1]
    n_pairs = q_ref.shape[2] // LANES
    n_sub = tq // tk
    k_head0 = lax.broadcasted_iota(jnp.int32, (tk, LANES), 1) < HEAD_DIM
    v_head0 = lax.broadcasted_iota(jnp.int32, (LANES, tk), 0) < HEAD_DIM
    q_lane = lax.broadcasted_iota(jnp.int32, (tq, LANES), 1)
    q_head0 = q_lane < HEAD_DIM
    bias0 = jnp.where((q_lane >= HEAD_DIM) & (q_lane < HEAD_DIM + FOX_BIAS_PARTS), -1.0, 0.0)
    bias1 = jnp.where(q_lane < FOX_BIAS_PARTS, -1.0, 0.0)
    ones = jnp.ones((LANES, tk), BF16)

    qs = []
    for pr in range(n_pairs):
        q = q_ref[0, :, pr * LANES:(pr + 1) * LANES].astype(F32) * (HEAD_DIM ** -0.5 * LOG2E)
        qs += [jnp.where(q_head0, q, bias0).astype(BF16), jnp.where(q_head0, bias1, q).astype(BF16)]

    def tile(j, carry, valid):
        start = pl.multiple_of(j * tk, tk)
        scores = []
        for pr in range(n_pairs):
            sl = slice(pr * LANES, (pr + 1) * LANES)
            k = k_ref[0, pl.ds(start, tk), sl]
            kb = kb_ref[0, pl.ds(start, tk), sl]
            for h in range(2):
                k_aug = jnp.where(k_head0, k, kb) if h == 0 else jnp.where(k_head0, kb, k)
                s = _dot(k_aug, qs[2 * pr + h], _NT)
                scores.append(s if valid is None else jnp.where(valid, s, -jnp.inf))
        probs = []
        for ci, s in enumerate(scores):
            m = carry[ci][0]
            m_new = jnp.maximum(m, jnp.max(s, axis=0, keepdims=True))
            probs.append((m_new, jnp.exp2(m - m_new), jnp.exp2(s - m_new).astype(BF16)))
        new = []
        for ci, (m_new, alpha, p) in enumerate(probs):
            pr, h = divmod(ci, 2)
            vt = vt_ref[pr * LANES:(pr + 1) * LANES, pl.ds(start, tk)]
            v_aug = jnp.where(v_head0, vt, ones) if h == 0 else jnp.where(v_head0, ones, vt)
            new.append((m_new, alpha * carry[ci][1] + _dot(v_aug, p)))
        return tuple(new)

    init = tuple((jnp.full((1, tq), -jnp.inf, F32), jnp.zeros((LANES, tq), F32)) for _ in range(2 * n_pairs))
    carry = lax.fori_loop(0, qi * n_sub, lambda j, c: tile(j, c, None), init)
    key = lax.broadcasted_iota(jnp.int32, (tk, tq), 0)
    qry = lax.broadcasted_iota(jnp.int32, (tk, tq), 1)
    for d in range(n_sub):
        carry = tile(qi * n_sub + d, carry, key + d * tk <= qry)

    for pr in range(n_pairs):
        sl = slice(pr * LANES, (pr + 1) * LANES)
        acc0, acc1 = carry[2 * pr][1], carry[2 * pr + 1][1]
        out_t = jnp.concatenate([acc0[:HEAD_DIM] * (1.0 / acc0[HEAD_DIM:]),
                                 acc1[HEAD_DIM:] * (1.0 / acc1[:HEAD_DIM])], axis=0)
        g = g_ref[0, :, sl]
        o_ref[0, :, sl] = (out_t.T * (g * _sigmoid(g))).astype(o_ref.dtype)


def _fox_attn(pb3, off_q, off_k, kb3, vt, pa3, off_g, fw, tq=256, tk=256, n_pairs=4):
    b, t, _ = pb3.shape
    tq = min(tq, t)
    tk = min(tk, tq)
    w = n_pairs * LANES
    assert t % tq == 0 and tq % tk == 0 and fw % w == 0
    assert off_q % w == 0 and off_k % w == 0 and off_g % w == 0
    qb, kb, gb = off_q // w, off_k // w, off_g // w
    return pl.pallas_call(
        functools.partial(_fox_attn_kernel, tk=tk),
        grid=(b, fw // w, t // tq),
        in_specs=[pl.BlockSpec((1, tq, w), lambda bi, p, i: (bi, i, qb + p)),
                  pl.BlockSpec((1, t, w), lambda bi, p, i: (bi, 0, kb + p)),
                  pl.BlockSpec((1, t, w), lambda bi, p, i: (bi, 0, p)),
                  pl.BlockSpec((w, t), lambda bi, p, i: (p, bi)),
                  pl.BlockSpec((1, tq, w), lambda bi, p, i: (bi, i, gb + p))],
        out_specs=pl.BlockSpec((1, tq, w), lambda bi, p, i: (bi, i, p)),
        out_shape=jax.ShapeDtypeStruct((b, t, fw), BF16),
        compiler_params=_params("parallel", "parallel", "arbitrary"),
        name="fox_attn",
    )(pb3, pb3, kb3, vt, pa3)


def _mem_attn_kernel(q_ref, kv_ref, g_ref, o_ref):
    mw = q_ref.shape[2]
    hd = mw // MEM_HEADS
    scale = hd ** -0.5
    outs = []
    for h in range(MEM_HEADS):
        q = q_ref[0, :, h * hd:(h + 1) * hd]
        mk = kv_ref[0, :, h * hd:(h + 1) * hd]
        mv = kv_ref[0, :, mw + h * hd:mw + (h + 1) * hd]
        s = _dot(q, mk, _NT) * scale
        p = jnp.exp(s - jnp.max(s, axis=1, keepdims=True))
        l = jnp.sum(p, axis=1, keepdims=True)
        outs.append(_dot(p.astype(BF16), mv) / l)
    g = g_ref[0]
    o_ref[0] = (jnp.concatenate(outs, axis=1) * (g * _sigmoid(g))).astype(o_ref.dtype)


def _mem_attn(pb3, off_q, mkv3, pa3, off_g, mw, tm=512):
    b, t, _ = pb3.shape
    tm = min(tm, t)
    assert t % tm == 0 and off_q % mw == 0 and off_g % mw == 0 and (mw // MEM_HEADS) % LANES == 0
    n_mem = mkv3.shape[1]
    qb, gb = off_q // mw, off_g // mw
    return pl.pallas_call(
        _mem_attn_kernel,
        grid=(b, t // tm),
        in_specs=[pl.BlockSpec((1, tm, mw), lambda bi, i: (bi, i, qb)),
                  pl.BlockSpec((1, n_mem, 2 * mw), lambda bi, i: (bi, 0, 0)),
                  pl.BlockSpec((1, tm, mw), lambda bi, i: (bi, i, gb))],
        out_specs=pl.BlockSpec((1, tm, mw), lambda bi, i: (bi, i, 0)),
        out_shape=jax.ShapeDtypeStruct((b, t, mw), BF16),
        compiler_params=_params("parallel", "parallel"),
        name="mem_attn",
    )(pb3, mkv3, pa3)


def _layer(x, mem, g_pre, w_in, mu_rwkv, w0, w_decay_up, a0, w_iclr_up, k_k, k_a, r_k,
           ln_x_w, ln_x_b, b_f, g_mem, w_mem_kv, w_out, g_post):
    b, t, d = x.shape
    n_mem = mem.shape[1]
    c = w0.shape[0]
    fh = b_f.shape[0]
    fw = fh * HEAD_DIM
    mw = w_mem_kv.shape[1] // 2
    lora = w_decay_up.shape[0]
    assert lora == LANES and w_iclr_up.shape[0] == LANES
    assert w_in.shape[1] == 4 * c + 2 * lora + 4 * fw + fh + 2 * mw

    o = 0
    cols = {}
    for name, width in (("r", c), ("k", c), ("v", c), ("wl", lora), ("al", lora), ("g_rwkv", c), ("fq", fw),
                        ("fk", fw), ("fv", fw), ("f", fh), ("g_fox", fw), ("mq", mw), ("g_mq", mw)):
        cols[name] = w_in[:, o:o + width]
        o += width

    pad = jnp.zeros((d, LORA_BLOCK - 2 * lora - fh), w_in.dtype)
    names_a = ("g_rwkv", "r", "k", "v", "g_mq", "g_fox", "wl", "al", "f")
    w_a = jnp.concatenate([cols[n] for n in names_a] + [pad], axis=1).astype(BF16)
    w_b = jnp.concatenate([cols[n] for n in ("mq", "fq", "fk")], axis=1).astype(BF16)
    w_vt = cols["fv"].T.astype(BF16)
    off_a = {"g_rwkv": 0, "r": c, "g_mq": 4 * c, "g_fox": 4 * c + mw, "lora": 4 * c + mw + fw}
    off_b = {"mq": 0, "fq": mw, "fk": mw + fw}

    x2 = x.reshape(b * t, d)
    h = _rmsnorm(x2, g_pre, BF16)
    pa3 = _matmul([h], [w_a], F32, 1024, 1024, "in_proj_a").reshape(b, t, -1)
    pb3 = _matmul([h], [w_b], BF16, 1024, 1024, "in_proj_b").reshape(b, t, -1)
    vt = _matmul_t(w_vt, h, BF16, 512, "in_proj_vt")

    prep = _rwkv_prep(pa3, off_a["r"], off_a["lora"], mu_rwkv, w0, w_decay_up, a0, w_iclr_up, k_k, k_a, c)
    y_rwkv = _rwkv_scan(prep, pa3, off_a["g_rwkv"], r_k.reshape(-1), ln_x_w, ln_x_b)

    kb3 = _fox_prefix(pa3, off_a["lora"], b_f, fw)
    y_fox = _fox_attn(pb3, off_b["fq"], off_b["fk"], kb3, vt, pa3, off_a["g_fox"], fw)

    hm = _rmsnorm(mem.reshape(b * n_mem, d), g_mem, BF16)
    mkv3 = _matmul([hm], [w_mem_kv.astype(BF16)], BF16, 512, 1024, "mem_kv_proj").reshape(b, n_mem, 2 * mw)
    y_mem = _mem_attn(pb3, off_b["mq"], mkv3, pa3, off_a["g_mq"], mw)

    w_o = w_out.astype(BF16)
    z = _matmul([y_rwkv.reshape(b * t, c), y_fox.reshape(b * t, fw), y_mem.reshape(b * t, mw)],
                [w_o[:c], w_o[c:c + fw], w_o[c + fw:]], F32, 1024, 1024, "out_proj")
    return _post(x2, z, g_post).reshape(b, t, d)


def kernel(x, mem, g_pre, w_in, mu_rwkv, w0, w_decay_up, a0, w_iclr_up, k_k, k_a, r_k, ln_x_w, ln_x_b, b_f,
           g_mem, w_mem_kv, w_out, g_post):
    for l in range(g_pre.shape[0]):
        x = _layer(x, mem, g_pre[l], w_in[l], mu_rwkv[l], w0[l], w_decay_up[l], a0[l], w_iclr_up[l], k_k[l],
                   k_a[l], r_k[l], ln_x_w[l], ln_x_b[l], b_f[l], g_mem[l], w_mem_kv[l], w_out[l], g_post[l])
    return x
```

```python
import functools

import jax
import jax.numpy as jnp
from jax import lax
from jax.experimental import pallas as pl
from jax.experimental.pallas import tpu as pltpu

HEAD_DIM = 64
MEM_HEADS = 4
RMS_EPS = 1e-6
GN_EPS = 64e-5
LOG2E = 1.4426950408889634

LANES = 128
MXU = 256
UNIT_HEADS = MXU // HEAD_DIM
UNIT = UNIT_HEADS * HEAD_DIM
RWKV_CHUNK = 64
WIDE = 512
VMEM_LIMIT_BYTES = 56 * 1024 * 1024
FOX_BIAS_PARTS = 3

ROWS_ELEMENTWISE = 256
ROWS_PROJ = 1024
ROWS_PROJ_T = 512
ROWS_PREP = 512
ROWS_PREFIX = 512
ROWS_MEM = 512
FOX_TQ = 512
FOX_TK = 256
FOX_PAIRS = 4
RWKV_UNITS = 6

F32 = jnp.float32
BF16 = jnp.bfloat16

_NN = (((1,), (0,)), ((), ()))
_NT = (((1,), (1,)), ((), ()))
_TN = (((0,), (0,)), ((), ()))


def _dot(a, b, dims=_NN):
    return lax.dot_general(a, b, dims, preferred_element_type=F32)


def _split_bf16(x, n):
    parts = []
    rem = x
    for _ in range(n):
        p = rem.astype(BF16)
        parts.append(p)
        rem = rem - p.astype(F32)
    return parts


def _dot_split(a, b_bf16, n, dims=_NN):
    acc = None
    for p in _split_bf16(a, n):
        d = _dot(p, b_bf16, dims)
        acc = d if acc is None else acc + d
    return acc


def _dot_f32(a, b):
    a_hi, a_lo = _split_bf16(a, 2)
    b_hi, b_lo = _split_bf16(b, 2)
    return _dot(a_hi, b_hi) + (_dot(a_lo, b_hi) + _dot(a_hi, b_lo))


def _sigmoid(x):
    return 1.0 / (1.0 + jnp.exp(-x))


def _softplus(x):
    return jnp.maximum(x, 0.0) + jnp.log1p(jnp.exp(-jnp.abs(x)))


def _params(*semantics):
    return pltpu.CompilerParams(dimension_semantics=semantics, vmem_limit_bytes=VMEM_LIMIT_BYTES)


def _rmsnorm_kernel(x_ref, g_ref, o_ref):
    x = x_ref[...]
    ms = jnp.mean(x * x, axis=-1, keepdims=True)
    o_ref[...] = (x * lax.rsqrt(ms + RMS_EPS) * g_ref[...]).astype(o_ref.dtype)


def _rmsnorm(x2d, g, out_dtype):
    m, d = x2d.shape
    tm = min(ROWS_ELEMENTWISE, m)
    assert m % tm == 0
    return pl.pallas_call(
        _rmsnorm_kernel,
        grid=(m // tm,),
        in_specs=[pl.BlockSpec((tm, d), lambda i: (i, 0)), pl.BlockSpec((1, d), lambda i: (0, 0))],
        out_specs=pl.BlockSpec((tm, d), lambda i: (i, 0)),
        out_shape=jax.ShapeDtypeStruct((m, d), out_dtype),
        compiler_params=_params("parallel"),
        name="rmsnorm",
    )(x2d, g.reshape(1, d))


def _proj_kernel(*refs, n_x, placement):
    xs = refs[:n_x]
    ws = refs[n_x:n_x + len(placement)]
    o_ref, wb_ref = refs[n_x + len(placement):]

    @pl.when(pl.program_id(1) == 0)
    def _():
        for w_ref, (r0, c0) in zip(ws, placement):
            wb_ref[r0:r0 + w_ref.shape[0], c0:c0 + w_ref.shape[1]] = w_ref[...].astype(BF16)

    acc = None
    r0 = 0
    for x_ref in xs:
        kx = x_ref.shape[1]
        d = _dot(x_ref[...], wb_ref[r0:r0 + kx, :])
        acc = d if acc is None else acc + d
        r0 += kx
    o_ref[...] = acc.astype(o_ref.dtype)


def _proj(xs, weights, n_cols, tn, out_dtype, name):
    m = xs[0].shape[0]
    k_total = sum(x.shape[1] for x in xs)
    tm = min(ROWS_PROJ, m)
    tn = min(tn, n_cols)
    assert m % tm == 0 and n_cols % tn == 0
    in_specs = [pl.BlockSpec((tm, x.shape[1]), lambda j, i: (i, 0)) for x in xs]
    for _, shape, index_fn, _ in weights:
        in_specs.append(pl.BlockSpec(shape, lambda j, i, f=index_fn: f(j)))
    return pl.pallas_call(
        functools.partial(_proj_kernel, n_x=len(xs), placement=tuple(p for _, _, _, p in weights)),
        grid=(n_cols // tn, m // tm),
        in_specs=in_specs,
        out_specs=pl.BlockSpec((tm, tn), lambda j, i: (i, j)),
        out_shape=jax.ShapeDtypeStruct((m, n_cols), out_dtype),
        scratch_shapes=[pltpu.VMEM((k_total, tn), BF16)],
        compiler_params=_params("parallel", "arbitrary"),
        name=name,
    )(*xs, *[w for w, _, _, _ in weights])


def _proj_t_kernel(wt_ref, x_ref, v0_ref, v1_ref):
    vt = _dot(wt_ref[...], x_ref[...], _NT)
    first = (lax.broadcasted_iota(jnp.int32, vt.shape, 0) % LANES) < HEAD_DIM
    v0_ref[...] = jnp.where(first, vt, 1.0).astype(v0_ref.dtype)
    v1_ref[...] = jnp.where(first, 1.0, vt).astype(v1_ref.dtype)


def _proj_t(wt, x):
    n, k = wt.shape
    m = x.shape[0]
    tm = min(ROWS_PROJ_T, m)
    assert m % tm == 0 and n % LANES == 0
    out = pl.BlockSpec((n, tm), lambda i: (0, i))
    return pl.pallas_call(
        _proj_t_kernel,
        grid=(m // tm,),
        in_specs=[pl.BlockSpec((n, k), lambda i: (0, 0)), pl.BlockSpec((tm, k), lambda i: (i, 0))],
        out_specs=[out, out],
        out_shape=[jax.ShapeDtypeStruct((n, m), BF16)] * 2,
        compiler_params=_params("parallel"),
        name="in_proj_vt",
    )(wt, x)


def _post_kernel(x_ref, z_ref, g_ref, o_ref):
    z = z_ref[...]
    ms = jnp.mean(z * z, axis=-1, keepdims=True)
    o_ref[...] = x_ref[...] + z * lax.rsqrt(ms + RMS_EPS) * g_ref[...]


def _post(x2d, z2d, g):
    m, d = x2d.shape
    tm = min(ROWS_ELEMENTWISE, m)
    assert m % tm == 0
    row = pl.BlockSpec((tm, d), lambda i: (i, 0))
    return pl.pallas_call(
        _post_kernel,
        grid=(m // tm,),
        in_specs=[row, row, pl.BlockSpec((1, d), lambda i: (0, 0))],
        out_specs=row,
        out_shape=jax.ShapeDtypeStruct((m, d), x2d.dtype),
        compiler_params=_params("parallel"),
        name="post_norm_residual",
    )(x2d, z2d, g.reshape(1, d))


def _head_ones(n):
    row = lax.broadcasted_iota(jnp.int32, (n, n), 0) // HEAD_DIM
    col = lax.broadcasted_iota(jnp.int32, (n, n), 1) // HEAD_DIM
    return jnp.where(row == col, 1.0, 0.0).astype(BF16)


def _rwkv_prep_kernel(r_ref, k_ref, v_ref, lora_ref, mu_r_ref, mu_k_ref, mu_v_ref, mu_lora_ref,
                      w0_ref, wdu_ref, a0_ref, wiu_ref, kk_ref, ka_ref,
                      r_o, k_o, v_o, lw_o, a_o, b_o,
                      carry_r, carry_k, carry_v, carry_lora):
    @pl.when(pl.program_id(2) == 0)
    def _():
        for c in (carry_r, carry_k, carry_v, carry_lora):
            c[...] = jnp.zeros_like(c)

    tm = r_ref.shape[1]
    first_row = lax.broadcasted_iota(jnp.int32, (tm, 1), 0) == 0

    def shift(x, carry_ref, mu):
        prev = jnp.where(first_row, carry_ref[...], pltpu.roll(x, 1, 0))
        carry_ref[...] = x[tm - 1:tm, :]
        return x + (prev - x) * mu

    r = shift(r_ref[0], carry_r, mu_r_ref[...])
    k = shift(k_ref[0], carry_k, mu_k_ref[...])
    v = shift(v_ref[0], carry_v, mu_v_ref[...])
    lora = shift(lora_ref[0][:, :2 * LANES], carry_lora, mu_lora_ref[...])
    wl = lora[:, :LANES]
    al = lora[:, LANES:]

    w_pre = -_softplus(-(w0_ref[...] + _dot_f32(jnp.tanh(wl), wdu_ref[...]))) - 0.5
    log_decay = -jnp.exp(w_pre)
    alpha = _sigmoid(a0_ref[...] + _dot_f32(al, wiu_ref[...]))

    kk = k * kk_ref[...]
    ss = _dot_split(kk * kk, _head_ones(kk.shape[1]), 2)
    kk = kk * lax.rsqrt(jnp.maximum(ss, 1e-24))

    r_o[0] = r
    k_o[0] = k * (1.0 + (alpha - 1.0) * ka_ref[...])
    v_o[0] = v
    lw_o[0] = log_decay
    a_o[0] = -kk
    b_o[0] = kk * alpha


def _rwkv_prep(pa3, off_r, off_lora, mu, w0, wdu, a0, wiu, k_k, k_a, c):
    b, t, _ = pa3.shape
    tm = min(ROWS_PREP, t)
    ws = min(WIDE, c)
    assert t % tm == 0 and c % ws == 0 and off_r % ws == 0 and off_lora % WIDE == 0
    nslab = c // ws
    rb, lb = off_r // ws, off_lora // WIDE

    def col(base):
        return pl.BlockSpec((1, tm, ws), lambda bi, s, i: (bi, i, base + s))

    def vec(base=0):
        return pl.BlockSpec((1, ws), lambda bi, s, i: (0, base + s))

    lora_w = pl.BlockSpec((wdu.shape[0], ws), lambda bi, s, i: (0, s))
    out = pl.BlockSpec((1, tm, ws), lambda bi, s, i: (bi, i, s))
    mu_rkv = mu[:3 * c].reshape(1, 3 * c)
    mu_lora = mu[3 * c:].reshape(1, 2 * LANES)
    row = lambda z: z.reshape(1, c)
    return pl.pallas_call(
        _rwkv_prep_kernel,
        grid=(b, nslab, t // tm),
        in_specs=[col(rb), col(rb + nslab), col(rb + 2 * nslab),
                  pl.BlockSpec((1, tm, WIDE), lambda bi, s, i: (bi, i, lb)),
                  vec(0), vec(nslab), vec(2 * nslab),
                  pl.BlockSpec((1, 2 * LANES), lambda bi, s, i: (0, 0)),
                  vec(), lora_w, vec(), lora_w, vec(), vec()],
        out_specs=[out] * 6,
        out_shape=[jax.ShapeDtypeStruct((b, t, c), F32)] * 6,
        scratch_shapes=[pltpu.VMEM((1, ws), F32)] * 3 + [pltpu.VMEM((1, 2 * LANES), F32)],
        compiler_params=_params("parallel", "parallel", "arbitrary"),
        name="rwkv_prep",
    )(pa3, pa3, pa3, pa3, mu_rkv, mu_rkv, mu_rkv, mu_lora,
      row(w0), wdu, row(a0), wiu, row(k_k), row(k_a))


def _rwkv_scan_kernel(*refs, n_units):
    r_ref, k_ref, v_ref, lw_ref, a_ref, b_ref = refs[:6]
    g_refs = refs[6:6 + n_units]
    rk_ref, lnw_ref, lnb_ref, o_ref, s_ref = refs[6 + n_units:]

    @pl.when(pl.program_id(2) == 0)
    def _():
        s_ref[...] = jnp.zeros_like(s_ref)

    L = r_ref.shape[1]
    hl = UNIT_HEADS * L
    n_sq = L.bit_length() - 2
    assert 2 ** (n_sq + 1) == L
    units = range(n_units)

    def blk(shape, d0, d1):
        return (lax.broadcasted_iota(jnp.int32, shape, 0) // d0) == (lax.broadcasted_iota(jnp.int32, shape, 1) // d1)

    t_row = lax.broadcasted_iota(jnp.int32, (hl, hl), 0)
    t_col = lax.broadcasted_iota(jnp.int32, (hl, hl), 1)
    same = blk((hl, hl), L, L)
    strict = same & (t_col < t_row)
    incl = same & (t_col <= t_row)
    eye = jnp.where(t_row == t_col, 1.0, 0.0)
    bd_rows = blk((hl, UNIT), L, HEAD_DIM)
    bd_state = blk((UNIT, UNIT), HEAD_DIM, HEAD_DIM)
    ones_bd = jnp.where(bd_state, 1.0, 0.0).astype(BF16)
    tril = jnp.where(lax.broadcasted_iota(jnp.int32, (L, L), 0) >= lax.broadcasted_iota(jnp.int32, (L, L), 1),
                     1.0, 0.0).astype(BF16)

    def rep(x):
        return jnp.concatenate([x] * UNIT_HEADS, axis=0)

    def stack(x):
        return jnp.where(bd_rows, rep(x), 0.0)

    def unstack(xs):
        out = xs[0:L]
        for h in range(1, UNIT_HEADS):
            out = out + xs[h * L:(h + 1) * L]
        return out

    def headsum(x):
        return _dot_split(x, ones_bd, 2)

    def lanes(ref, u):
        return ref[0, :, u * UNIT:(u + 1) * UNIT]

    lw_all = lw_ref[0]
    lw_hi, lw_lo = _split_bf16(lw_all, 2)
    cl_all = _dot(tril, lw_hi) + _dot(tril, lw_lo)

    ah, rh, e_last, lhs, rhs_b, rhs_k, upd_rhs = [], [], [], [], [], [], []
    for u in units:
        sl = slice(u * UNIT, (u + 1) * UNIT)
        cl = cl_all[:, sl]
        e_pos = jnp.exp(cl)
        e_neg = jnp.exp(-cl)
        e_rem = jnp.exp(cl[L - 1:L, :] - cl)
        a, b, k = lanes(a_ref, u), lanes(b_ref, u), lanes(k_ref, u)
        ah.append(a * jnp.exp(cl - lw_all[:, sl]))
        rh.append(lanes(r_ref, u) * e_pos)
        e_last.append(e_pos[L - 1:L, :])
        lhs.append(jnp.concatenate([stack(ah[u]), stack(rh[u])], axis=0).astype(BF16))
        rhs_b.append(rep(b * e_neg).astype(BF16))
        rhs_k.append(rep(k * e_neg).astype(BF16))
        upd_rhs.append(jnp.concatenate([b * e_rem, k * e_rem], axis=0).astype(BF16))

    a_b = [_dot(lhs[u], rhs_b[u], _NT) for u in units]
    a_k = [_dot(lhs[u], rhs_k[u], _NT) for u in units]
    pw = [jnp.where(strict, a_b[u][:hl], 0.0) for u in units]
    a_ak = [jnp.where(strict, a_k[u][:hl], 0.0).astype(BF16) for u in units]
    a_rb = [jnp.where(incl, a_b[u][hl:], 0.0).astype(BF16) for u in units]
    a_rk = [jnp.where(incl, a_k[u][hl:], 0.0).astype(BF16) for u in units]

    t_inv = [eye + pw[u] for u in units]
    for _ in range(n_sq):
        pw = [_dot(pw[u].astype(BF16), pw[u].astype(BF16)) for u in units]
        t_inv = [t_inv[u] + _dot(t_inv[u].astype(BF16), pw[u].astype(BF16)) for u in units]

    s0 = [s_ref[u] for u in units]
    xr = [_dot(jnp.concatenate([ah[u], rh[u]], axis=0).astype(BF16), s0[u].astype(BF16), _NT) for u in units]
    v = [lanes(v_ref, u) for u in units]
    v_s = [stack(v[u]).astype(BF16) for u in units]
    x_s = [stack(xr[u][:L]) + _dot(a_ak[u], v_s[u]) for u in units]
    u_s = [_dot(t_inv[u].astype(BF16), x_s[u].astype(BF16)) for u in units]
    y_s = [stack(xr[u][L:]) + _dot(a_rb[u], u_s[u].astype(BF16)) + _dot(a_rk[u], v_s[u]) for u in units]
    for u in units:
        upd = _dot(jnp.concatenate([unstack(u_s[u]), v[u]], axis=0).astype(BF16), upd_rhs[u], _TN)
        s_ref[u] = jnp.where(bd_state, s0[u] * e_last[u] + upd, 0.0)

    inv_n = 1.0 / HEAD_DIM
    y = [unstack(y_s[u]) for u in units]
    d = [y[u] - headsum(y[u]) * inv_n for u in units]
    var = [headsum(d[u] * d[u]) * inv_n for u in units]
    bonus = [headsum(lanes(r_ref, u) * lanes(k_ref, u) * rk_ref[:, u * UNIT:(u + 1) * UNIT]) * v[u] for u in units]
    for u in units:
        sl = slice(u * UNIT, (u + 1) * UNIT)
        yn = d[u] * lax.rsqrt(var[u] + GN_EPS) * lnw_ref[:, sl] + lnb_ref[:, sl]
        g = g_refs[u][0]
        o_ref[0, :, sl] = ((yn + bonus[u]) * (g * _sigmoid(g))).astype(o_ref.dtype)


def _rwkv_scan(prep, pa3, off_g, r_k, ln_w, ln_b):
    b, t, c = prep[0].shape
    L = RWKV_CHUNK
    n_units = min(RWKV_UNITS, c // UNIT)
    w = n_units * UNIT
    assert t % L == 0 and c % w == 0 and off_g % UNIT == 0
    gb = off_g // UNIT
    blk = pl.BlockSpec((1, L, w), lambda bi, gi, ci: (bi, ci, gi))
    vec = pl.BlockSpec((1, w), lambda bi, gi, ci: (0, gi))
    gates = [pl.BlockSpec((1, L, UNIT), lambda bi, gi, ci, u=u: (bi, ci, gb + gi * n_units + u))
             for u in range(n_units)]
    return pl.pallas_call(
        functools.partial(_rwkv_scan_kernel, n_units=n_units),
        grid=(b, c // w, t // L),
        in_specs=[blk] * 6 + gates + [vec, vec, vec],
        out_specs=blk,
        out_shape=jax.ShapeDtypeStruct((b, t, c), BF16),
        scratch_shapes=[pltpu.VMEM((n_units, UNIT, UNIT), F32)],
        compiler_params=_params("parallel", "parallel", "arbitrary"),
        name="rwkv_scan",
    )(*prep, *([pa3] * n_units), r_k.reshape(1, c), ln_w.reshape(1, c), ln_b.reshape(1, c))


def _fox_prefix_kernel(f_ref, bf_ref, k_ref, k0_ref, k1_ref, carry_ref):
    @pl.when(pl.program_id(1) == 0)
    def _():
        carry_ref[...] = jnp.zeros_like(carry_ref)

    x = f_ref[0][:, :LANES] + bf_ref[...]
    log_f = -_softplus(-x)
    tc = x.shape[0]
    tril = jnp.where(lax.broadcasted_iota(jnp.int32, (tc, tc), 0) >= lax.broadcasted_iota(jnp.int32, (tc, tc), 1),
                     1.0, 0.0).astype(BF16)
    c = carry_ref[...]
    for part in _split_bf16(log_f, 3):
        c = c + _dot(tril, part)
    carry_ref[...] = c[tc - 1:tc, :]

    fw = k_ref.shape[2]
    src = lax.broadcasted_iota(jnp.int32, (LANES, fw), 0)
    dst = lax.broadcasted_iota(jnp.int32, (LANES, fw), 1)
    lane = dst % LANES
    head = 2 * (dst // LANES) + jnp.where(lane < HEAD_DIM, 1, 0)
    bias = None
    for i, part in enumerate(_split_bf16(c * LOG2E, FOX_BIAS_PARTS)):
        sel = jnp.where((src == head) & (lane % HEAD_DIM == i), 1.0, 0.0).astype(BF16)
        d = _dot(part, sel)
        bias = d if bias is None else bias + d
    bias = bias.astype(BF16)
    k = k_ref[0]
    first = (lax.broadcasted_iota(jnp.int32, (tc, fw), 1) % LANES) < HEAD_DIM
    k0_ref[0] = jnp.where(first, k, bias)
    k1_ref[0] = jnp.where(first, bias, k)


def _fox_prefix(pt3, off_f, b_f, pb3, off_k, fw):
    b, t, _ = pt3.shape
    fh = b_f.shape[0]
    tc = min(ROWS_PREFIX, t)
    assert t % tc == 0 and fh <= LANES and fw == fh * HEAD_DIM and off_f % WIDE == 0 and off_k % fw == 0
    bf = jnp.zeros((1, LANES), F32).at[0, :fh].set(b_f)
    fb, kb = off_f // WIDE, off_k // fw
    out = pl.BlockSpec((1, tc, fw), lambda bi, i: (bi, i, 0))
    return pl.pallas_call(
        _fox_prefix_kernel,
        grid=(b, t // tc),
        in_specs=[pl.BlockSpec((1, tc, WIDE), lambda bi, i: (bi, i, fb)),
                  pl.BlockSpec((1, LANES), lambda bi, i: (0, 0)),
                  pl.BlockSpec((1, tc, fw), lambda bi, i: (bi, i, kb))],
        out_specs=[out, out],
        out_shape=[jax.ShapeDtypeStruct((b, t, fw), BF16)] * 2,
        scratch_shapes=[pltpu.VMEM((1, LANES), F32)],
        compiler_params=_params("parallel", "arbitrary"),
        name="fox_prefix",
    )(pt3, bf, pb3)


def _fox_attn_kernel(q_ref, k0_ref, k1_ref, v0_ref, v1_ref, g_ref, o_ref, m_ref, acc_ref, *, tk):
    qi = pl.program_id(2)
    tq = q_ref.shape[1]
    n_pairs = q_ref.shape[2] // LANES
    n_sub = tq // tk
    q_lane = lax.broadcasted_iota(jnp.int32, (tq, LANES), 1)
    q_head0 = q_lane < HEAD_DIM
    bias0 = jnp.where((q_lane >= HEAD_DIM) & (q_lane < HEAD_DIM + FOX_BIAS_PARTS), -1.0, 0.0)
    bias1 = jnp.where(q_lane < FOX_BIAS_PARTS, -1.0, 0.0)
    k_refs = (k0_ref, k1_ref)
    v_refs = (v0_ref, v1_ref)
    chains = [(pr, h) for pr in range(n_pairs) for h in range(2)]

    qs = []
    for pr in range(n_pairs):
        q = q_ref[0, :, pr * LANES:(pr + 1) * LANES].astype(F32) * (HEAD_DIM ** -0.5 * LOG2E)
        qs += [jnp.where(q_head0, q, bias0).astype(BF16), jnp.where(q_head0, bias1, q).astype(BF16)]

    m_ref[...] = jnp.full_like(m_ref, -jnp.inf)
    acc_ref[...] = jnp.zeros_like(acc_ref)

    def tile(j, valid):
        start = pl.multiple_of(j * tk, tk)
        scores = []
        for ci, (pr, h) in enumerate(chains):
            s = _dot(k_refs[h][0, pl.ds(start, tk), pr * LANES:(pr + 1) * LANES], qs[ci], _NT)
            scores.append(s if valid is None else jnp.where(valid, s, -jnp.inf))
        probs = []
        for ci, s in enumerate(scores):
            m = m_ref[ci]
            m_new = jnp.maximum(m, jnp.max(s, axis=0, keepdims=True))
            m_ref[ci] = m_new
            probs.append((jnp.exp2(m - m_new), jnp.exp2(s - m_new).astype(BF16)))
        for ci, (pr, h) in enumerate(chains):
            alpha, p = probs[ci]
            pv = _dot(v_refs[h][pr * LANES:(pr + 1) * LANES, pl.ds(start, tk)], p)
            acc_ref[ci] = alpha * acc_ref[ci] + pv

    def body(j, carry):
        tile(j, None)
        return carry

    lax.fori_loop(0, qi * n_sub, body, 0)
    key = lax.broadcasted_iota(jnp.int32, (tk, tq), 0)
    qry = lax.broadcasted_iota(jnp.int32, (tk, tq), 1)
    for d in range(n_sub):
        tile(qi * n_sub + d, key + d * tk <= qry)

    for pr in range(n_pairs):
        sl = slice(pr * LANES, (pr + 1) * LANES)
        acc0, acc1 = acc_ref[2 * pr], acc_ref[2 * pr + 1]
        out_t = jnp.concatenate([acc0[:HEAD_DIM] * (1.0 / acc0[HEAD_DIM:]),
                                 acc1[HEAD_DIM:] * (1.0 / acc1[:HEAD_DIM])], axis=0)
        g = g_ref[0, :, sl]
        o_ref[0, :, sl] = (out_t.T * (g * _sigmoid(g))).astype(o_ref.dtype)


def _fox_attn(pb3, off_q, k0, k1, v0t, v1t, pt3, off_g, fw):
    b, t, _ = pb3.shape
    tq = min(FOX_TQ, t)
    tk = min(FOX_TK, tq)
    n_pairs = min(FOX_PAIRS, fw // LANES)
    w = n_pairs * LANES
    assert t % tq == 0 and tq % tk == 0 and fw % w == 0 and off_q % w == 0 and off_g % w == 0
    qb, gb = off_q // w, off_g // w
    keys = pl.BlockSpec((1, t, w), lambda bi, p, i: (bi, 0, p))
    vals = pl.BlockSpec((w, t), lambda bi, p, i: (p, bi))
    return pl.pallas_call(
        functools.partial(_fox_attn_kernel, tk=tk),
        grid=(b, fw // w, t // tq),
        in_specs=[pl.BlockSpec((1, tq, w), lambda bi, p, i: (bi, i, qb + p)), keys, keys, vals, vals,
                  pl.BlockSpec((1, tq, w), lambda bi, p, i: (bi, i, gb + p))],
        out_specs=pl.BlockSpec((1, tq, w), lambda bi, p, i: (bi, i, p)),
        out_shape=jax.ShapeDtypeStruct((b, t, fw), BF16),
        scratch_shapes=[pltpu.VMEM((2 * n_pairs, 1, tq), F32), pltpu.VMEM((2 * n_pairs, LANES, tq), F32)],
        compiler_params=_params("parallel", "parallel", "arbitrary"),
        name="fox_attn",
    )(pb3, k0, k1, v0t, v1t, pt3)


def _mem_attn_kernel(q_ref, kv_ref, g_ref, o_ref):
    mw = q_ref.shape[2]
    hd = mw // MEM_HEADS
    scale = hd ** -0.5
    outs = []
    for h in range(MEM_HEADS):
        q = q_ref[0, :, h * hd:(h + 1) * hd]
        mk = kv_ref[0, :, h * hd:(h + 1) * hd]
        mv = kv_ref[0, :, mw + h * hd:mw + (h + 1) * hd]
        s = _dot(q, mk, _NT) * scale
        p = jnp.exp(s - jnp.max(s, axis=1, keepdims=True))
        l = jnp.sum(p, axis=1, keepdims=True)
        outs.append(_dot(p.astype(BF16), mv) / l)
    g = g_ref[0]
    o_ref[0] = (jnp.concatenate(outs, axis=1) * (g * _sigmoid(g))).astype(o_ref.dtype)


def _mem_attn(pq3, off_q, mkv3, pt3, off_g, mw):
    b, t, _ = pq3.shape
    tm = min(ROWS_MEM, t)
    assert t % tm == 0 and off_q % mw == 0 and off_g % mw == 0 and (mw // MEM_HEADS) % LANES == 0
    n_mem = mkv3.shape[1]
    qb, gb = off_q // mw, off_g // mw
    return pl.pallas_call(
        _mem_attn_kernel,
        grid=(b, t // tm),
        in_specs=[pl.BlockSpec((1, tm, mw), lambda bi, i: (bi, i, qb)),
                  pl.BlockSpec((1, n_mem, 2 * mw), lambda bi, i: (bi, 0, 0)),
                  pl.BlockSpec((1, tm, mw), lambda bi, i: (bi, i, gb))],
        out_specs=pl.BlockSpec((1, tm, mw), lambda bi, i: (bi, i, 0)),
        out_shape=jax.ShapeDtypeStruct((b, t, mw), BF16),
        compiler_params=_params("parallel", "parallel"),
        name="mem_attn",
    )(pq3, mkv3, pt3)


def _layer(x, mem, g_pre, w_in, mu_rwkv, w0, w_decay_up, a0, w_iclr_up, k_k, k_a, r_k,
           ln_x_w, ln_x_b, b_f, g_mem, w_mem_kv, w_out, g_post):
    b, t, d = x.shape
    n_mem = mem.shape[1]
    c = w0.shape[0]
    fh = b_f.shape[0]
    fw = fh * HEAD_DIM
    mw = w_mem_kv.shape[1] // 2
    lora = w_decay_up.shape[0]
    assert lora == LANES and w_iclr_up.shape[0] == LANES and fh <= WIDE
    assert w_in.shape[1] == 4 * c + 2 * lora + 4 * fw + fh + 2 * mw
    assert c % WIDE == 0 and fw % WIDE == 0 and mw % WIDE == 0

    o_g = 3 * c + 2 * lora
    o_fq = o_g + c
    o_fv = o_fq + 2 * fw
    o_f = o_fv + fw
    o_gfox = o_f + fh
    o_mq = o_gfox + fw
    o_gmq = o_mq + mw

    x2 = x.reshape(b * t, d)
    h = _rmsnorm(x2, g_pre, BF16)

    n_a = -(-o_fq // WIDE) * WIDE
    pa3 = _proj([h], [(w_in, (d, WIDE), lambda j: (0, j), (0, 0))], n_a, WIDE, F32, "in_proj_a").reshape(b, t, n_a)

    assert o_fq % MXU == 0
    fqb = o_fq // MXU
    halves = [(w_in, (d, MXU), lambda j, s=s: (0, fqb + 2 * j + s), (0, s * MXU)) for s in range(2)]
    pb3 = _proj([h], halves, 2 * fw, 2 * MXU, BF16, "in_proj_b").reshape(b, t, 2 * fw)

    v0t, v1t = _proj_t(w_in[:, o_fv:o_fv + fw].T.astype(BF16), h)

    w_t = jnp.concatenate([w_in[:, o_mq:o_mq + mw], w_in[:, o_gmq:o_gmq + mw], w_in[:, o_gfox:o_gfox + fw],
                           w_in[:, o_f:o_f + fh], jnp.zeros((d, WIDE - fh), w_in.dtype)], axis=1).astype(BF16)
    n_t = mw + fw + WIDE
    pq3 = _proj([h], [(w_t, (d, WIDE), lambda j: (0, j), (0, 0))], mw, WIDE, BF16, "in_proj_mq").reshape(b, t, mw)
    tb = mw // WIDE
    pt3 = _proj([h], [(w_t, (d, WIDE), lambda j: (0, tb + j), (0, 0))], n_t, WIDE, F32,
                "in_proj_tail").reshape(b, t, n_t)
    off_t = {"g_mq": 0, "g_fox": mw, "f": mw + fw}

    prep = _rwkv_prep(pa3, 0, 3 * c, mu_rwkv, w0, w_decay_up, a0, w_iclr_up, k_k, k_a, c)
    y_rwkv = _rwkv_scan(prep, pa3, o_g, r_k.reshape(-1), ln_x_w, ln_x_b)

    k0, k1 = _fox_prefix(pt3, off_t["f"], b_f, pb3, fw, fw)
    y_fox = _fox_attn(pb3, 0, k0, k1, v0t, v1t, pt3, off_t["g_fox"], fw)

    hm = _rmsnorm(mem.reshape(b * n_mem, d), g_mem, BF16)
    mkv3 = _proj([hm], [(w_mem_kv, (d, WIDE), lambda j: (0, j), (0, 0))], 2 * mw, WIDE, BF16,
                 "mem_kv_proj").reshape(b, n_mem, 2 * mw)
    y_mem = _mem_attn(pq3, 0, mkv3, pt3, off_t["g_mq"], mw)

    assert c == fw and (c + fw) % mw == 0
    slabs = [(w_out, (c, WIDE), lambda j: (0, j), (0, 0)),
             (w_out, (fw, WIDE), lambda j: (1, j), (c, 0)),
             (w_out, (mw, WIDE), lambda j: ((c + fw) // mw, j), (c + fw, 0))]
    z = _proj([y_rwkv.reshape(b * t, c), y_fox.reshape(b * t, fw), y_mem.reshape(b * t, mw)], slabs, d, WIDE, F32,
              "out_proj")
    return _post(x2, z, g_post).reshape(b, t, d)


def kernel(x, mem, g_pre, w_in, mu_rwkv, w0, w_decay_up, a0, w_iclr_up, k_k, k_a, r_k, ln_x_w, ln_x_b, b_f,
           g_mem, w_mem_kv, w_out, g_post):
    for l in range(g_pre.shape[0]):
        x = _layer(x, mem, g_pre[l], w_in[l], mu_rwkv[l], w0[l], w_decay_up[l], a0[l], w_iclr_up[l], k_k[l],
                   k_a[l], r_k[l], ln_x_w[l], ln_x_b[l], b_f[l], g_mem[l], w_mem_kv[l], w_out[l], g_post[l])
    return x
```

```python
import functools

import jax
import jax.numpy as jnp
from jax import lax
from jax.experimental import pallas as pl
from jax.experimental.pallas import tpu as pltpu

HEAD_DIM = 64
MEM_HEADS = 4
RMS_EPS = 1e-6
GN_EPS = 64e-5
LOG2E = 1.4426950408889634

LANES = 128
MXU = 256
UNIT_HEADS = MXU // HEAD_DIM
UNIT = UNIT_HEADS * HEAD_DIM
RWKV_CHUNK = 64
WIDE = 512
VMEM_LIMIT_BYTES = 56 * 1024 * 1024
FOX_BIAS_PARTS = 3

ROWS_ELEMENTWISE = 256
ROWS_PROJ = 1024
ROWS_PROJ_T = 512
ROWS_WEIGHT_PREP = 128
ROWS_PREP = 512
ROWS_PREFIX = 512
ROWS_MEM = 512
FOX_TQ = 512
FOX_TK = 256
FOX_PAIRS = 4
RWKV_UNITS = 6

F32 = jnp.float32
BF16 = jnp.bfloat16

_NN = (((1,), (0,)), ((), ()))
_NT = (((1,), (1,)), ((), ()))
_TN = (((0,), (0,)), ((), ()))


def _dot(a, b, dims=_NN):
    return lax.dot_general(a, b, dims, preferred_element_type=F32)


def _split_bf16(x, n):
    parts = []
    rem = x
    for _ in range(n):
        p = rem.astype(BF16)
        parts.append(p)
        rem = rem - p.astype(F32)
    return parts


def _dot_split(a, b_bf16, n, dims=_NN):
    acc = None
    for p in _split_bf16(a, n):
        d = _dot(p, b_bf16, dims)
        acc = d if acc is None else acc + d
    return acc


def _dot_f32(a, b):
    a_hi, a_lo = _split_bf16(a, 2)
    b_hi, b_lo = _split_bf16(b, 2)
    return _dot(a_hi, b_hi) + (_dot(a_lo, b_hi) + _dot(a_hi, b_lo))


def _sigmoid(x):
    return 1.0 / (1.0 + jnp.exp(-x))


def _softplus(x):
    return jnp.maximum(x, 0.0) + jnp.log1p(jnp.exp(-jnp.abs(x)))


def _params(*semantics):
    return pltpu.CompilerParams(dimension_semantics=semantics, vmem_limit_bytes=VMEM_LIMIT_BYTES)


def _rmsnorm_kernel(x_ref, g_ref, o_ref):
    x = x_ref[...]
    ms = jnp.mean(x * x, axis=-1, keepdims=True)
    o_ref[...] = (x * lax.rsqrt(ms + RMS_EPS) * g_ref[...]).astype(o_ref.dtype)


def _rmsnorm(x2d, g, out_dtype):
    m, d = x2d.shape
    tm = min(ROWS_ELEMENTWISE, m)
    assert m % tm == 0
    return pl.pallas_call(
        _rmsnorm_kernel,
        grid=(m // tm,),
        in_specs=[pl.BlockSpec((tm, d), lambda i: (i, 0)), pl.BlockSpec((1, d), lambda i: (0, 0))],
        out_specs=pl.BlockSpec((tm, d), lambda i: (i, 0)),
        out_shape=jax.ShapeDtypeStruct((m, d), out_dtype),
        compiler_params=_params("parallel"),
        name="rmsnorm",
    )(x2d, g.reshape(1, d))


def _proj_kernel(*refs, n_x, placement):
    xs = refs[:n_x]
    ws = refs[n_x:n_x + len(placement)]
    o_ref, wb_ref = refs[n_x + len(placement):]

    @pl.when(pl.program_id(1) == 0)
    def _():
        for w_ref, (r0, c0) in zip(ws, placement):
            wb_ref[r0:r0 + w_ref.shape[0], c0:c0 + w_ref.shape[1]] = w_ref[...].astype(BF16)

    acc = None
    r0 = 0
    for x_ref in xs:
        kx = x_ref.shape[1]
        d = _dot(x_ref[...], wb_ref[r0:r0 + kx, :])
        acc = d if acc is None else acc + d
        r0 += kx
    o_ref[...] = acc.astype(o_ref.dtype)


def _proj(xs, weights, n_cols, tn, out_dtype, name):
    m = xs[0].shape[0]
    k_total = sum(x.shape[1] for x in xs)
    tm = min(ROWS_PROJ, m)
    tn = min(tn, n_cols)
    assert m % tm == 0 and n_cols % tn == 0
    in_specs = [pl.BlockSpec((tm, x.shape[1]), lambda j, i: (i, 0)) for x in xs]
    for _, shape, index_fn, _ in weights:
        in_specs.append(pl.BlockSpec(shape, lambda j, i, f=index_fn: f(j)))
    return pl.pallas_call(
        functools.partial(_proj_kernel, n_x=len(xs), placement=tuple(p for _, _, _, p in weights)),
        grid=(n_cols // tn, m // tm),
        in_specs=in_specs,
        out_specs=pl.BlockSpec((tm, tn), lambda j, i: (i, j)),
        out_shape=jax.ShapeDtypeStruct((m, n_cols), out_dtype),
        scratch_shapes=[pltpu.VMEM((k_total, tn), BF16)],
        compiler_params=_params("parallel", "arbitrary"),
        name=name,
    )(*xs, *[w for w, _, _, _ in weights])


def _matmul_kernel(x_ref, w_ref, o_ref):
    o_ref[...] = _dot(x_ref[...], w_ref[...]).astype(o_ref.dtype)


def _matmul(x, w, out_dtype, tn, name):
    m, k = x.shape
    n = w.shape[1]
    tm = min(ROWS_PROJ, m)
    tn = min(tn, n)
    assert m % tm == 0 and n % tn == 0
    return pl.pallas_call(
        _matmul_kernel,
        grid=(n // tn, m // tm),
        in_specs=[pl.BlockSpec((tm, k), lambda j, i: (i, 0)), pl.BlockSpec((k, tn), lambda j, i: (0, j))],
        out_specs=pl.BlockSpec((tm, tn), lambda j, i: (i, j)),
        out_shape=jax.ShapeDtypeStruct((m, n), out_dtype),
        compiler_params=_params("parallel", "parallel"),
        name=name,
    )(x, w)


def _weight_prep_kernel(w_ref, wa_ref, wb_ref, wvt_ref, *, pieces_a, pieces_b, piece_vt):
    def gather(pieces, o_ref):
        dst = 0
        for src, width in pieces:
            o_ref[:, dst:dst + width] = w_ref[:, src:src + width].astype(o_ref.dtype)
            dst += width

    gather(pieces_a, wa_ref)
    gather(pieces_b, wb_ref)
    src, width = piece_vt
    wvt_ref[...] = w_ref[:, src:src + width].T.astype(wvt_ref.dtype)


def _weight_prep(w_in, pieces_a, pieces_b, piece_vt):
    k, n = w_in.shape
    tr = min(ROWS_WEIGHT_PREP, k)
    assert k % tr == 0
    na = sum(wd for _, wd in pieces_a)
    nb = sum(wd for _, wd in pieces_b)
    return pl.pallas_call(
        functools.partial(_weight_prep_kernel, pieces_a=pieces_a, pieces_b=pieces_b, piece_vt=piece_vt),
        grid=(k // tr,),
        in_specs=[pl.BlockSpec((tr, n), lambda i: (i, 0))],
        out_specs=[pl.BlockSpec((tr, na), lambda i: (i, 0)), pl.BlockSpec((tr, nb), lambda i: (i, 0)),
                   pl.BlockSpec((piece_vt[1], tr), lambda i: (0, i))],
        out_shape=[jax.ShapeDtypeStruct((k, na), BF16), jax.ShapeDtypeStruct((k, nb), BF16),
                   jax.ShapeDtypeStruct((piece_vt[1], k), BF16)],
        compiler_params=_params("parallel"),
        name="weight_prep",
    )(w_in)


def _proj_t_kernel(wt_ref, x_ref, v0_ref, v1_ref):
    vt = _dot(wt_ref[...], x_ref[...], _NT)
    first = (lax.broadcasted_iota(jnp.int32, vt.shape, 0) % LANES) < HEAD_DIM
    v0_ref[...] = jnp.where(first, vt, 1.0).astype(v0_ref.dtype)
    v1_ref[...] = jnp.where(first, 1.0, vt).astype(v1_ref.dtype)


def _proj_t(wt, x):
    n, k = wt.shape
    m = x.shape[0]
    tm = min(ROWS_PROJ_T, m)
    assert m % tm == 0 and n % LANES == 0
    out = pl.BlockSpec((n, tm), lambda i: (0, i))
    return pl.pallas_call(
        _proj_t_kernel,
        grid=(m // tm,),
        in_specs=[pl.BlockSpec((n, k), lambda i: (0, 0)), pl.BlockSpec((tm, k), lambda i: (i, 0))],
        out_specs=[out, out],
        out_shape=[jax.ShapeDtypeStruct((n, m), BF16)] * 2,
        compiler_params=_params("parallel"),
        name="in_proj_vt",
    )(wt, x)


def _post_kernel(x_ref, z_ref, g_ref, o_ref):
    z = z_ref[...]
    ms = jnp.mean(z * z, axis=-1, keepdims=True)
    o_ref[...] = x_ref[...] + z * lax.rsqrt(ms + RMS_EPS) * g_ref[...]


def _post(x2d, z2d, g):
    m, d = x2d.shape
    tm = min(ROWS_ELEMENTWISE, m)
    assert m % tm == 0
    row = pl.BlockSpec((tm, d), lambda i: (i, 0))
    return pl.pallas_call(
        _post_kernel,
        grid=(m // tm,),
        in_specs=[row, row, pl.BlockSpec((1, d), lambda i: (0, 0))],
        out_specs=row,
        out_shape=jax.ShapeDtypeStruct((m, d), x2d.dtype),
        compiler_params=_params("parallel"),
        name="post_norm_residual",
    )(x2d, z2d, g.reshape(1, d))


def _head_ones(n):
    row = lax.broadcasted_iota(jnp.int32, (n, n), 0) // HEAD_DIM
    col = lax.broadcasted_iota(jnp.int32, (n, n), 1) // HEAD_DIM
    return jnp.where(row == col, 1.0, 0.0).astype(BF16)


def _rwkv_prep_kernel(r_ref, k_ref, v_ref, lora_ref, mu_r_ref, mu_k_ref, mu_v_ref, mu_lora_ref,
                      w0_ref, wdu_ref, a0_ref, wiu_ref, kk_ref, ka_ref,
                      r_o, k_o, v_o, lw_o, a_o, b_o,
                      carry_r, carry_k, carry_v, carry_lora):
    @pl.when(pl.program_id(2) == 0)
    def _():
        for c in (carry_r, carry_k, carry_v, carry_lora):
            c[...] = jnp.zeros_like(c)

    tm = r_ref.shape[1]
    first_row = lax.broadcasted_iota(jnp.int32, (tm, 1), 0) == 0

    def shift(x, carry_ref, mu):
        prev = jnp.where(first_row, carry_ref[...], pltpu.roll(x, 1, 0))
        carry_ref[...] = x[tm - 1:tm, :]
        return x + (prev - x) * mu

    r = shift(r_ref[0], carry_r, mu_r_ref[...])
    k = shift(k_ref[0], carry_k, mu_k_ref[...])
    v = shift(v_ref[0], carry_v, mu_v_ref[...])
    lora = shift(lora_ref[0][:, :2 * LANES], carry_lora, mu_lora_ref[...])
    wl = lora[:, :LANES]
    al = lora[:, LANES:]

    w_pre = -_softplus(-(w0_ref[...] + _dot_f32(jnp.tanh(wl), wdu_ref[...]))) - 0.5
    log_decay = -jnp.exp(w_pre)
    alpha = _sigmoid(a0_ref[...] + _dot_f32(al, wiu_ref[...]))

    kk = k * kk_ref[...]
    ss = _dot_split(kk * kk, _head_ones(kk.shape[1]), 2)
    kk = kk * lax.rsqrt(jnp.maximum(ss, 1e-24))

    r_o[0] = r
    k_o[0] = k * (1.0 + (alpha - 1.0) * ka_ref[...])
    v_o[0] = v
    lw_o[0] = log_decay
    a_o[0] = -kk
    b_o[0] = kk * alpha


def _rwkv_prep(pa3, off_r, off_lora, mu, w0, wdu, a0, wiu, k_k, k_a, c):
    b, t, _ = pa3.shape
    tm = min(ROWS_PREP, t)
    ws = min(WIDE, c)
    assert t % tm == 0 and c % ws == 0 and off_r % ws == 0 and off_lora % WIDE == 0
    nslab = c // ws
    rb, lb = off_r // ws, off_lora // WIDE

    def col(base):
        return pl.BlockSpec((1, tm, ws), lambda bi, s, i: (bi, i, base + s))

    def vec(base=0):
        return pl.BlockSpec((1, ws), lambda bi, s, i: (0, base + s))

    lora_w = pl.BlockSpec((wdu.shape[0], ws), lambda bi, s, i: (0, s))
    out = pl.BlockSpec((1, tm, ws), lambda bi, s, i: (bi, i, s))
    mu_rkv = mu[:3 * c].reshape(1, 3 * c)
    mu_lora = mu[3 * c:].reshape(1, 2 * LANES)
    row = lambda z: z.reshape(1, c)
    return pl.pallas_call(
        _rwkv_prep_kernel,
        grid=(b, nslab, t // tm),
        in_specs=[col(rb), col(rb + nslab), col(rb + 2 * nslab),
                  pl.BlockSpec((1, tm, WIDE), lambda bi, s, i: (bi, i, lb)),
                  vec(0), vec(nslab), vec(2 * nslab),
                  pl.BlockSpec((1, 2 * LANES), lambda bi, s, i: (0, 0)),
                  vec(), lora_w, vec(), lora_w, vec(), vec()],
        out_specs=[out] * 6,
        out_shape=[jax.ShapeDtypeStruct((b, t, c), F32)] * 6,
        scratch_shapes=[pltpu.VMEM((1, ws), F32)] * 3 + [pltpu.VMEM((1, 2 * LANES), F32)],
        compiler_params=_params("parallel", "parallel", "arbitrary"),
        name="rwkv_prep",
    )(pa3, pa3, pa3, pa3, mu_rkv, mu_rkv, mu_rkv, mu_lora,
      row(w0), wdu, row(a0), wiu, row(k_k), row(k_a))


def _rwkv_scan_kernel(*refs, n_units):
    r_ref, k_ref, v_ref, lw_ref, a_ref, b_ref = refs[:6]
    g_refs = refs[6:6 + n_units]
    rk_ref, lnw_ref, lnb_ref, o_ref, s_ref = refs[6 + n_units:]

    @pl.when(pl.program_id(2) == 0)
    def _():
        s_ref[...] = jnp.zeros_like(s_ref)

    L = r_ref.shape[1]
    hl = UNIT_HEADS * L
    n_sq = L.bit_length() - 2
    assert 2 ** (n_sq + 1) == L
    units = range(n_units)

    def blk(shape, d0, d1):
        return (lax.broadcasted_iota(jnp.int32, shape, 0) // d0) == (lax.broadcasted_iota(jnp.int32, shape, 1) // d1)

    t_row = lax.broadcasted_iota(jnp.int32, (hl, hl), 0)
    t_col = lax.broadcasted_iota(jnp.int32, (hl, hl), 1)
    same = blk((hl, hl), L, L)
    strict = same & (t_col < t_row)
    incl = same & (t_col <= t_row)
    eye = jnp.where(t_row == t_col, 1.0, 0.0)
    bd_rows = blk((hl, UNIT), L, HEAD_DIM)
    bd_state = blk((UNIT, UNIT), HEAD_DIM, HEAD_DIM)
    ones_bd = jnp.where(bd_state, 1.0, 0.0).astype(BF16)
    tril = jnp.where(lax.broadcasted_iota(jnp.int32, (L, L), 0) >= lax.broadcasted_iota(jnp.int32, (L, L), 1),
                     1.0, 0.0).astype(BF16)

    def rep(x):
        return jnp.concatenate([x] * UNIT_HEADS, axis=0)

    def stack(x):
        return jnp.where(bd_rows, rep(x), 0.0)

    def unstack(xs):
        out = xs[0:L]
        for h in range(1, UNIT_HEADS):
            out = out + xs[h * L:(h + 1) * L]
        return out

    def headsum(x):
        return _dot_split(x, ones_bd, 2)

    def lanes(ref, u):
        return ref[0, :, u * UNIT:(u + 1) * UNIT]

    lw_all = lw_ref[0]
    lw_hi, lw_lo = _split_bf16(lw_all, 2)
    cl_all = _dot(tril, lw_hi) + _dot(tril, lw_lo)

    ah, rh, e_last, lhs, rhs_b, rhs_k, upd_rhs = [], [], [], [], [], [], []
    for u in units:
        sl = slice(u * UNIT, (u + 1) * UNIT)
        cl = cl_all[:, sl]
        e_pos = jnp.exp(cl)
        e_neg = jnp.exp(-cl)
        e_rem = jnp.exp(cl[L - 1:L, :] - cl)
        a, b, k = lanes(a_ref, u), lanes(b_ref, u), lanes(k_ref, u)
        ah.append(a * jnp.exp(cl - lw_all[:, sl]))
        rh.append(lanes(r_ref, u) * e_pos)
        e_last.append(e_pos[L - 1:L, :])
        lhs.append(jnp.concatenate([stack(ah[u]), stack(rh[u])], axis=0).astype(BF16))
        rhs_b.append(rep(b * e_neg).astype(BF16))
        rhs_k.append(rep(k * e_neg).astype(BF16))
        upd_rhs.append(jnp.concatenate([b * e_rem, k * e_rem], axis=0).astype(BF16))

    a_b = [_dot(lhs[u], rhs_b[u], _NT) for u in units]
    a_k = [_dot(lhs[u], rhs_k[u], _NT) for u in units]
    pw = [jnp.where(strict, a_b[u][:hl], 0.0) for u in units]
    a_ak = [jnp.where(strict, a_k[u][:hl], 0.0).astype(BF16) for u in units]
    a_rb = [jnp.where(incl, a_b[u][hl:], 0.0).astype(BF16) for u in units]
    a_rk = [jnp.where(incl, a_k[u][hl:], 0.0).astype(BF16) for u in units]

    t_inv = [eye + pw[u] for u in units]
    for _ in range(n_sq):
        pw = [_dot(pw[u].astype(BF16), pw[u].astype(BF16)) for u in units]
        t_inv = [t_inv[u] + _dot(t_inv[u].astype(BF16), pw[u].astype(BF16)) for u in units]

    s0 = [s_ref[u] for u in units]
    xr = [_dot(jnp.concatenate([ah[u], rh[u]], axis=0).astype(BF16), s0[u].astype(BF16), _NT) for u in units]
    v = [lanes(v_ref, u) for u in units]
    v_s = [stack(v[u]).astype(BF16) for u in units]
    x_s = [stack(xr[u][:L]) + _dot(a_ak[u], v_s[u]) for u in units]
    u_s = [_dot(t_inv[u].astype(BF16), x_s[u].astype(BF16)) for u in units]
    y_s = [stack(xr[u][L:]) + _dot(a_rb[u], u_s[u].astype(BF16)) + _dot(a_rk[u], v_s[u]) for u in units]
    for u in units:
        upd = _dot(jnp.concatenate([unstack(u_s[u]), v[u]], axis=0).astype(BF16), upd_rhs[u], _TN)
        s_ref[u] = jnp.where(bd_state, s0[u] * e_last[u] + upd, 0.0)

    inv_n = 1.0 / HEAD_DIM
    y = [unstack(y_s[u]) for u in units]
    d = [y[u] - headsum(y[u]) * inv_n for u in units]
    var = [headsum(d[u] * d[u]) * inv_n for u in units]
    bonus = [headsum(lanes(r_ref, u) * lanes(k_ref, u) * rk_ref[:, u * UNIT:(u + 1) * UNIT]) * v[u] for u in units]
    for u in units:
        sl = slice(u * UNIT, (u + 1) * UNIT)
        yn = d[u] * lax.rsqrt(var[u] + GN_EPS) * lnw_ref[:, sl] + lnb_ref[:, sl]
        g = g_refs[u][0]
        o_ref[0, :, sl] = ((yn + bonus[u]) * (g * _sigmoid(g))).astype(o_ref.dtype)


def _rwkv_scan(prep, pa3, off_g, r_k, ln_w, ln_b):
    b, t, c = prep[0].shape
    L = RWKV_CHUNK
    n_units = min(RWKV_UNITS, c // UNIT)
    w = n_units * UNIT
    assert t % L == 0 and c % w == 0 and off_g % UNIT == 0
    gb = off_g // UNIT
    blk = pl.BlockSpec((1, L, w), lambda bi, gi, ci: (bi, ci, gi))
    vec = pl.BlockSpec((1, w), lambda bi, gi, ci: (0, gi))
    gates = [pl.BlockSpec((1, L, UNIT), lambda bi, gi, ci, u=u: (bi, ci, gb + gi * n_units + u))
             for u in range(n_units)]
    return pl.pallas_call(
        functools.partial(_rwkv_scan_kernel, n_units=n_units),
        grid=(b, c // w, t // L),
        in_specs=[blk] * 6 + gates + [vec, vec, vec],
        out_specs=blk,
        out_shape=jax.ShapeDtypeStruct((b, t, c), BF16),
        scratch_shapes=[pltpu.VMEM((n_units, UNIT, UNIT), F32)],
        compiler_params=_params("parallel", "parallel", "arbitrary"),
        name="rwkv_scan",
    )(*prep, *([pa3] * n_units), r_k.reshape(1, c), ln_w.reshape(1, c), ln_b.reshape(1, c))


def _fox_prefix_kernel(f_ref, bf_ref, k_ref, k0_ref, k1_ref, carry_ref):
    @pl.when(pl.program_id(1) == 0)
    def _():
        carry_ref[...] = jnp.zeros_like(carry_ref)

    x = f_ref[0][:, 2 * LANES:3 * LANES] + bf_ref[...]
    log_f = -_softplus(-x)
    tc = x.shape[0]
    tril = jnp.where(lax.broadcasted_iota(jnp.int32, (tc, tc), 0) >= lax.broadcasted_iota(jnp.int32, (tc, tc), 1),
                     1.0, 0.0).astype(BF16)
    c = carry_ref[...]
    for part in _split_bf16(log_f, 3):
        c = c + _dot(tril, part)
    carry_ref[...] = c[tc - 1:tc, :]

    fw = k_ref.shape[2]
    src = lax.broadcasted_iota(jnp.int32, (LANES, fw), 0)
    dst = lax.broadcasted_iota(jnp.int32, (LANES, fw), 1)
    lane = dst % LANES
    head = 2 * (dst // LANES) + jnp.where(lane < HEAD_DIM, 1, 0)
    bias = None
    for i, part in enumerate(_split_bf16(c * LOG2E, FOX_BIAS_PARTS)):
        sel = jnp.where((src == head) & (lane % HEAD_DIM == i), 1.0, 0.0).astype(BF16)
        d = _dot(part, sel)
        bias = d if bias is None else bias + d
    bias = bias.astype(BF16)
    k = k_ref[0]
    first = (lax.broadcasted_iota(jnp.int32, (tc, fw), 1) % LANES) < HEAD_DIM
    k0_ref[0] = jnp.where(first, k, bias)
    k1_ref[0] = jnp.where(first, bias, k)


def _fox_prefix(pt3, off_f, b_f, pb3, off_k, fw):
    b, t, _ = pt3.shape
    fh = b_f.shape[0]
    tc = min(ROWS_PREFIX, t)
    assert t % tc == 0 and fh <= LANES and fw == fh * HEAD_DIM and off_f % WIDE == 0 and off_k % fw == 0
    bf = jnp.zeros((1, LANES), F32).at[0, :fh].set(b_f)
    fb, kb = off_f // WIDE, off_k // fw
    out = pl.BlockSpec((1, tc, fw), lambda bi, i: (bi, i, 0))
    return pl.pallas_call(
        _fox_prefix_kernel,
        grid=(b, t // tc),
        in_specs=[pl.BlockSpec((1, tc, WIDE), lambda bi, i: (bi, i, fb)),
                  pl.BlockSpec((1, LANES), lambda bi, i: (0, 0)),
                  pl.BlockSpec((1, tc, fw), lambda bi, i: (bi, i, kb))],
        out_specs=[out, out],
        out_shape=[jax.ShapeDtypeStruct((b, t, fw), BF16)] * 2,
        scratch_shapes=[pltpu.VMEM((1, LANES), F32)],
        compiler_params=_params("parallel", "arbitrary"),
        name="fox_prefix",
    )(pt3, bf, pb3)


def _fox_attn_kernel(q_ref, k0_ref, k1_ref, v0_ref, v1_ref, g_ref, o_ref, m_ref, acc_ref, *, tk):
    qi = pl.program_id(2)
    tq = q_ref.shape[1]
    n_pairs = q_ref.shape[2] // LANES
    n_sub = tq // tk
    q_lane = lax.broadcasted_iota(jnp.int32, (tq, LANES), 1)
    q_head0 = q_lane < HEAD_DIM
    bias0 = jnp.where((q_lane >= HEAD_DIM) & (q_lane < HEAD_DIM + FOX_BIAS_PARTS), -1.0, 0.0)
    bias1 = jnp.where(q_lane < FOX_BIAS_PARTS, -1.0, 0.0)
    k_refs = (k0_ref, k1_ref)
    v_refs = (v0_ref, v1_ref)
    chains = [(pr, h) for pr in range(n_pairs) for h in range(2)]

    qs = []
    for pr in range(n_pairs):
        q = q_ref[0, :, pr * LANES:(pr + 1) * LANES].astype(F32) * (HEAD_DIM ** -0.5 * LOG2E)
        qs += [jnp.where(q_head0, q, bias0).astype(BF16), jnp.where(q_head0, bias1, q).astype(BF16)]

    m_ref[...] = jnp.full_like(m_ref, -jnp.inf)
    acc_ref[...] = jnp.zeros_like(acc_ref)

    def tile(j, valid):
        start = pl.multiple_of(j * tk, tk)
        scores = []
        for ci, (pr, h) in enumerate(chains):
            s = _dot(k_refs[h][0, pl.ds(start, tk), pr * LANES:(pr + 1) * LANES], qs[ci], _NT)
            scores.append(s if valid is None else jnp.where(valid, s, -jnp.inf))
        probs = []
        for ci, s in enumerate(scores):
            m = m_ref[ci]
            m_new = jnp.maximum(m, jnp.max(s, axis=0, keepdims=True))
            m_ref[ci] = m_new
            probs.append((jnp.exp2(m - m_new), jnp.exp2(s - m_new).astype(BF16)))
        for ci, (pr, h) in enumerate(chains):
            alpha, p = probs[ci]
            pv = _dot(v_refs[h][pr * LANES:(pr + 1) * LANES, pl.ds(start, tk)], p)
            acc_ref[ci] = alpha * acc_ref[ci] + pv

    def body(j, carry):
        tile(j, None)
        return carry

    lax.fori_loop(0, qi * n_sub, body, 0)
    key = lax.broadcasted_iota(jnp.int32, (tk, tq), 0)
    qry = lax.broadcasted_iota(jnp.int32, (tk, tq), 1)
    for d in range(n_sub):
        tile(qi * n_sub + d, key + d * tk <= qry)

    for pr in range(n_pairs):
        sl = slice(pr * LANES, (pr + 1) * LANES)
        acc0, acc1 = acc_ref[2 * pr], acc_ref[2 * pr + 1]
        out_t = jnp.concatenate([acc0[:HEAD_DIM] * (1.0 / acc0[HEAD_DIM:]),
                                 acc1[HEAD_DIM:] * (1.0 / acc1[:HEAD_DIM])], axis=0)
        g = g_ref[0, :, sl]
        o_ref[0, :, sl] = (out_t.T * (g * _sigmoid(g))).astype(o_ref.dtype)


def _fox_attn(pb3, off_q, k0, k1, v0t, v1t, pt3, off_g, fw):
    b, t, _ = pb3.shape
    tq = min(FOX_TQ, t)
    tk = min(FOX_TK, tq)
    n_pairs = min(FOX_PAIRS, fw // LANES)
    w = n_pairs * LANES
    assert t % tq == 0 and tq % tk == 0 and fw % w == 0 and off_q % w == 0 and off_g % w == 0
    qb, gb = off_q // w, off_g // w
    keys = pl.BlockSpec((1, t, w), lambda bi, p, i: (bi, 0, p))
    vals = pl.BlockSpec((w, t), lambda bi, p, i: (p, bi))
    return pl.pallas_call(
        functools.partial(_fox_attn_kernel, tk=tk),
        grid=(b, fw // w, t // tq),
        in_specs=[pl.BlockSpec((1, tq, w), lambda bi, p, i: (bi, i, qb + p)), keys, keys, vals, vals,
                  pl.BlockSpec((1, tq, w), lambda bi, p, i: (bi, i, gb + p))],
        out_specs=pl.BlockSpec((1, tq, w), lambda bi, p, i: (bi, i, p)),
        out_shape=jax.ShapeDtypeStruct((b, t, fw), BF16),
        scratch_shapes=[pltpu.VMEM((2 * n_pairs, 1, tq), F32), pltpu.VMEM((2 * n_pairs, LANES, tq), F32)],
        compiler_params=_params("parallel", "parallel", "arbitrary"),
        name="fox_attn",
    )(pb3, k0, k1, v0t, v1t, pt3)


def _mem_attn_kernel(q_ref, kv_ref, g_ref, o_ref):
    mw = q_ref.shape[2]
    hd = mw // MEM_HEADS
    scale = hd ** -0.5
    outs = []
    for h in range(MEM_HEADS):
        q = q_ref[0, :, h * hd:(h + 1) * hd]
        mk = kv_ref[0, :, h * hd:(h + 1) * hd]
        mv = kv_ref[0, :, mw + h * hd:mw + (h + 1) * hd]
        s = _dot(q, mk, _NT) * scale
        p = jnp.exp(s - jnp.max(s, axis=1, keepdims=True))
        l = jnp.sum(p, axis=1, keepdims=True)
        outs.append(_dot(p.astype(BF16), mv) / l)
    g = g_ref[0]
    o_ref[0] = (jnp.concatenate(outs, axis=1) * (g * _sigmoid(g))).astype(o_ref.dtype)


def _mem_attn(pq3, off_q, mkv3, pt3, off_g, mw):
    b, t, _ = pq3.shape
    tm = min(ROWS_MEM, t)
    assert t % tm == 0 and off_q % mw == 0 and off_g % mw == 0 and (mw // MEM_HEADS) % LANES == 0
    n_mem = mkv3.shape[1]
    qb, gb = off_q // mw, off_g // mw
    return pl.pallas_call(
        _mem_attn_kernel,
        grid=(b, t // tm),
        in_specs=[pl.BlockSpec((1, tm, mw), lambda bi, i: (bi, i, qb)),
                  pl.BlockSpec((1, n_mem, 2 * mw), lambda bi, i: (bi, 0, 0)),
                  pl.BlockSpec((1, tm, mw), lambda bi, i: (bi, i, gb))],
        out_specs=pl.BlockSpec((1, tm, mw), lambda bi, i: (bi, i, 0)),
        out_shape=jax.ShapeDtypeStruct((b, t, mw), BF16),
        compiler_params=_params("parallel", "parallel"),
        name="mem_attn",
    )(pq3, mkv3, pt3)


def _layer(x, mem, g_pre, w_in, mu_rwkv, w0, w_decay_up, a0, w_iclr_up, k_k, k_a, r_k,
           ln_x_w, ln_x_b, b_f, g_mem, w_mem_kv, w_out, g_post):
    b, t, d = x.shape
    n_mem = mem.shape[1]
    c = w0.shape[0]
    fh = b_f.shape[0]
    fw = fh * HEAD_DIM
    mw = w_mem_kv.shape[1] // 2
    lora = w_decay_up.shape[0]
    assert lora == LANES and w_iclr_up.shape[0] == LANES and fh <= WIDE
    assert w_in.shape[1] == 4 * c + 2 * lora + 4 * fw + fh + 2 * mw
    assert c % WIDE == 0 and fw % WIDE == 0 and mw % WIDE == 0

    o_g = 3 * c + 2 * lora
    o_fq = o_g + c
    o_fv = o_fq + 2 * fw
    o_f = o_fv + fw
    o_gfox = o_f + fh
    o_mq = o_gfox + fw
    o_gmq = o_mq + mw

    x2 = x.reshape(b * t, d)
    h = _rmsnorm(x2, g_pre, BF16)

    lora_f = WIDE - 2 * lora
    assert o_f % LANES == 0 and o_f + lora_f <= w_in.shape[1]
    pieces_a = ((o_g, c), (0, 3 * c), (o_gmq, mw), (o_gfox, fw), (3 * c, 2 * lora), (o_f, lora_f))
    pieces_b = ((o_fq, 2 * fw), (o_mq, mw))
    w_a, w_b, w_vt = _weight_prep(w_in, pieces_a, pieces_b, (o_fv, fw))
    off_a = {"g_rwkv": 0, "r": c, "g_mq": 4 * c, "g_fox": 4 * c + mw, "lora": 4 * c + mw + fw}
    off_b = {"fq": 0, "fk": fw, "mq": 2 * fw}

    pa3 = _matmul(h, w_a, F32, 2 * WIDE, "in_proj_a").reshape(b, t, -1)
    pb3 = _matmul(h, w_b, BF16, 2 * WIDE, "in_proj_b").reshape(b, t, -1)
    v0t, v1t = _proj_t(w_vt, h)

    prep = _rwkv_prep(pa3, off_a["r"], off_a["lora"], mu_rwkv, w0, w_decay_up, a0, w_iclr_up, k_k, k_a, c)
    y_rwkv = _rwkv_scan(prep, pa3, off_a["g_rwkv"], r_k.reshape(-1), ln_x_w, ln_x_b)

    k0, k1 = _fox_prefix(pa3, off_a["lora"], b_f, pb3, off_b["fk"], fw)
    y_fox = _fox_attn(pb3, off_b["fq"], k0, k1, v0t, v1t, pa3, off_a["g_fox"], fw)

    hm = _rmsnorm(mem.reshape(b * n_mem, d), g_mem, BF16)
    mkv3 = _proj([hm], [(w_mem_kv, (d, WIDE), lambda j: (0, j), (0, 0))], 2 * mw, WIDE, BF16,
                 "mem_kv_proj").reshape(b, n_mem, 2 * mw)
    y_mem = _mem_attn(pb3, off_b["mq"], mkv3, pa3, off_a["g_mq"], mw)

    assert c == fw and (c + fw) % mw == 0
    slabs = [(w_out, (c, WIDE), lambda j: (0, j), (0, 0)),
             (w_out, (fw, WIDE), lambda j: (1, j), (c, 0)),
             (w_out, (mw, WIDE), lambda j: ((c + fw) // mw, j), (c + fw, 0))]
    z = _proj([y_rwkv.reshape(b * t, c), y_fox.reshape(b * t, fw), y_mem.reshape(b * t, mw)], slabs, d, WIDE, F32,
              "out_proj")
    return _post(x2, z, g_post).reshape(b, t, d)


def kernel(x, mem, g_pre, w_in, mu_rwkv, w0, w_decay_up, a0, w_iclr_up, k_k, k_a, r_k, ln_x_w, ln_x_b, b_f,
           g_mem, w_mem_kv, w_out, g_post):
    for l in range(g_pre.shape[0]):
        x = _layer(x, mem, g_pre[l], w_in[l], mu_rwkv[l], w0[l], w_decay_up[l], a0[l], w_iclr_up[l], k_k[l],
                   k_a[l], r_k[l], ln_x_w[l], ln_x_b[l], b_f[l], g_mem[l], w_mem_kv[l], w_out[l], g_post[l])
    return x
```

```python
import functools

import jax
import jax.numpy as jnp
from jax import lax
from jax.experimental import pallas as pl
from jax.experimental.pallas import tpu as pltpu

HEAD_DIM = 64
MEM_HEADS = 4
RMS_EPS = 1e-6
GN_EPS = 64e-5
LOG2E = 1.4426950408889634

LANES = 128
MXU = 256
UNIT_HEADS = MXU // HEAD_DIM
UNIT = UNIT_HEADS * HEAD_DIM
RWKV_CHUNK = 64
WIDE = 512
VMEM_LIMIT_BYTES = 56 * 1024 * 1024
FOX_BIAS_PARTS = 3

ROWS_ELEMENTWISE = 256
ROWS_PROJ = 1024
ROWS_PROJ_T = 512
COLS_WEIGHT_PREP = 128
RWKV_CHUNKS_PER_STEP = 2
ROWS_PREFIX = 512
ROWS_MEM = 512
FOX_TQ = 512
FOX_TK = 256
FOX_PAIRS = 4
RWKV_UNITS = 6

F32 = jnp.float32
BF16 = jnp.bfloat16

_NN = (((1,), (0,)), ((), ()))
_NT = (((1,), (1,)), ((), ()))
_TN = (((0,), (0,)), ((), ()))


def _dot(a, b, dims=_NN):
    return lax.dot_general(a, b, dims, preferred_element_type=F32)


def _split_bf16(x, n):
    parts = []
    rem = x
    for _ in range(n):
        p = rem.astype(BF16)
        parts.append(p)
        rem = rem - p.astype(F32)
    return parts


def _dot_split(a, b_bf16, n, dims=_NN):
    acc = None
    for p in _split_bf16(a, n):
        d = _dot(p, b_bf16, dims)
        acc = d if acc is None else acc + d
    return acc


def _dot_f32(a, b):
    a_hi, a_lo = _split_bf16(a, 2)
    b_hi, b_lo = _split_bf16(b, 2)
    return _dot(a_hi, b_hi) + (_dot(a_lo, b_hi) + _dot(a_hi, b_lo))


def _sigmoid(x):
    return 1.0 / (1.0 + jnp.exp(-x))


def _softplus(x):
    return jnp.maximum(x, 0.0) + jnp.log1p(jnp.exp(-jnp.abs(x)))


def _params(*semantics):
    return pltpu.CompilerParams(dimension_semantics=semantics, vmem_limit_bytes=VMEM_LIMIT_BYTES)


def _rmsnorm_kernel(x_ref, g_ref, o_ref):
    x = x_ref[...]
    ms = jnp.mean(x * x, axis=-1, keepdims=True)
    o_ref[...] = (x * lax.rsqrt(ms + RMS_EPS) * g_ref[...]).astype(o_ref.dtype)


def _rmsnorm(x2d, g, out_dtype):
    m, d = x2d.shape
    tm = min(ROWS_ELEMENTWISE, m)
    assert m % tm == 0
    return pl.pallas_call(
        _rmsnorm_kernel,
        grid=(m // tm,),
        in_specs=[pl.BlockSpec((tm, d), lambda i: (i, 0)), pl.BlockSpec((1, d), lambda i: (0, 0))],
        out_specs=pl.BlockSpec((tm, d), lambda i: (i, 0)),
        out_shape=jax.ShapeDtypeStruct((m, d), out_dtype),
        compiler_params=_params("parallel"),
        name="rmsnorm",
    )(x2d, g.reshape(1, d))


def _proj_kernel(*refs, n_x, placement):
    xs = refs[:n_x]
    ws = refs[n_x:n_x + len(placement)]
    o_ref, wb_ref = refs[n_x + len(placement):]

    @pl.when(pl.program_id(1) == 0)
    def _():
        for w_ref, (r0, c0) in zip(ws, placement):
            wb_ref[r0:r0 + w_ref.shape[0], c0:c0 + w_ref.shape[1]] = w_ref[...].astype(BF16)

    acc = None
    r0 = 0
    for x_ref in xs:
        kx = x_ref.shape[1]
        d = _dot(x_ref[...], wb_ref[r0:r0 + kx, :])
        acc = d if acc is None else acc + d
        r0 += kx
    o_ref[...] = acc.astype(o_ref.dtype)


def _proj(xs, weights, n_cols, tn, out_dtype, name):
    m = xs[0].shape[0]
    k_total = sum(x.shape[1] for x in xs)
    tm = min(ROWS_PROJ, m)
    tn = min(tn, n_cols)
    assert m % tm == 0 and n_cols % tn == 0
    in_specs = [pl.BlockSpec((tm, x.shape[1]), lambda j, i: (i, 0)) for x in xs]
    for _, shape, index_fn, _ in weights:
        in_specs.append(pl.BlockSpec(shape, lambda j, i, f=index_fn: f(j)))
    return pl.pallas_call(
        functools.partial(_proj_kernel, n_x=len(xs), placement=tuple(p for _, _, _, p in weights)),
        grid=(n_cols // tn, m // tm),
        in_specs=in_specs,
        out_specs=pl.BlockSpec((tm, tn), lambda j, i: (i, j)),
        out_shape=jax.ShapeDtypeStruct((m, n_cols), out_dtype),
        scratch_shapes=[pltpu.VMEM((k_total, tn), BF16)],
        compiler_params=_params("parallel", "arbitrary"),
        name=name,
    )(*xs, *[w for w, _, _, _ in weights])


def _matmul_kernel(x_ref, w_ref, o_ref):
    o_ref[...] = _dot(x_ref[...], w_ref[...]).astype(o_ref.dtype)


def _matmul(x, w, out_dtype, tn, name):
    m, k = x.shape
    n = w.shape[1]
    tm = min(ROWS_PROJ, m)
    tn = min(tn, n)
    assert m % tm == 0 and n % tn == 0
    return pl.pallas_call(
        _matmul_kernel,
        grid=(n // tn, m // tm),
        in_specs=[pl.BlockSpec((tm, k), lambda j, i: (i, 0)), pl.BlockSpec((k, tn), lambda j, i: (0, j))],
        out_specs=pl.BlockSpec((tm, tn), lambda j, i: (i, j)),
        out_shape=jax.ShapeDtypeStruct((m, n), out_dtype),
        compiler_params=_params("parallel", "parallel"),
        name=name,
    )(x, w)


def _weight_prep_kernel(wt_ref, wa_ref, wb_ref, wvt_ref, *, pieces_a, pieces_b, piece_vt):
    def gather(pieces, o_ref):
        dst = 0
        for src, width in pieces:
            o_ref[:, dst:dst + width] = wt_ref[src:src + width, :].T.astype(o_ref.dtype)
            dst += width

    gather(pieces_a, wa_ref)
    gather(pieces_b, wb_ref)
    src, width = piece_vt
    wvt_ref[...] = wt_ref[src:src + width, :].astype(wvt_ref.dtype)


def _weight_prep(w_in_t, pieces_a, pieces_b, piece_vt):
    n, k = w_in_t.shape
    tc = min(COLS_WEIGHT_PREP, k)
    assert k % tc == 0 and all(s % 8 == 0 and wd % 16 == 0 for s, wd in pieces_a + pieces_b + (piece_vt,))
    na = sum(wd for _, wd in pieces_a)
    nb = sum(wd for _, wd in pieces_b)
    return pl.pallas_call(
        functools.partial(_weight_prep_kernel, pieces_a=pieces_a, pieces_b=pieces_b, piece_vt=piece_vt),
        grid=(k // tc,),
        in_specs=[pl.BlockSpec((n, tc), lambda i: (0, i))],
        out_specs=[pl.BlockSpec((tc, na), lambda i: (i, 0)), pl.BlockSpec((tc, nb), lambda i: (i, 0)),
                   pl.BlockSpec((piece_vt[1], tc), lambda i: (0, i))],
        out_shape=[jax.ShapeDtypeStruct((k, na), BF16), jax.ShapeDtypeStruct((k, nb), BF16),
                   jax.ShapeDtypeStruct((piece_vt[1], k), BF16)],
        compiler_params=_params("parallel"),
        name="weight_prep",
    )(w_in_t)


def _proj_t_kernel(wt_ref, x_ref, v0_ref, v1_ref):
    vt = _dot(wt_ref[...], x_ref[...], _NT)
    first = (lax.broadcasted_iota(jnp.int32, vt.shape, 0) % LANES) < HEAD_DIM
    v0_ref[...] = jnp.where(first, vt, 1.0).astype(v0_ref.dtype)
    v1_ref[...] = jnp.where(first, 1.0, vt).astype(v1_ref.dtype)


def _proj_t(wt, x):
    n, k = wt.shape
    m = x.shape[0]
    tm = min(ROWS_PROJ_T, m)
    assert m % tm == 0 and n % LANES == 0
    out = pl.BlockSpec((n, tm), lambda i: (0, i))
    return pl.pallas_call(
        _proj_t_kernel,
        grid=(m // tm,),
        in_specs=[pl.BlockSpec((n, k), lambda i: (0, 0)), pl.BlockSpec((tm, k), lambda i: (i, 0))],
        out_specs=[out, out],
        out_shape=[jax.ShapeDtypeStruct((n, m), BF16)] * 2,
        compiler_params=_params("parallel"),
        name="in_proj_vt",
    )(wt, x)


def _post_kernel(x_ref, z_ref, g_ref, o_ref):
    z = z_ref[...]
    ms = jnp.mean(z * z, axis=-1, keepdims=True)
    o_ref[...] = x_ref[...] + z * lax.rsqrt(ms + RMS_EPS) * g_ref[...]


def _post(x2d, z2d, g):
    m, d = x2d.shape
    tm = min(ROWS_ELEMENTWISE, m)
    assert m % tm == 0
    row = pl.BlockSpec((tm, d), lambda i: (i, 0))
    return pl.pallas_call(
        _post_kernel,
        grid=(m // tm,),
        in_specs=[row, row, pl.BlockSpec((1, d), lambda i: (0, 0))],
        out_specs=row,
        out_shape=jax.ShapeDtypeStruct((m, d), x2d.dtype),
        compiler_params=_params("parallel"),
        name="post_norm_residual",
    )(x2d, z2d, g.reshape(1, d))


def _rwkv_scan_kernel(*refs, n_units, L):
    r_ref, k_ref, v_ref, lora_ref = refs[:4]
    g_refs = refs[4:4 + n_units]
    (mu_r_ref, mu_k_ref, mu_v_ref, mu_lora_ref, w0_ref, wdu_ref, a0_ref, wiu_ref, kk_ref, ka_ref,
     rk_ref, lnw_ref, lnb_ref, o_ref, s_ref, carry_r, carry_k, carry_v, carry_lora) = refs[4 + n_units:]

    @pl.when(pl.program_id(2) == 0)
    def _():
        for z in (s_ref, carry_r, carry_k, carry_v, carry_lora):
            z[...] = jnp.zeros_like(z)

    rows = r_ref.shape[1]
    n_ch = rows // L
    first_row = lax.broadcasted_iota(jnp.int32, (rows, 1), 0) == 0

    def shift(x, carry_ref, mu):
        prev = jnp.where(first_row, carry_ref[...], pltpu.roll(x, 1, 0))
        carry_ref[...] = x[rows - 1:rows, :]
        return x + (prev - x) * mu

    r_all = shift(r_ref[0], carry_r, mu_r_ref[...])
    k_raw = shift(k_ref[0], carry_k, mu_k_ref[...])
    v_all = shift(v_ref[0], carry_v, mu_v_ref[...])
    lora = shift(lora_ref[0][:, :2 * LANES], carry_lora, mu_lora_ref[...])
    w_pre = -_softplus(-(w0_ref[...] + _dot_f32(jnp.tanh(lora[:, :LANES]), wdu_ref[...]))) - 0.5
    lw_all = -jnp.exp(w_pre)
    alpha = _sigmoid(a0_ref[...] + _dot_f32(lora[:, LANES:], wiu_ref[...]))
    k_all = k_raw * (1.0 + (alpha - 1.0) * ka_ref[...])
    kk_all = k_raw * kk_ref[...]
    hl = UNIT_HEADS * L
    n_sq = L.bit_length() - 2
    assert 2 ** (n_sq + 1) == L
    units = range(n_units)

    def blk(shape, d0, d1):
        return (lax.broadcasted_iota(jnp.int32, shape, 0) // d0) == (lax.broadcasted_iota(jnp.int32, shape, 1) // d1)

    t_row = lax.broadcasted_iota(jnp.int32, (hl, hl), 0)
    t_col = lax.broadcasted_iota(jnp.int32, (hl, hl), 1)
    same = blk((hl, hl), L, L)
    strict = same & (t_col < t_row)
    incl = same & (t_col <= t_row)
    eye = jnp.where(t_row == t_col, 1.0, 0.0)
    bd_rows = blk((hl, UNIT), L, HEAD_DIM)
    bd_state = blk((UNIT, UNIT), HEAD_DIM, HEAD_DIM)
    ones_bd = jnp.where(bd_state, 1.0, 0.0).astype(BF16)
    c_row = lax.broadcasted_iota(jnp.int32, (rows, rows), 0)
    c_col = lax.broadcasted_iota(jnp.int32, (rows, rows), 1)
    tril = jnp.where((c_row // L == c_col // L) & (c_row >= c_col), 1.0, 0.0).astype(BF16)

    def rep(x):
        return jnp.concatenate([x] * UNIT_HEADS, axis=0)

    def stack(x):
        return jnp.where(bd_rows, rep(x), 0.0)

    def unstack(xs):
        out = xs[0:L]
        for h in range(1, UNIT_HEADS):
            out = out + xs[h * L:(h + 1) * L]
        return out

    def headsum(x):
        return _dot_split(x, ones_bd, 2)

    def lanes(x, u):
        return x[:, u * UNIT:(u + 1) * UNIT]

    chains = [(ch, u) for ch in range(n_ch) for u in units]
    lw_hi, lw_lo = _split_bf16(lw_all, 2)
    cl_all = _dot(tril, lw_hi) + _dot(tril, lw_lo)
    kk_norm = [headsum(lanes(kk_all, u) * lanes(kk_all, u)) for u in units]
    kk_unit = [lanes(kk_all, u) * lax.rsqrt(jnp.maximum(kk_norm[u], 1e-24)) for u in units]

    ah, rh, e_last, lhs, rhs_b, rhs_k, upd_rhs, v = [], [], [], [], [], [], [], []
    for ch, u in chains:
        rs = slice(ch * L, (ch + 1) * L)
        sl = slice(u * UNIT, (u + 1) * UNIT)
        cl = cl_all[rs, sl]
        e_pos = jnp.exp(cl)
        e_neg = jnp.exp(-cl)
        e_rem = jnp.exp(cl[L - 1:L, :] - cl)
        kk = kk_unit[u][rs]
        a, b, k = -kk, kk * alpha[rs, sl], k_all[rs, sl]
        ah.append(a * jnp.exp(cl - lw_all[rs, sl]))
        rh.append(r_all[rs, sl] * e_pos)
        e_last.append(e_pos[L - 1:L, :])
        lhs.append(jnp.concatenate([stack(ah[-1]), stack(rh[-1])], axis=0).astype(BF16))
        rhs_b.append(rep(b * e_neg).astype(BF16))
        rhs_k.append(rep(k * e_neg).astype(BF16))
        upd_rhs.append(jnp.concatenate([b * e_rem, k * e_rem], axis=0).astype(BF16))
        v.append(v_all[rs, sl])

    ids = range(len(chains))
    a_b = [_dot(lhs[i], rhs_b[i], _NT) for i in ids]
    a_k = [_dot(lhs[i], rhs_k[i], _NT) for i in ids]
    pw = [jnp.where(strict, a_b[i][:hl], 0.0) for i in ids]
    a_ak = [jnp.where(strict, a_k[i][:hl], 0.0).astype(BF16) for i in ids]
    a_rb = [jnp.where(incl, a_b[i][hl:], 0.0).astype(BF16) for i in ids]
    a_rk = [jnp.where(incl, a_k[i][hl:], 0.0).astype(BF16) for i in ids]

    t_inv = [eye + pw[i] for i in ids]
    for _ in range(n_sq):
        pw = [_dot(pw[i].astype(BF16), pw[i].astype(BF16)) for i in ids]
        t_inv = [t_inv[i] + _dot(t_inv[i].astype(BF16), pw[i].astype(BF16)) for i in ids]
    v_s = [stack(v[i]).astype(BF16) for i in ids]

    s_cur = [s_ref[u] for u in units]
    y = [None] * len(chains)
    for ch in range(n_ch):
        cid = [ch * n_units + u for u in units]
        xr = [_dot(jnp.concatenate([ah[i], rh[i]], axis=0).astype(BF16), s_cur[u].astype(BF16), _NT)
              for u, i in zip(units, cid)]
        x_s = [stack(xr[u][:L]) + _dot(a_ak[i], v_s[i]) for u, i in zip(units, cid)]
        u_s = [_dot(t_inv[i].astype(BF16), x_s[u].astype(BF16)) for u, i in zip(units, cid)]
        y_s = [stack(xr[u][L:]) + _dot(a_rb[i], u_s[u].astype(BF16)) + _dot(a_rk[i], v_s[i])
               for u, i in zip(units, cid)]
        for u, i in zip(units, cid):
            upd = _dot(jnp.concatenate([unstack(u_s[u]), v[i]], axis=0).astype(BF16), upd_rhs[i], _TN)
            s_cur[u] = jnp.where(bd_state, s_cur[u] * e_last[i] + upd, 0.0)
            y[i] = unstack(y_s[u])
    for u in units:
        s_ref[u] = s_cur[u]

    inv_n = 1.0 / HEAD_DIM
    y = [jnp.concatenate([y[ch * n_units + u] for ch in range(n_ch)], axis=0) for u in units]
    d = [y[u] - headsum(y[u]) * inv_n for u in units]
    var = [headsum(d[u] * d[u]) * inv_n for u in units]
    bonus = [headsum(lanes(r_all, u) * lanes(k_all, u) * lanes(rk_ref[...], u)) * lanes(v_all, u) for u in units]
    for u in units:
        sl = slice(u * UNIT, (u + 1) * UNIT)
        yn = d[u] * lax.rsqrt(var[u] + GN_EPS) * lnw_ref[:, sl] + lnb_ref[:, sl]
        g = g_refs[u][0]
        o_ref[0, :, sl] = ((yn + bonus[u]) * (g * _sigmoid(g))).astype(o_ref.dtype)


def _rwkv_scan(pa3, off_r, off_lora, off_g, mu, w0, wdu, a0, wiu, k_k, k_a, r_k, ln_w, ln_b):
    b, t, _ = pa3.shape
    c = w0.shape[0]
    L = RWKV_CHUNK * min(RWKV_CHUNKS_PER_STEP, t // RWKV_CHUNK)
    n_units = min(RWKV_UNITS, c // UNIT)
    w = n_units * UNIT
    assert t % L == 0 and c % w == 0 and off_r % w == 0 and off_g % UNIT == 0 and off_lora % WIDE == 0
    rb, gb, lb, ng = off_r // w, off_g // UNIT, off_lora // WIDE, c // w

    def col(base):
        return pl.BlockSpec((1, L, w), lambda bi, gi, ci: (bi, ci, base + gi))

    def vec(base=0):
        return pl.BlockSpec((1, w), lambda bi, gi, ci: (0, base + gi))

    gates = [pl.BlockSpec((1, L, UNIT), lambda bi, gi, ci, u=u: (bi, ci, gb + gi * n_units + u))
             for u in range(n_units)]
    lora_w = pl.BlockSpec((wdu.shape[0], w), lambda bi, gi, ci: (0, gi))
    mu_rkv = mu[:3 * c].reshape(1, 3 * c)
    mu_lora = mu[3 * c:].reshape(1, 2 * LANES)
    row = lambda z: z.reshape(1, c)
    return pl.pallas_call(
        functools.partial(_rwkv_scan_kernel, n_units=n_units, L=RWKV_CHUNK),
        grid=(b, ng, t // L),
        in_specs=[col(rb), col(rb + ng), col(rb + 2 * ng),
                  pl.BlockSpec((1, L, WIDE), lambda bi, gi, ci: (bi, ci, lb))] + gates
                 + [vec(0), vec(ng), vec(2 * ng), pl.BlockSpec((1, 2 * LANES), lambda bi, gi, ci: (0, 0)),
                    vec(), lora_w, vec(), lora_w, vec(), vec(), vec(), vec(), vec()],
        out_specs=pl.BlockSpec((1, L, w), lambda bi, gi, ci: (bi, ci, gi)),
        out_shape=jax.ShapeDtypeStruct((b, t, c), BF16),
        scratch_shapes=[pltpu.VMEM((n_units, UNIT, UNIT), F32)] + [pltpu.VMEM((1, w), F32)] * 3
                       + [pltpu.VMEM((1, 2 * LANES), F32)],
        compiler_params=_params("parallel", "parallel", "arbitrary"),
        name="rwkv_scan",
    )(pa3, pa3, pa3, pa3, *([pa3] * n_units), mu_rkv, mu_rkv, mu_rkv, mu_lora,
      row(w0), wdu, row(a0), wiu, row(k_k), row(k_a), row(r_k), row(ln_w), row(ln_b))


def _fox_prefix_kernel(f_ref, bf_ref, k_ref, k0_ref, k1_ref, carry_ref):
    @pl.when(pl.program_id(1) == 0)
    def _():
        carry_ref[...] = jnp.zeros_like(carry_ref)

    x = f_ref[0][:, 2 * LANES:3 * LANES] + bf_ref[...]
    log_f = -_softplus(-x)
    tc = x.shape[0]
    tril = jnp.where(lax.broadcasted_iota(jnp.int32, (tc, tc), 0) >= lax.broadcasted_iota(jnp.int32, (tc, tc), 1),
                     1.0, 0.0).astype(BF16)
    c = carry_ref[...]
    for part in _split_bf16(log_f, 3):
        c = c + _dot(tril, part)
    carry_ref[...] = c[tc - 1:tc, :]

    fw = k_ref.shape[2]
    src = lax.broadcasted_iota(jnp.int32, (LANES, fw), 0)
    dst = lax.broadcasted_iota(jnp.int32, (LANES, fw), 1)
    lane = dst % LANES
    head = 2 * (dst // LANES) + jnp.where(lane < HEAD_DIM, 1, 0)
    bias = None
    for i, part in enumerate(_split_bf16(c * LOG2E, FOX_BIAS_PARTS)):
        sel = jnp.where((src == head) & (lane % HEAD_DIM == i), 1.0, 0.0).astype(BF16)
        d = _dot(part, sel)
        bias = d if bias is None else bias + d
    bias = bias.astype(BF16)
    k = k_ref[0]
    first = (lax.broadcasted_iota(jnp.int32, (tc, fw), 1) % LANES) < HEAD_DIM
    k0_ref[0] = jnp.where(first, k, bias)
    k1_ref[0] = jnp.where(first, bias, k)


def _fox_prefix(pt3, off_f, b_f, pb3, off_k, fw):
    b, t, _ = pt3.shape
    fh = b_f.shape[0]
    tc = min(ROWS_PREFIX, t)
    assert t % tc == 0 and fh <= LANES and fw == fh * HEAD_DIM and off_f % WIDE == 0 and off_k % fw == 0
    bf = jnp.zeros((1, LANES), F32).at[0, :fh].set(b_f)
    fb, kb = off_f // WIDE, off_k // fw
    out = pl.BlockSpec((1, tc, fw), lambda bi, i: (bi, i, 0))
    return pl.pallas_call(
        _fox_prefix_kernel,
        grid=(b, t // tc),
        in_specs=[pl.BlockSpec((1, tc, WIDE), lambda bi, i: (bi, i, fb)),
                  pl.BlockSpec((1, LANES), lambda bi, i: (0, 0)),
                  pl.BlockSpec((1, tc, fw), lambda bi, i: (bi, i, kb))],
        out_specs=[out, out],
        out_shape=[jax.ShapeDtypeStruct((b, t, fw), BF16)] * 2,
        scratch_shapes=[pltpu.VMEM((1, LANES), F32)],
        compiler_params=_params("parallel", "arbitrary"),
        name="fox_prefix",
    )(pt3, bf, pb3)


def _fox_attn_kernel(q_ref, k0_ref, k1_ref, v0_ref, v1_ref, g_ref, o_ref, m_ref, acc_ref, *, tk):
    qi = pl.program_id(2)
    tq = q_ref.shape[1]
    n_pairs = q_ref.shape[2] // LANES
    n_sub = tq // tk
    q_lane = lax.broadcasted_iota(jnp.int32, (tq, LANES), 1)
    q_head0 = q_lane < HEAD_DIM
    bias0 = jnp.where((q_lane >= HEAD_DIM) & (q_lane < HEAD_DIM + FOX_BIAS_PARTS), -1.0, 0.0)
    bias1 = jnp.where(q_lane < FOX_BIAS_PARTS, -1.0, 0.0)
    k_refs = (k0_ref, k1_ref)
    v_refs = (v0_ref, v1_ref)
    chains = [(pr, h) for pr in range(n_pairs) for h in range(2)]

    qs = []
    for pr in range(n_pairs):
        q = q_ref[0, :, pr * LANES:(pr + 1) * LANES].astype(F32) * (HEAD_DIM ** -0.5 * LOG2E)
        qs += [jnp.where(q_head0, q, bias0).astype(BF16), jnp.where(q_head0, bias1, q).astype(BF16)]

    m_ref[...] = jnp.full_like(m_ref, -jnp.inf)
    acc_ref[...] = jnp.zeros_like(acc_ref)

    def tile(j, valid):
        start = pl.multiple_of(j * tk, tk)
        scores = []
        for ci, (pr, h) in enumerate(chains):
            s = _dot(k_refs[h][0, pl.ds(start, tk), pr * LANES:(pr + 1) * LANES], qs[ci], _NT)
            scores.append(s if valid is None else jnp.where(valid, s, -jnp.inf))
        probs = []
        for ci, s in enumerate(scores):
            m = m_ref[ci]
            m_new = jnp.maximum(m, jnp.max(s, axis=0, keepdims=True))
            m_ref[ci] = m_new
            probs.append((jnp.exp2(m - m_new), jnp.exp2(s - m_new).astype(BF16)))
        for ci, (pr, h) in enumerate(chains):
            alpha, p = probs[ci]
            pv = _dot(v_refs[h][pr * LANES:(pr + 1) * LANES, pl.ds(start, tk)], p)
            acc_ref[ci] = alpha * acc_ref[ci] + pv

    def body(j, carry):
        tile(j, None)
        return carry

    lax.fori_loop(0, qi * n_sub, body, 0)
    key = lax.broadcasted_iota(jnp.int32, (tk, tq), 0)
    qry = lax.broadcasted_iota(jnp.int32, (tk, tq), 1)
    for d in range(n_sub):
        tile(qi * n_sub + d, key + d * tk <= qry)

    for pr in range(n_pairs):
        sl = slice(pr * LANES, (pr + 1) * LANES)
        acc0, acc1 = acc_ref[2 * pr], acc_ref[2 * pr + 1]
        out_t = jnp.concatenate([acc0[:HEAD_DIM] * (1.0 / acc0[HEAD_DIM:]),
                                 acc1[HEAD_DIM:] * (1.0 / acc1[:HEAD_DIM])], axis=0)
        g = g_ref[0, :, sl]
        o_ref[0, :, sl] = (out_t.T * (g * _sigmoid(g))).astype(o_ref.dtype)


def _fox_attn(pb3, off_q, k0, k1, v0t, v1t, pt3, off_g, fw):
    b, t, _ = pb3.shape
    tq = min(FOX_TQ, t)
    tk = min(FOX_TK, tq)
    n_pairs = min(FOX_PAIRS, fw // LANES)
    w = n_pairs * LANES
    assert t % tq == 0 and tq % tk == 0 and fw % w == 0 and off_q % w == 0 and off_g % w == 0
    qb, gb = off_q // w, off_g // w
    keys = pl.BlockSpec((1, t, w), lambda bi, p, i: (bi, 0, p))
    vals = pl.BlockSpec((w, t), lambda bi, p, i: (p, bi))
    return pl.pallas_call(
        functools.partial(_fox_attn_kernel, tk=tk),
        grid=(b, fw // w, t // tq),
        in_specs=[pl.BlockSpec((1, tq, w), lambda bi, p, i: (bi, i, qb + p)), keys, keys, vals, vals,
                  pl.BlockSpec((1, tq, w), lambda bi, p, i: (bi, i, gb + p))],
        out_specs=pl.BlockSpec((1, tq, w), lambda bi, p, i: (bi, i, p)),
        out_shape=jax.ShapeDtypeStruct((b, t, fw), BF16),
        scratch_shapes=[pltpu.VMEM((2 * n_pairs, 1, tq), F32), pltpu.VMEM((2 * n_pairs, LANES, tq), F32)],
        compiler_params=_params("parallel", "parallel", "arbitrary"),
        name="fox_attn",
    )(pb3, k0, k1, v0t, v1t, pt3)


def _mem_attn_kernel(q_ref, kv_ref, g_ref, o_ref):
    mw = q_ref.shape[2]
    hd = mw // MEM_HEADS
    scale = hd ** -0.5
    outs = []
    for h in range(MEM_HEADS):
        q = q_ref[0, :, h * hd:(h + 1) * hd]
        mk = kv_ref[0, :, h * hd:(h + 1) * hd]
        mv = kv_ref[0, :, mw + h * hd:mw + (h + 1) * hd]
        s = _dot(q, mk, _NT) * scale
        p = jnp.exp(s - jnp.max(s, axis=1, keepdims=True))
        l = jnp.sum(p, axis=1, keepdims=True)
        outs.append(_dot(p.astype(BF16), mv) / l)
    g = g_ref[0]
    o_ref[0] = (jnp.concatenate(outs, axis=1) * (g * _sigmoid(g))).astype(o_ref.dtype)


def _mem_attn(pq3, off_q, mkv3, pt3, off_g, mw):
    b, t, _ = pq3.shape
    tm = min(ROWS_MEM, t)
    assert t % tm == 0 and off_q % mw == 0 and off_g % mw == 0 and (mw // MEM_HEADS) % LANES == 0
    n_mem = mkv3.shape[1]
    qb, gb = off_q // mw, off_g // mw
    return pl.pallas_call(
        _mem_attn_kernel,
        grid=(b, t // tm),
        in_specs=[pl.BlockSpec((1, tm, mw), lambda bi, i: (bi, i, qb)),
                  pl.BlockSpec((1, n_mem, 2 * mw), lambda bi, i: (bi, 0, 0)),
                  pl.BlockSpec((1, tm, mw), lambda bi, i: (bi, i, gb))],
        out_specs=pl.BlockSpec((1, tm, mw), lambda bi, i: (bi, i, 0)),
        out_shape=jax.ShapeDtypeStruct((b, t, mw), BF16),
        compiler_params=_params("parallel", "parallel"),
        name="mem_attn",
    )(pq3, mkv3, pt3)


def _layer(x, mem, g_pre, w_in, mu_rwkv, w0, w_decay_up, a0, w_iclr_up, k_k, k_a, r_k,
           ln_x_w, ln_x_b, b_f, g_mem, w_mem_kv, w_out, g_post):
    b, t, d = x.shape
    n_mem = mem.shape[1]
    c = w0.shape[0]
    fh = b_f.shape[0]
    fw = fh * HEAD_DIM
    mw = w_mem_kv.shape[1] // 2
    lora = w_decay_up.shape[0]
    assert lora == LANES and w_iclr_up.shape[0] == LANES and fh <= WIDE
    assert w_in.shape[1] == 4 * c + 2 * lora + 4 * fw + fh + 2 * mw
    assert c % WIDE == 0 and fw % WIDE == 0 and mw % WIDE == 0

    o_g = 3 * c + 2 * lora
    o_fq = o_g + c
    o_fv = o_fq + 2 * fw
    o_f = o_fv + fw
    o_gfox = o_f + fh
    o_mq = o_gfox + fw
    o_gmq = o_mq + mw

    x2 = x.reshape(b * t, d)
    h = _rmsnorm(x2, g_pre, BF16)

    lora_f = WIDE - 2 * lora
    assert o_f % LANES == 0 and o_f + lora_f <= w_in.shape[1]
    pieces_a = ((o_g, c), (0, 3 * c), (o_gmq, mw), (o_gfox, fw), (3 * c, 2 * lora), (o_f, lora_f))
    pieces_b = ((o_fq, 2 * fw), (o_mq, mw))
    w_a, w_b, w_vt = _weight_prep(w_in.T, pieces_a, pieces_b, (o_fv, fw))
    off_a = {"g_rwkv": 0, "r": c, "g_mq": 4 * c, "g_fox": 4 * c + mw, "lora": 4 * c + mw + fw}
    off_b = {"fq": 0, "fk": fw, "mq": 2 * fw}

    pa3 = _matmul(h, w_a, F32, 2 * WIDE, "in_proj_a").reshape(b, t, -1)
    pb3 = _matmul(h, w_b, BF16, 2 * WIDE, "in_proj_b").reshape(b, t, -1)
    v0t, v1t = _proj_t(w_vt, h)

    y_rwkv = _rwkv_scan(pa3, off_a["r"], off_a["lora"], off_a["g_rwkv"], mu_rwkv, w0, w_decay_up, a0, w_iclr_up,
                        k_k, k_a, r_k.reshape(-1), ln_x_w, ln_x_b)

    k0, k1 = _fox_prefix(pa3, off_a["lora"], b_f, pb3, off_b["fk"], fw)
    y_fox = _fox_attn(pb3, off_b["fq"], k0, k1, v0t, v1t, pa3, off_a["g_fox"], fw)

    hm = _rmsnorm(mem.reshape(b * n_mem, d), g_mem, BF16)
    mkv3 = _proj([hm], [(w_mem_kv, (d, WIDE), lambda j: (0, j), (0, 0))], 2 * mw, WIDE, BF16,
                 "mem_kv_proj").reshape(b, n_mem, 2 * mw)
    y_mem = _mem_attn(pb3, off_b["mq"], mkv3, pa3, off_a["g_mq"], mw)

    assert c == fw and (c + fw) % mw == 0
    slabs = [(w_out, (c, WIDE), lambda j: (0, j), (0, 0)),
             (w_out, (fw, WIDE), lambda j: (1, j), (c, 0)),
             (w_out, (mw, WIDE), lambda j: ((c + fw) // mw, j), (c + fw, 0))]
    z = _proj([y_rwkv.reshape(b * t, c), y_fox.reshape(b * t, fw), y_mem.reshape(b * t, mw)], slabs, d, WIDE, F32,
              "out_proj")
    return _post(x2, z, g_post).reshape(b, t, d)


def kernel(x, mem, g_pre, w_in, mu_rwkv, w0, w_decay_up, a0, w_iclr_up, k_k, k_a, r_k, ln_x_w, ln_x_b, b_f,
           g_mem, w_mem_kv, w_out, g_post):
    for l in range(g_pre.shape[0]):
        x = _layer(x, mem, g_pre[l], w_in[l], mu_rwkv[l], w0[l], w_decay_up[l], a0[l], w_iclr_up[l], k_k[l],
                   k_a[l], r_k[l], ln_x_w[l], ln_x_b[l], b_f[l], g_mem[l], w_mem_kv[l], w_out[l], g_post[l])
    return x
```

```python
import functools

import jax
import jax.numpy as jnp
from jax import lax
from jax.experimental import pallas as pl
from jax.experimental.pallas import tpu as pltpu

HEAD_DIM = 64
MEM_HEADS = 4
RMS_EPS = 1e-6
GN_EPS = 64e-5
LOG2E = 1.4426950408889634

LANES = 128
MXU = 256
UNIT_HEADS = MXU // HEAD_DIM
UNIT = UNIT_HEADS * HEAD_DIM
RWKV_CHUNK = 64
WIDE = 512
VMEM_LIMIT_BYTES = 56 * 1024 * 1024
FOX_BIAS_PARTS = 3
FOX_V_ROWS = HEAD_DIM + 16

ROWS_ELEMENTWISE = 256
ROWS_PROJ = 1024
ROWS_PROJ_T = 512
ROWS_OUT = 512
COLS_WEIGHT_PREP = 128
RWKV_CHUNKS_PER_STEP = 2
ROWS_PREFIX = 512
ROWS_MEM = 512
FOX_TQ = 512
FOX_TK = 256
FOX_PAIRS = 4
RWKV_UNITS = 6

F32 = jnp.float32
BF16 = jnp.bfloat16

_NN = (((1,), (0,)), ((), ()))
_NT = (((1,), (1,)), ((), ()))
_TN = (((0,), (0,)), ((), ()))


def _dot(a, b, dims=_NN):
    return lax.dot_general(a, b, dims, preferred_element_type=F32)


def _split_bf16(x, n):
    parts = []
    rem = x
    for _ in range(n):
        p = rem.astype(BF16)
        parts.append(p)
        rem = rem - p.astype(F32)
    return parts


def _dot_split(a, b_bf16, n, dims=_NN):
    acc = None
    for p in _split_bf16(a, n):
        d = _dot(p, b_bf16, dims)
        acc = d if acc is None else acc + d
    return acc


def _dot_f32(a, b):
    a_hi, a_lo = _split_bf16(a, 2)
    b_hi, b_lo = _split_bf16(b, 2)
    return _dot(a_hi, b_hi) + (_dot(a_lo, b_hi) + _dot(a_hi, b_lo))


def _sigmoid(x):
    return 1.0 / (1.0 + jnp.exp(-x))


def _softplus(x):
    return jnp.maximum(x, 0.0) + jnp.log1p(jnp.exp(-jnp.abs(x)))


def _params(*semantics):
    return pltpu.CompilerParams(dimension_semantics=semantics, vmem_limit_bytes=VMEM_LIMIT_BYTES)


def _rmsnorm_kernel(x_ref, g_ref, o_ref):
    x = x_ref[...]
    ms = jnp.mean(x * x, axis=-1, keepdims=True)
    o_ref[...] = (x * lax.rsqrt(ms + RMS_EPS) * g_ref[...]).astype(o_ref.dtype)


def _rmsnorm(x2d, g, out_dtype):
    m, d = x2d.shape
    tm = min(ROWS_ELEMENTWISE, m)
    assert m % tm == 0
    return pl.pallas_call(
        _rmsnorm_kernel,
        grid=(m // tm,),
        in_specs=[pl.BlockSpec((tm, d), lambda i: (i, 0)), pl.BlockSpec((1, d), lambda i: (0, 0))],
        out_specs=pl.BlockSpec((tm, d), lambda i: (i, 0)),
        out_shape=jax.ShapeDtypeStruct((m, d), out_dtype),
        compiler_params=_params("parallel"),
        name="rmsnorm",
    )(x2d, g.reshape(1, d))


def _proj_kernel(*refs, n_x, placement):
    xs = refs[:n_x]
    ws = refs[n_x:n_x + len(placement)]
    o_ref, wb_ref = refs[n_x + len(placement):]

    @pl.when(pl.program_id(1) == 0)
    def _():
        for w_ref, (r0, c0) in zip(ws, placement):
            wb_ref[r0:r0 + w_ref.shape[0], c0:c0 + w_ref.shape[1]] = w_ref[...].astype(BF16)

    acc = None
    r0 = 0
    for x_ref in xs:
        kx = x_ref.shape[1]
        d = _dot(x_ref[...], wb_ref[r0:r0 + kx, :])
        acc = d if acc is None else acc + d
        r0 += kx
    o_ref[...] = acc.astype(o_ref.dtype)


def _proj(xs, weights, n_cols, tn, out_dtype, name):
    m = xs[0].shape[0]
    k_total = sum(x.shape[1] for x in xs)
    tm = min(ROWS_PROJ, m)
    tn = min(tn, n_cols)
    assert m % tm == 0 and n_cols % tn == 0
    in_specs = [pl.BlockSpec((tm, x.shape[1]), lambda j, i: (i, 0)) for x in xs]
    for _, shape, index_fn, _ in weights:
        in_specs.append(pl.BlockSpec(shape, lambda j, i, f=index_fn: f(j)))
    return pl.pallas_call(
        functools.partial(_proj_kernel, n_x=len(xs), placement=tuple(p for _, _, _, p in weights)),
        grid=(n_cols // tn, m // tm),
        in_specs=in_specs,
        out_specs=pl.BlockSpec((tm, tn), lambda j, i: (i, j)),
        out_shape=jax.ShapeDtypeStruct((m, n_cols), out_dtype),
        scratch_shapes=[pltpu.VMEM((k_total, tn), BF16)],
        compiler_params=_params("parallel", "arbitrary"),
        name=name,
    )(*xs, *[w for w, _, _, _ in weights])


def _matmul_kernel(x_ref, w_ref, o_ref):
    o_ref[...] = _dot(x_ref[...], w_ref[...]).astype(o_ref.dtype)


def _matmul(x, w, out_dtype, tn, name):
    m, k = x.shape
    n = w.shape[1]
    tm = min(ROWS_PROJ, m)
    tn = min(tn, n)
    assert m % tm == 0 and n % tn == 0
    return pl.pallas_call(
        _matmul_kernel,
        grid=(n // tn, m // tm),
        in_specs=[pl.BlockSpec((tm, k), lambda j, i: (i, 0)), pl.BlockSpec((k, tn), lambda j, i: (0, j))],
        out_specs=pl.BlockSpec((tm, tn), lambda j, i: (i, j)),
        out_shape=jax.ShapeDtypeStruct((m, n), out_dtype),
        compiler_params=_params("parallel", "parallel"),
        name=name,
    )(x, w)


def _weight_prep_kernel(wt_ref, wa_ref, wb_ref, wvt_ref, *, pieces_a, pieces_b, piece_vt):
    def gather(pieces, o_ref):
        dst = 0
        for src, width in pieces:
            o_ref[:, dst:dst + width] = wt_ref[src:src + width, :].T.astype(o_ref.dtype)
            dst += width

    gather(pieces_a, wa_ref)
    gather(pieces_b, wb_ref)
    src, width = piece_vt
    wvt_ref[...] = wt_ref[src:src + width, :].astype(wvt_ref.dtype)


def _weight_prep(w_in_t, pieces_a, pieces_b, piece_vt):
    n, k = w_in_t.shape
    tc = min(COLS_WEIGHT_PREP, k)
    assert k % tc == 0 and all(s % 8 == 0 and wd % 16 == 0 for s, wd in pieces_a + pieces_b + (piece_vt,))
    na = sum(wd for _, wd in pieces_a)
    nb = sum(wd for _, wd in pieces_b)
    return pl.pallas_call(
        functools.partial(_weight_prep_kernel, pieces_a=pieces_a, pieces_b=pieces_b, piece_vt=piece_vt),
        grid=(k // tc,),
        in_specs=[pl.BlockSpec((n, tc), lambda i: (0, i))],
        out_specs=[pl.BlockSpec((tc, na), lambda i: (i, 0)), pl.BlockSpec((tc, nb), lambda i: (i, 0)),
                   pl.BlockSpec((piece_vt[1], tc), lambda i: (0, i))],
        out_shape=[jax.ShapeDtypeStruct((k, na), BF16), jax.ShapeDtypeStruct((k, nb), BF16),
                   jax.ShapeDtypeStruct((piece_vt[1], k), BF16)],
        compiler_params=_params("parallel"),
        name="weight_prep",
    )(w_in_t)


def _proj_t_kernel(wt_ref, x_ref, v_ref):
    vt = _dot(wt_ref[...], x_ref[...], _NT)
    ones = jnp.ones((FOX_V_ROWS - HEAD_DIM, vt.shape[1]), v_ref.dtype)
    for h in range(vt.shape[0] // HEAD_DIM):
        v_ref[h * FOX_V_ROWS:h * FOX_V_ROWS + HEAD_DIM, :] = vt[h * HEAD_DIM:(h + 1) * HEAD_DIM].astype(v_ref.dtype)
        v_ref[h * FOX_V_ROWS + HEAD_DIM:(h + 1) * FOX_V_ROWS, :] = ones


def _proj_t(wt, x):
    n, k = wt.shape
    m = x.shape[0]
    tm = min(ROWS_PROJ_T, m)
    assert m % tm == 0 and n % HEAD_DIM == 0
    n_aug = n // HEAD_DIM * FOX_V_ROWS
    return pl.pallas_call(
        _proj_t_kernel,
        grid=(m // tm,),
        in_specs=[pl.BlockSpec((n, k), lambda i: (0, 0)), pl.BlockSpec((tm, k), lambda i: (i, 0))],
        out_specs=pl.BlockSpec((n_aug, tm), lambda i: (0, i)),
        out_shape=jax.ShapeDtypeStruct((n_aug, m), BF16),
        compiler_params=_params("parallel"),
        name="in_proj_vt",
    )(wt, x)


def _cast_kernel(x_ref, o_ref):
    o_ref[...] = x_ref[...].astype(o_ref.dtype)


def _cast(x, dtype, name):
    m, n = x.shape
    tm = min(ROWS_OUT, m)
    assert m % tm == 0
    return pl.pallas_call(
        _cast_kernel,
        grid=(m // tm,),
        in_specs=[pl.BlockSpec((tm, n), lambda i: (i, 0))],
        out_specs=pl.BlockSpec((tm, n), lambda i: (i, 0)),
        out_shape=jax.ShapeDtypeStruct((m, n), dtype),
        compiler_params=_params("parallel"),
        name=name,
    )(x)


def _out_proj_kernel(*refs, n_y):
    ys = refs[:n_y]
    w_ref, x_ref, g_ref, o_ref, ss_ref = refs[n_y:]
    j = pl.program_id(1)
    tn = w_ref.shape[1]
    d = o_ref.shape[1]

    acc = None
    r0 = 0
    for y_ref in ys:
        ky = y_ref.shape[1]
        part = _dot(y_ref[...], w_ref[r0:r0 + ky, :])
        acc = part if acc is None else acc + part
        r0 += ky

    @pl.when(j == 0)
    def _():
        ss_ref[...] = jnp.zeros_like(ss_ref)

    ss_ref[...] += jnp.sum(acc * acc, axis=1, keepdims=True)
    o_ref[:, pl.ds(pl.multiple_of(j * tn, tn), tn)] = acc

    @pl.when(j == pl.num_programs(1) - 1)
    def _():
        scale = lax.rsqrt(ss_ref[...] * (1.0 / d) + RMS_EPS)
        for c0 in range(0, d, tn):
            sl = slice(c0, c0 + tn)
            o_ref[:, sl] = x_ref[:, sl] + o_ref[:, sl] * scale * g_ref[:, sl]


def _out_proj(ys, w_bf16, x2d, g):
    m, d = x2d.shape
    tm = min(ROWS_OUT, m)
    tn = min(WIDE, d)
    assert m % tm == 0 and d % tn == 0 and sum(y.shape[1] for y in ys) == w_bf16.shape[0]
    row = pl.BlockSpec((tm, d), lambda i, j: (i, 0))
    return pl.pallas_call(
        functools.partial(_out_proj_kernel, n_y=len(ys)),
        grid=(m // tm, d // tn),
        in_specs=[pl.BlockSpec((tm, y.shape[1]), lambda i, j: (i, 0)) for y in ys]
                 + [pl.BlockSpec((w_bf16.shape[0], tn), lambda i, j: (0, j)), row,
                    pl.BlockSpec((1, d), lambda i, j: (0, 0))],
        out_specs=row,
        out_shape=jax.ShapeDtypeStruct((m, d), x2d.dtype),
        scratch_shapes=[pltpu.VMEM((tm, 1), F32)],
        compiler_params=_params("parallel", "arbitrary"),
        name="out_proj_post",
    )(*ys, w_bf16, x2d, g.reshape(1, d))


def _rwkv_scan_kernel(*refs, n_units, L):
    r_ref, k_ref, v_ref, lora_ref = refs[:4]
    g_refs = refs[4:4 + n_units]
    (mu_r_ref, mu_k_ref, mu_v_ref, mu_lora_ref, w0_ref, wdu_ref, a0_ref, wiu_ref, kk_ref, ka_ref,
     rk_ref, lnw_ref, lnb_ref, o_ref, s_ref, carry_r, carry_k, carry_v, carry_lora) = refs[4 + n_units:]

    @pl.when(pl.program_id(2) == 0)
    def _():
        for z in (s_ref, carry_r, carry_k, carry_v, carry_lora):
            z[...] = jnp.zeros_like(z)

    rows = r_ref.shape[1]
    n_ch = rows // L
    first_row = lax.broadcasted_iota(jnp.int32, (rows, 1), 0) == 0

    def shift(x, carry_ref, mu):
        prev = jnp.where(first_row, carry_ref[...], pltpu.roll(x, 1, 0))
        carry_ref[...] = x[rows - 1:rows, :]
        return x + (prev - x) * mu

    r_all = shift(r_ref[0], carry_r, mu_r_ref[...])
    k_raw = shift(k_ref[0], carry_k, mu_k_ref[...])
    v_all = shift(v_ref[0], carry_v, mu_v_ref[...])
    lora = shift(lora_ref[0][:, :2 * LANES], carry_lora, mu_lora_ref[...])
    w_pre = -_softplus(-(w0_ref[...] + _dot_f32(jnp.tanh(lora[:, :LANES]), wdu_ref[...]))) - 0.5
    lw_all = -jnp.exp(w_pre)
    alpha = _sigmoid(a0_ref[...] + _dot_f32(lora[:, LANES:], wiu_ref[...]))
    k_all = k_raw * (1.0 + (alpha - 1.0) * ka_ref[...])
    kk_all = k_raw * kk_ref[...]
    hl = UNIT_HEADS * L
    n_sq = L.bit_length() - 2
    assert 2 ** (n_sq + 1) == L
    units = range(n_units)

    def blk(shape, d0, d1):
        return (lax.broadcasted_iota(jnp.int32, shape, 0) // d0) == (lax.broadcasted_iota(jnp.int32, shape, 1) // d1)

    t_row = lax.broadcasted_iota(jnp.int32, (hl, hl), 0)
    t_col = lax.broadcasted_iota(jnp.int32, (hl, hl), 1)
    same = blk((hl, hl), L, L)
    strict = same & (t_col < t_row)
    incl = same & (t_col <= t_row)
    eye = jnp.where(t_row == t_col, 1.0, 0.0)
    bd_rows = blk((hl, UNIT), L, HEAD_DIM)
    bd_state = blk((UNIT, UNIT), HEAD_DIM, HEAD_DIM)
    ones_bd = jnp.where(bd_state, 1.0, 0.0).astype(BF16)
    c_row = lax.broadcasted_iota(jnp.int32, (rows, rows), 0)
    c_col = lax.broadcasted_iota(jnp.int32, (rows, rows), 1)
    tril = jnp.where((c_row // L == c_col // L) & (c_row >= c_col), 1.0, 0.0).astype(BF16)

    def rep(x):
        return jnp.concatenate([x] * UNIT_HEADS, axis=0)

    def stack(x):
        return jnp.where(bd_rows, rep(x), 0.0)

    def unstack(xs):
        out = xs[0:L]
        for h in range(1, UNIT_HEADS):
            out = out + xs[h * L:(h + 1) * L]
        return out

    def headsum(x):
        return _dot_split(x, ones_bd, 2)

    def lanes(x, u):
        return x[:, u * UNIT:(u + 1) * UNIT]

    chains = [(ch, u) for ch in range(n_ch) for u in units]
    lw_hi, lw_lo = _split_bf16(lw_all, 2)
    cl_all = _dot(tril, lw_hi) + _dot(tril, lw_lo)
    kk_norm = [headsum(lanes(kk_all, u) * lanes(kk_all, u)) for u in units]
    kk_unit = [lanes(kk_all, u) * lax.rsqrt(jnp.maximum(kk_norm[u], 1e-24)) for u in units]

    ah, rh, e_last, lhs, rhs_b, rhs_k, upd_rhs, v = [], [], [], [], [], [], [], []
    for ch, u in chains:
        rs = slice(ch * L, (ch + 1) * L)
        sl = slice(u * UNIT, (u + 1) * UNIT)
        cl = cl_all[rs, sl]
        e_pos = jnp.exp(cl)
        e_neg = jnp.exp(-cl)
        e_rem = jnp.exp(cl[L - 1:L, :] - cl)
        kk = kk_unit[u][rs]
        a, b, k = -kk, kk * alpha[rs, sl], k_all[rs, sl]
        ah.append(a * jnp.exp(cl - lw_all[rs, sl]))
        rh.append(r_all[rs, sl] * e_pos)
        e_last.append(e_pos[L - 1:L, :])
        lhs.append(jnp.concatenate([stack(ah[-1]), stack(rh[-1])], axis=0).astype(BF16))
        rhs_b.append(rep(b * e_neg).astype(BF16))
        rhs_k.append(rep(k * e_neg).astype(BF16))
        upd_rhs.append(jnp.concatenate([b * e_rem, k * e_rem], axis=0).astype(BF16))
        v.append(v_all[rs, sl])

    ids = range(len(chains))
    a_b = [_dot(lhs[i], rhs_b[i], _NT) for i in ids]
    a_k = [_dot(lhs[i], rhs_k[i], _NT) for i in ids]
    pw = [jnp.where(strict, a_b[i][:hl], 0.0) for i in ids]
    a_ak = [jnp.where(strict, a_k[i][:hl], 0.0).astype(BF16) for i in ids]
    a_rb = [jnp.where(incl, a_b[i][hl:], 0.0).astype(BF16) for i in ids]
    a_rk = [jnp.where(incl, a_k[i][hl:], 0.0).astype(BF16) for i in ids]

    t_inv = [eye + pw[i] for i in ids]
    for _ in range(n_sq):
        pw = [_dot(pw[i].astype(BF16), pw[i].astype(BF16)) for i in ids]
        t_inv = [t_inv[i] + _dot(t_inv[i].astype(BF16), pw[i].astype(BF16)) for i in ids]
    v_s = [stack(v[i]).astype(BF16) for i in ids]

    s_cur = [s_ref[u] for u in units]
    y = [None] * len(chains)
    for ch in range(n_ch):
        cid = [ch * n_units + u for u in units]
        xr = [_dot(jnp.concatenate([ah[i], rh[i]], axis=0).astype(BF16), s_cur[u].astype(BF16), _NT)
              for u, i in zip(units, cid)]
        x_s = [stack(xr[u][:L]) + _dot(a_ak[i], v_s[i]) for u, i in zip(units, cid)]
        u_s = [_dot(t_inv[i].astype(BF16), x_s[u].astype(BF16)) for u, i in zip(units, cid)]
        y_s = [stack(xr[u][L:]) + _dot(a_rb[i], u_s[u].astype(BF16)) + _dot(a_rk[i], v_s[i])
               for u, i in zip(units, cid)]
        for u, i in zip(units, cid):
            upd = _dot(jnp.concatenate([unstack(u_s[u]), v[i]], axis=0).astype(BF16), upd_rhs[i], _TN)
            s_cur[u] = jnp.where(bd_state, s_cur[u] * e_last[i] + upd, 0.0)
            y[i] = unstack(y_s[u])
    for u in units:
        s_ref[u] = s_cur[u]

    inv_n = 1.0 / HEAD_DIM
    y = [jnp.concatenate([y[ch * n_units + u] for ch in range(n_ch)], axis=0) for u in units]
    d = [y[u] - headsum(y[u]) * inv_n for u in units]
    var = [headsum(d[u] * d[u]) * inv_n for u in units]
    bonus = [headsum(lanes(r_all, u) * lanes(k_all, u) * lanes(rk_ref[...], u)) * lanes(v_all, u) for u in units]
    for u in units:
        sl = slice(u * UNIT, (u + 1) * UNIT)
        yn = d[u] * lax.rsqrt(var[u] + GN_EPS) * lnw_ref[:, sl] + lnb_ref[:, sl]
        g = g_refs[u][0]
        o_ref[0, :, sl] = ((yn + bonus[u]) * (g * _sigmoid(g))).astype(o_ref.dtype)


def _rwkv_scan(pa3, off_r, off_lora, off_g, mu, w0, wdu, a0, wiu, k_k, k_a, r_k, ln_w, ln_b):
    b, t, _ = pa3.shape
    c = w0.shape[0]
    L = RWKV_CHUNK * min(RWKV_CHUNKS_PER_STEP, t // RWKV_CHUNK)
    n_units = min(RWKV_UNITS, c // UNIT)
    w = n_units * UNIT
    assert t % L == 0 and c % w == 0 and off_r % w == 0 and off_g % UNIT == 0 and off_lora % WIDE == 0
    rb, gb, lb, ng = off_r // w, off_g // UNIT, off_lora // WIDE, c // w

    def col(base):
        return pl.BlockSpec((1, L, w), lambda bi, gi, ci: (bi, ci, base + gi))

    def vec(base=0):
        return pl.BlockSpec((1, w), lambda bi, gi, ci: (0, base + gi))

    gates = [pl.BlockSpec((1, L, UNIT), lambda bi, gi, ci, u=u: (bi, ci, gb + gi * n_units + u))
             for u in range(n_units)]
    lora_w = pl.BlockSpec((wdu.shape[0], w), lambda bi, gi, ci: (0, gi))
    mu_rkv = mu[:3 * c].reshape(1, 3 * c)
    mu_lora = mu[3 * c:].reshape(1, 2 * LANES)
    row = lambda z: z.reshape(1, c)
    return pl.pallas_call(
        functools.partial(_rwkv_scan_kernel, n_units=n_units, L=RWKV_CHUNK),
        grid=(b, ng, t // L),
        in_specs=[col(rb), col(rb + ng), col(rb + 2 * ng),
                  pl.BlockSpec((1, L, WIDE), lambda bi, gi, ci: (bi, ci, lb))] + gates
                 + [vec(0), vec(ng), vec(2 * ng), pl.BlockSpec((1, 2 * LANES), lambda bi, gi, ci: (0, 0)),
                    vec(), lora_w, vec(), lora_w, vec(), vec(), vec(), vec(), vec()],
        out_specs=pl.BlockSpec((1, L, w), lambda bi, gi, ci: (bi, ci, gi)),
        out_shape=jax.ShapeDtypeStruct((b, t, c), BF16),
        scratch_shapes=[pltpu.VMEM((n_units, UNIT, UNIT), F32)] + [pltpu.VMEM((1, w), F32)] * 3
                       + [pltpu.VMEM((1, 2 * LANES), F32)],
        compiler_params=_params("parallel", "parallel", "arbitrary"),
        name="rwkv_scan",
    )(pa3, pa3, pa3, pa3, *([pa3] * n_units), mu_rkv, mu_rkv, mu_rkv, mu_lora,
      row(w0), wdu, row(a0), wiu, row(k_k), row(k_a), row(r_k), row(ln_w), row(ln_b))


def _fox_prefix_kernel(f_ref, bf_ref, k_ref, k0_ref, k1_ref, carry_ref):
    @pl.when(pl.program_id(1) == 0)
    def _():
        carry_ref[...] = jnp.zeros_like(carry_ref)

    x = f_ref[0][:, 2 * LANES:3 * LANES] + bf_ref[...]
    log_f = -_softplus(-x)
    tc = x.shape[0]
    tril = jnp.where(lax.broadcasted_iota(jnp.int32, (tc, tc), 0) >= lax.broadcasted_iota(jnp.int32, (tc, tc), 1),
                     1.0, 0.0).astype(BF16)
    c = carry_ref[...]
    for part in _split_bf16(log_f, 3):
        c = c + _dot(tril, part)
    carry_ref[...] = c[tc - 1:tc, :]

    fw = k_ref.shape[2]
    src = lax.broadcasted_iota(jnp.int32, (LANES, fw), 0)
    dst = lax.broadcasted_iota(jnp.int32, (LANES, fw), 1)
    lane = dst % LANES
    head = 2 * (dst // LANES) + jnp.where(lane < HEAD_DIM, 1, 0)
    bias = None
    for i, part in enumerate(_split_bf16(c * LOG2E, FOX_BIAS_PARTS)):
        sel = jnp.where((src == head) & (lane % HEAD_DIM == i), 1.0, 0.0).astype(BF16)
        d = _dot(part, sel)
        bias = d if bias is None else bias + d
    bias = bias.astype(BF16)
    k = k_ref[0]
    first = (lax.broadcasted_iota(jnp.int32, (tc, fw), 1) % LANES) < HEAD_DIM
    k0_ref[0] = jnp.where(first, k, bias)
    k1_ref[0] = jnp.where(first, bias, k)


def _fox_prefix(pt3, off_f, b_f, pb3, off_k, fw):
    b, t, _ = pt3.shape
    fh = b_f.shape[0]
    tc = min(ROWS_PREFIX, t)
    assert t % tc == 0 and fh <= LANES and fw == fh * HEAD_DIM and off_f % WIDE == 0 and off_k % fw == 0
    bf = jnp.zeros((1, LANES), F32).at[0, :fh].set(b_f)
    fb, kb = off_f // WIDE, off_k // fw
    out = pl.BlockSpec((1, tc, fw), lambda bi, i: (bi, i, 0))
    return pl.pallas_call(
        _fox_prefix_kernel,
        grid=(b, t // tc),
        in_specs=[pl.BlockSpec((1, tc, WIDE), lambda bi, i: (bi, i, fb)),
                  pl.BlockSpec((1, LANES), lambda bi, i: (0, 0)),
                  pl.BlockSpec((1, tc, fw), lambda bi, i: (bi, i, kb))],
        out_specs=[out, out],
        out_shape=[jax.ShapeDtypeStruct((b, t, fw), BF16)] * 2,
        scratch_shapes=[pltpu.VMEM((1, LANES), F32)],
        compiler_params=_params("parallel", "arbitrary"),
        name="fox_prefix",
    )(pt3, bf, pb3)


def _fox_attn_kernel(q_ref, k0_ref, k1_ref, v_ref, g_ref, o_ref, m_ref, acc_ref, *, tk):
    qi = pl.program_id(2)
    tq = q_ref.shape[1]
    n_pairs = q_ref.shape[2] // LANES
    n_sub = tq // tk
    q_lane = lax.broadcasted_iota(jnp.int32, (tq, LANES), 1)
    q_head0 = q_lane < HEAD_DIM
    bias0 = jnp.where((q_lane >= HEAD_DIM) & (q_lane < HEAD_DIM + FOX_BIAS_PARTS), -1.0, 0.0)
    bias1 = jnp.where(q_lane < FOX_BIAS_PARTS, -1.0, 0.0)
    k_refs = (k0_ref, k1_ref)
    chains = [(pr, h) for pr in range(n_pairs) for h in range(2)]

    qs = []
    for pr in range(n_pairs):
        q = q_ref[0, :, pr * LANES:(pr + 1) * LANES].astype(F32) * (HEAD_DIM ** -0.5 * LOG2E)
        qs += [jnp.where(q_head0, q, bias0).astype(BF16), jnp.where(q_head0, bias1, q).astype(BF16)]

    m_ref[...] = jnp.full_like(m_ref, -jnp.inf)
    acc_ref[...] = jnp.zeros_like(acc_ref)

    def tile(j, valid):
        start = pl.multiple_of(j * tk, tk)
        scores = []
        for ci, (pr, h) in enumerate(chains):
            s = _dot(k_refs[h][0, pl.ds(start, tk), pr * LANES:(pr + 1) * LANES], qs[ci], _NT)
            scores.append(s if valid is None else jnp.where(valid, s, -jnp.inf))
        probs = []
        for ci, s in enumerate(scores):
            m = m_ref[ci]
            m_new = jnp.maximum(m, jnp.max(s, axis=0, keepdims=True))
            m_ref[ci] = m_new
            probs.append((jnp.exp2(m - m_new), jnp.exp2(s - m_new).astype(BF16)))
        for ci, (alpha, p) in enumerate(probs):
            pv = _dot(v_ref[ci * FOX_V_ROWS:(ci + 1) * FOX_V_ROWS, pl.ds(start, tk)], p)
            acc_ref[ci] = alpha * acc_ref[ci] + pv

    def body(j, carry):
        tile(j, None)
        return carry

    lax.fori_loop(0, qi * n_sub, body, 0)
    key = lax.broadcasted_iota(jnp.int32, (tk, tq), 0)
    qry = lax.broadcasted_iota(jnp.int32, (tk, tq), 1)
    for d in range(n_sub):
        tile(qi * n_sub + d, key + d * tk <= qry)

    for pr in range(n_pairs):
        sl = slice(pr * LANES, (pr + 1) * LANES)
        out_t = jnp.concatenate(
            [acc_ref[ci][:HEAD_DIM] * (1.0 / acc_ref[ci][HEAD_DIM:HEAD_DIM + 1]) for ci in (2 * pr, 2 * pr + 1)], axis=0)
        g = g_ref[0, :, sl]
        o_ref[0, :, sl] = (out_t.T * (g * _sigmoid(g))).astype(o_ref.dtype)


def _fox_attn(pb3, off_q, k0, k1, vt, pt3, off_g, fw):
    b, t, _ = pb3.shape
    tq = min(FOX_TQ, t)
    tk = min(FOX_TK, tq)
    n_pairs = min(FOX_PAIRS, fw // LANES)
    w = n_pairs * LANES
    v_rows = 2 * n_pairs * FOX_V_ROWS
    assert t % tq == 0 and tq % tk == 0 and fw % w == 0 and off_q % w == 0 and off_g % w == 0
    qb, gb = off_q // w, off_g // w
    keys = pl.BlockSpec((1, t, w), lambda bi, p, i: (bi, 0, p))
    return pl.pallas_call(
        functools.partial(_fox_attn_kernel, tk=tk),
        grid=(b, fw // w, t // tq),
        in_specs=[pl.BlockSpec((1, tq, w), lambda bi, p, i: (bi, i, qb + p)), keys, keys,
                  pl.BlockSpec((v_rows, t), lambda bi, p, i: (p, bi)),
                  pl.BlockSpec((1, tq, w), lambda bi, p, i: (bi, i, gb + p))],
        out_specs=pl.BlockSpec((1, tq, w), lambda bi, p, i: (bi, i, p)),
        out_shape=jax.ShapeDtypeStruct((b, t, fw), BF16),
        scratch_shapes=[pltpu.VMEM((2 * n_pairs, 1, tq), F32), pltpu.VMEM((2 * n_pairs, FOX_V_ROWS, tq), F32)],
        compiler_params=_params("parallel", "parallel", "arbitrary"),
        name="fox_attn",
    )(pb3, k0, k1, vt, pt3)


def _mem_attn_kernel(q_ref, kv_ref, g_ref, o_ref):
    mw = q_ref.shape[2]
    hd = mw // MEM_HEADS
    scale = hd ** -0.5
    outs = []
    for h in range(MEM_HEADS):
        q = q_ref[0, :, h * hd:(h + 1) * hd]
        mk = kv_ref[0, :, h * hd:(h + 1) * hd]
        mv = kv_ref[0, :, mw + h * hd:mw + (h + 1) * hd]
        s = _dot(q, mk, _NT) * scale
        p = jnp.exp(s - jnp.max(s, axis=1, keepdims=True))
        l = jnp.sum(p, axis=1, keepdims=True)
        outs.append(_dot(p.astype(BF16), mv) / l)
    g = g_ref[0]
    o_ref[0] = (jnp.concatenate(outs, axis=1) * (g * _sigmoid(g))).astype(o_ref.dtype)


def _mem_attn(pq3, off_q, mkv3, pt3, off_g, mw):
    b, t, _ = pq3.shape
    tm = min(ROWS_MEM, t)
    assert t % tm == 0 and off_q % mw == 0 and off_g % mw == 0 and (mw // MEM_HEADS) % LANES == 0
    n_mem = mkv3.shape[1]
    qb, gb = off_q // mw, off_g // mw
    return pl.pallas_call(
        _mem_attn_kernel,
        grid=(b, t // tm),
        in_specs=[pl.BlockSpec((1, tm, mw), lambda bi, i: (bi, i, qb)),
                  pl.BlockSpec((1, n_mem, 2 * mw), lambda bi, i: (bi, 0, 0)),
                  pl.BlockSpec((1, tm, mw), lambda bi, i: (bi, i, gb))],
        out_specs=pl.BlockSpec((1, tm, mw), lambda bi, i: (bi, i, 0)),
        out_shape=jax.ShapeDtypeStruct((b, t, mw), BF16),
        compiler_params=_params("parallel", "parallel"),
        name="mem_attn",
    )(pq3, mkv3, pt3)


def _layer(x, mem, g_pre, w_in, mu_rwkv, w0, w_decay_up, a0, w_iclr_up, k_k, k_a, r_k,
           ln_x_w, ln_x_b, b_f, g_mem, w_mem_kv, w_out, g_post):
    b, t, d = x.shape
    n_mem = mem.shape[1]
    c = w0.shape[0]
    fh = b_f.shape[0]
    fw = fh * HEAD_DIM
    mw = w_mem_kv.shape[1] // 2
    lora = w_decay_up.shape[0]
    assert lora == LANES and w_iclr_up.shape[0] == LANES and fh <= WIDE
    assert w_in.shape[1] == 4 * c + 2 * lora + 4 * fw + fh + 2 * mw
    assert c % WIDE == 0 and fw % WIDE == 0 and mw % WIDE == 0

    o_g = 3 * c + 2 * lora
    o_fq = o_g + c
    o_fv = o_fq + 2 * fw
    o_f = o_fv + fw
    o_gfox = o_f + fh
    o_mq = o_gfox + fw
    o_gmq = o_mq + mw

    x2 = x.reshape(b * t, d)
    h = _rmsnorm(x2, g_pre, BF16)

    lora_f = WIDE - 2 * lora
    assert o_f % LANES == 0 and o_f + lora_f <= w_in.shape[1]
    pieces_a = ((o_g, c), (0, 3 * c), (o_gmq, mw), (o_gfox, fw), (3 * c, 2 * lora), (o_f, lora_f))
    pieces_b = ((o_fq, 2 * fw), (o_mq, mw))
    w_a, w_b, w_vt = _weight_prep(w_in.T, pieces_a, pieces_b, (o_fv, fw))
    off_a = {"g_rwkv": 0, "r": c, "g_mq": 4 * c, "g_fox": 4 * c + mw, "lora": 4 * c + mw + fw}
    off_b = {"fq": 0, "fk": fw, "mq": 2 * fw}

    pa3 = _matmul(h, w_a, F32, 2 * WIDE, "in_proj_a").reshape(b, t, -1)
    pb3 = _matmul(h, w_b, BF16, 2 * WIDE, "in_proj_b").reshape(b, t, -1)
    vt = _proj_t(w_vt, h)

    y_rwkv = _rwkv_scan(pa3, off_a["r"], off_a["lora"], off_a["g_rwkv"], mu_rwkv, w0, w_decay_up, a0, w_iclr_up,
                        k_k, k_a, r_k.reshape(-1), ln_x_w, ln_x_b)

    k0, k1 = _fox_prefix(pa3, off_a["lora"], b_f, pb3, off_b["fk"], fw)
    y_fox = _fox_attn(pb3, off_b["fq"], k0, k1, vt, pa3, off_a["g_fox"], fw)

    hm = _rmsnorm(mem.reshape(b * n_mem, d), g_mem, BF16)
    mkv3 = _proj([hm], [(w_mem_kv, (d, WIDE), lambda j: (0, j), (0, 0))], 2 * mw, WIDE, BF16,
                 "mem_kv_proj").reshape(b, n_mem, 2 * mw)
    y_mem = _mem_attn(pb3, off_b["mq"], mkv3, pa3, off_a["g_mq"], mw)

    ys = [y_rwkv.reshape(b * t, c), y_fox.reshape(b * t, fw), y_mem.reshape(b * t, mw)]
    return _out_proj(ys, _cast(w_out, BF16, "w_out_cast"), x2, g_post).reshape(b, t, d)


def kernel(x, mem, g_pre, w_in, mu_rwkv, w0, w_decay_up, a0, w_iclr_up, k_k, k_a, r_k, ln_x_w, ln_x_b, b_f,
           g_mem, w_mem_kv, w_out, g_post):
    for l in range(g_pre.shape[0]):
        x = _layer(x, mem, g_pre[l], w_in[l], mu_rwkv[l], w0[l], w_decay_up[l], a0[l], w_iclr_up[l], k_k[l],
                   k_a[l], r_k[l], ln_x_w[l], ln_x_b[l], b_f[l], g_mem[l], w_mem_kv[l], w_out[l], g_post[l])
    return x
```

```python
import functools

import jax
import jax.numpy as jnp
from jax import lax
from jax.experimental import pallas as pl
from jax.experimental.pallas import tpu as pltpu

HEAD_DIM = 64
MEM_HEADS = 4
RMS_EPS = 1e-6
GN_EPS = 64e-5
LOG2E = 1.4426950408889634

LANES = 128
MXU = 256
UNIT_HEADS = MXU // HEAD_DIM
UNIT = UNIT_HEADS * HEAD_DIM
RWKV_CHUNK = 64
WIDE = 512
VMEM_LIMIT_BYTES = 56 * 1024 * 1024
FOX_BIAS_PARTS = 3
FOX_V_ROWS = HEAD_DIM + 16

ROWS_ELEMENTWISE = 256
ROWS_PROJ = 1024
ROWS_PROJ_T = 512
ROWS_OUT = 512
COLS_WEIGHT_PREP = 128
RWKV_CHUNKS_PER_STEP = 2
ROWS_PREFIX = 512
ROWS_MEM = 512
FOX_TQ = 512
FOX_TK = 256
FOX_PAIRS = 4
RWKV_UNITS = 6

F32 = jnp.float32
BF16 = jnp.bfloat16

_NN = (((1,), (0,)), ((), ()))
_NT = (((1,), (1,)), ((), ()))
_TN = (((0,), (0,)), ((), ()))


def _dot(a, b, dims=_NN):
    return lax.dot_general(a, b, dims, preferred_element_type=F32)


def _split_bf16(x, n):
    parts = []
    rem = x
    for _ in range(n):
        p = rem.astype(BF16)
        parts.append(p)
        rem = rem - p.astype(F32)
    return parts


def _dot_split(a, b_bf16, n, dims=_NN):
    acc = None
    for p in _split_bf16(a, n):
        d = _dot(p, b_bf16, dims)
        acc = d if acc is None else acc + d
    return acc


def _dot_f32(a, b):
    a_hi, a_lo = _split_bf16(a, 2)
    b_hi, b_lo = _split_bf16(b, 2)
    return _dot(a_hi, b_hi) + (_dot(a_lo, b_hi) + _dot(a_hi, b_lo))


def _sigmoid(x):
    return 1.0 / (1.0 + jnp.exp(-x))


def _softplus(x):
    return jnp.maximum(x, 0.0) + jnp.log1p(jnp.exp(-jnp.abs(x)))


def _params(*semantics):
    return pltpu.CompilerParams(dimension_semantics=semantics, vmem_limit_bytes=VMEM_LIMIT_BYTES)


def _rmsnorm_kernel(x_ref, g_ref, o_ref):
    x = x_ref[...]
    ms = jnp.mean(x * x, axis=-1, keepdims=True)
    o_ref[...] = (x * lax.rsqrt(ms + RMS_EPS) * g_ref[...]).astype(o_ref.dtype)


def _rmsnorm(x2d, g, out_dtype):
    m, d = x2d.shape
    tm = min(ROWS_ELEMENTWISE, m)
    assert m % tm == 0
    return pl.pallas_call(
        _rmsnorm_kernel,
        grid=(m // tm,),
        in_specs=[pl.BlockSpec((tm, d), lambda i: (i, 0)), pl.BlockSpec((1, d), lambda i: (0, 0))],
        out_specs=pl.BlockSpec((tm, d), lambda i: (i, 0)),
        out_shape=jax.ShapeDtypeStruct((m, d), out_dtype),
        compiler_params=_params("parallel"),
        name="rmsnorm",
    )(x2d, g.reshape(1, d))


def _proj_kernel(*refs, n_x, placement):
    xs = refs[:n_x]
    ws = refs[n_x:n_x + len(placement)]
    o_ref, wb_ref = refs[n_x + len(placement):]

    @pl.when(pl.program_id(1) == 0)
    def _():
        for w_ref, (r0, c0) in zip(ws, placement):
            wb_ref[r0:r0 + w_ref.shape[0], c0:c0 + w_ref.shape[1]] = w_ref[...].astype(BF16)

    acc = None
    r0 = 0
    for x_ref in xs:
        kx = x_ref.shape[1]
        d = _dot(x_ref[...], wb_ref[r0:r0 + kx, :])
        acc = d if acc is None else acc + d
        r0 += kx
    o_ref[...] = acc.astype(o_ref.dtype)


def _proj(xs, weights, n_cols, tn, out_dtype, name):
    m = xs[0].shape[0]
    k_total = sum(x.shape[1] for x in xs)
    tm = min(ROWS_PROJ, m)
    tn = min(tn, n_cols)
    assert m % tm == 0 and n_cols % tn == 0
    in_specs = [pl.BlockSpec((tm, x.shape[1]), lambda j, i: (i, 0)) for x in xs]
    for _, shape, index_fn, _ in weights:
        in_specs.append(pl.BlockSpec(shape, lambda j, i, f=index_fn: f(j)))
    return pl.pallas_call(
        functools.partial(_proj_kernel, n_x=len(xs), placement=tuple(p for _, _, _, p in weights)),
        grid=(n_cols // tn, m // tm),
        in_specs=in_specs,
        out_specs=pl.BlockSpec((tm, tn), lambda j, i: (i, j)),
        out_shape=jax.ShapeDtypeStruct((m, n_cols), out_dtype),
        scratch_shapes=[pltpu.VMEM((k_total, tn), BF16)],
        compiler_params=_params("parallel", "arbitrary"),
        name=name,
    )(*xs, *[w for w, _, _, _ in weights])


def _matmul_kernel(x_ref, w_ref, o_ref):
    o_ref[...] = _dot(x_ref[...], w_ref[...]).astype(o_ref.dtype)


def _matmul(x, w, out_dtype, tn, name):
    m, k = x.shape
    n = w.shape[1]
    tm = min(ROWS_PROJ, m)
    tn = min(tn, n)
    assert m % tm == 0 and n % tn == 0
    return pl.pallas_call(
        _matmul_kernel,
        grid=(n // tn, m // tm),
        in_specs=[pl.BlockSpec((tm, k), lambda j, i: (i, 0)), pl.BlockSpec((k, tn), lambda j, i: (0, j))],
        out_specs=pl.BlockSpec((tm, tn), lambda j, i: (i, j)),
        out_shape=jax.ShapeDtypeStruct((m, n), out_dtype),
        compiler_params=_params("parallel", "parallel"),
        name=name,
    )(x, w)


def _weight_prep_kernel(wt_ref, wa_ref, wb_ref, wvt_ref, *, pieces_a, pieces_b, piece_vt):
    def gather(pieces, o_ref):
        dst = 0
        for src, width in pieces:
            o_ref[:, dst:dst + width] = wt_ref[src:src + width, :].T.astype(o_ref.dtype)
            dst += width

    gather(pieces_a, wa_ref)
    gather(pieces_b, wb_ref)
    src, width = piece_vt
    wvt_ref[...] = wt_ref[src:src + width, :].astype(wvt_ref.dtype)


def _weight_prep(w_in_t, pieces_a, pieces_b, piece_vt):
    n, k = w_in_t.shape
    tc = min(COLS_WEIGHT_PREP, k)
    assert k % tc == 0 and all(s % 8 == 0 and wd % 16 == 0 for s, wd in pieces_a + pieces_b + (piece_vt,))
    na = sum(wd for _, wd in pieces_a)
    nb = sum(wd for _, wd in pieces_b)
    return pl.pallas_call(
        functools.partial(_weight_prep_kernel, pieces_a=pieces_a, pieces_b=pieces_b, piece_vt=piece_vt),
        grid=(k // tc,),
        in_specs=[pl.BlockSpec((n, tc), lambda i: (0, i))],
        out_specs=[pl.BlockSpec((tc, na), lambda i: (i, 0)), pl.BlockSpec((tc, nb), lambda i: (i, 0)),
                   pl.BlockSpec((piece_vt[1], tc), lambda i: (0, i))],
        out_shape=[jax.ShapeDtypeStruct((k, na), BF16), jax.ShapeDtypeStruct((k, nb), BF16),
                   jax.ShapeDtypeStruct((piece_vt[1], k), BF16)],
        compiler_params=_params("parallel"),
        name="weight_prep",
    )(w_in_t)


def _proj_t_kernel(wt_ref, x_ref, v_ref):
    vt = _dot(wt_ref[...], x_ref[...], _NT)
    ones = jnp.ones((FOX_V_ROWS - HEAD_DIM, vt.shape[1]), v_ref.dtype)
    for h in range(vt.shape[0] // HEAD_DIM):
        v_ref[h * FOX_V_ROWS:h * FOX_V_ROWS + HEAD_DIM, :] = vt[h * HEAD_DIM:(h + 1) * HEAD_DIM].astype(v_ref.dtype)
        v_ref[h * FOX_V_ROWS + HEAD_DIM:(h + 1) * FOX_V_ROWS, :] = ones


def _proj_t(wt, x):
    n, k = wt.shape
    m = x.shape[0]
    tm = min(ROWS_PROJ_T, m)
    assert m % tm == 0 and n % HEAD_DIM == 0
    n_aug = n // HEAD_DIM * FOX_V_ROWS
    return pl.pallas_call(
        _proj_t_kernel,
        grid=(m // tm,),
        in_specs=[pl.BlockSpec((n, k), lambda i: (0, 0)), pl.BlockSpec((tm, k), lambda i: (i, 0))],
        out_specs=pl.BlockSpec((n_aug, tm), lambda i: (0, i)),
        out_shape=jax.ShapeDtypeStruct((n_aug, m), BF16),
        compiler_params=_params("parallel"),
        name="in_proj_vt",
    )(wt, x)


def _cast_kernel(x_ref, o_ref):
    o_ref[...] = x_ref[...].astype(o_ref.dtype)


def _cast(x, dtype, name):
    m, n = x.shape
    tm = min(ROWS_OUT, m)
    assert m % tm == 0
    return pl.pallas_call(
        _cast_kernel,
        grid=(m // tm,),
        in_specs=[pl.BlockSpec((tm, n), lambda i: (i, 0))],
        out_specs=pl.BlockSpec((tm, n), lambda i: (i, 0)),
        out_shape=jax.ShapeDtypeStruct((m, n), dtype),
        compiler_params=_params("parallel"),
        name=name,
    )(x)


def _out_proj_kernel(*refs, n_y):
    ys = refs[:n_y]
    w_ref, x_ref, g_ref, o_ref, ss_ref = refs[n_y:]
    j = pl.program_id(1)
    tn = w_ref.shape[1]
    d = o_ref.shape[1]

    acc = None
    r0 = 0
    for y_ref in ys:
        ky = y_ref.shape[1]
        part = _dot(y_ref[...], w_ref[r0:r0 + ky, :])
        acc = part if acc is None else acc + part
        r0 += ky

    @pl.when(j == 0)
    def _():
        ss_ref[...] = jnp.zeros_like(ss_ref)

    ss_ref[...] += jnp.sum(acc * acc, axis=1, keepdims=True)
    o_ref[:, pl.ds(pl.multiple_of(j * tn, tn), tn)] = acc

    @pl.when(j == pl.num_programs(1) - 1)
    def _():
        scale = lax.rsqrt(ss_ref[...] * (1.0 / d) + RMS_EPS)
        for c0 in range(0, d, tn):
            sl = slice(c0, c0 + tn)
            o_ref[:, sl] = x_ref[:, sl] + o_ref[:, sl] * scale * g_ref[:, sl]


def _out_proj(ys, w_bf16, x2d, g):
    m, d = x2d.shape
    tm = min(ROWS_OUT, m)
    tn = min(WIDE, d)
    assert m % tm == 0 and d % tn == 0 and sum(y.shape[1] for y in ys) == w_bf16.shape[0]
    row = pl.BlockSpec((tm, d), lambda i, j: (i, 0))
    return pl.pallas_call(
        functools.partial(_out_proj_kernel, n_y=len(ys)),
        grid=(m // tm, d // tn),
        in_specs=[pl.BlockSpec((tm, y.shape[1]), lambda i, j: (i, 0)) for y in ys]
                 + [pl.BlockSpec((w_bf16.shape[0], tn), lambda i, j: (0, j)), row,
                    pl.BlockSpec((1, d), lambda i, j: (0, 0))],
        out_specs=row,
        out_shape=jax.ShapeDtypeStruct((m, d), x2d.dtype),
        scratch_shapes=[pltpu.VMEM((tm, 1), F32)],
        compiler_params=_params("parallel", "arbitrary"),
        name="out_proj_post",
    )(*ys, w_bf16, x2d, g.reshape(1, d))


def _rwkv_scan_kernel(*refs, n_units, L):
    r_ref, k_ref, v_ref, lora_ref = refs[:4]
    g_refs = refs[4:4 + n_units]
    (mu_r_ref, mu_k_ref, mu_v_ref, mu_lora_ref, w0_ref, wdu_ref, a0_ref, wiu_ref, kk_ref, ka_ref,
     rk_ref, lnw_ref, lnb_ref, o_ref, s_ref, carry_r, carry_k, carry_v, carry_lora) = refs[4 + n_units:]

    @pl.when(pl.program_id(2) == 0)
    def _():
        for z in (s_ref, carry_r, carry_k, carry_v, carry_lora):
            z[...] = jnp.zeros_like(z)

    rows = r_ref.shape[1]
    n_ch = rows // L
    first_row = lax.broadcasted_iota(jnp.int32, (rows, 1), 0) == 0

    def shift(x, carry_ref, mu):
        prev = jnp.where(first_row, carry_ref[...], pltpu.roll(x, 1, 0))
        carry_ref[...] = x[rows - 1:rows, :]
        return x + (prev - x) * mu

    r_all = shift(r_ref[0], carry_r, mu_r_ref[...])
    k_raw = shift(k_ref[0], carry_k, mu_k_ref[...])
    v_all = shift(v_ref[0], carry_v, mu_v_ref[...])
    lora = shift(lora_ref[0][:, :2 * LANES], carry_lora, mu_lora_ref[...])
    w_pre = -_softplus(-(w0_ref[...] + _dot_f32(jnp.tanh(lora[:, :LANES]), wdu_ref[...]))) - 0.5
    lw_all = -jnp.exp(w_pre)
    alpha = _sigmoid(a0_ref[...] + _dot_f32(lora[:, LANES:], wiu_ref[...]))
    k_all = k_raw * (1.0 + (alpha - 1.0) * ka_ref[...])
    kk_all = k_raw * kk_ref[...]
    hl = UNIT_HEADS * L
    n_sq = L.bit_length() - 2
    assert 2 ** (n_sq + 1) == L
    units = range(n_units)

    def blk(shape, d0, d1):
        return (lax.broadcasted_iota(jnp.int32, shape, 0) // d0) == (lax.broadcasted_iota(jnp.int32, shape, 1) // d1)

    t_row = lax.broadcasted_iota(jnp.int32, (hl, hl), 0)
    t_col = lax.broadcasted_iota(jnp.int32, (hl, hl), 1)
    same = blk((hl, hl), L, L)
    strict = same & (t_col < t_row)
    eye = jnp.where(t_row == t_col, 1.0, 0.0)
    u_row = lax.broadcasted_iota(jnp.int32, (L, hl), 0)
    u_col = lax.broadcasted_iota(jnp.int32, (L, hl), 1) % L
    strict_u = u_col < u_row
    incl_u = u_col <= u_row
    bd_rows = blk((hl, UNIT), L, HEAD_DIM)
    bd_state = blk((UNIT, UNIT), HEAD_DIM, HEAD_DIM)
    ones_bd = jnp.where(bd_state, 1.0, 0.0).astype(BF16)
    c_row = lax.broadcasted_iota(jnp.int32, (rows, rows), 0)
    c_col = lax.broadcasted_iota(jnp.int32, (rows, rows), 1)
    tril = jnp.where((c_row // L == c_col // L) & (c_row >= c_col), 1.0, 0.0).astype(BF16)

    def rep(x):
        return jnp.concatenate([x] * UNIT_HEADS, axis=0)

    def stack(x):
        return jnp.where(bd_rows, rep(x), 0.0)

    def unstack(xs):
        out = xs[0:L]
        for h in range(1, UNIT_HEADS):
            out = out + xs[h * L:(h + 1) * L]
        return out

    def headsum(x):
        return _dot(x.astype(BF16), ones_bd)

    def lanes(x, u):
        return x[:, u * UNIT:(u + 1) * UNIT]

    chains = [(ch, u) for ch in range(n_ch) for u in units]
    lw_hi, lw_lo = _split_bf16(lw_all, 2)
    cl_all = _dot(tril, lw_hi) + _dot(tril, lw_lo)
    kk_norm = [headsum(lanes(kk_all, u) * lanes(kk_all, u)) for u in units]
    kk_unit = [lanes(kk_all, u) * lax.rsqrt(jnp.maximum(kk_norm[u], 1e-24)) for u in units]

    ah, rh, e_last, lhs, rhs_b, rhs_k, upd_rhs, v = [], [], [], [], [], [], [], []
    for ch, u in chains:
        rs = slice(ch * L, (ch + 1) * L)
        sl = slice(u * UNIT, (u + 1) * UNIT)
        cl = cl_all[rs, sl]
        e_pos = jnp.exp(cl)
        e_neg = jnp.exp(-cl)
        e_rem = jnp.exp(cl[L - 1:L, :] - cl)
        kk = kk_unit[u][rs]
        a, b, k = -kk, kk * alpha[rs, sl], k_all[rs, sl]
        ah.append(a * jnp.exp(cl - lw_all[rs, sl]))
        rh.append(r_all[rs, sl] * e_pos)
        e_last.append(e_pos[L - 1:L, :])
        lhs.append(jnp.concatenate([ah[-1], rh[-1]], axis=0).astype(BF16))
        rhs_b.append(stack(b * e_neg).astype(BF16))
        rhs_k.append(stack(k * e_neg).astype(BF16))
        upd_rhs.append(jnp.concatenate([b * e_rem, k * e_rem], axis=0).astype(BF16))
        v.append(v_all[rs, sl])

    ids = range(len(chains))
    a_b = [_dot(lhs[i], rhs_b[i], _NT) for i in ids]
    a_k = [_dot(lhs[i], rhs_k[i], _NT) for i in ids]
    pw = [jnp.where(strict, rep(a_b[i][:L]), 0.0) for i in ids]
    a_ak = [jnp.where(strict_u, a_k[i][:L], 0.0).astype(BF16) for i in ids]
    a_rb = [jnp.where(incl_u, a_b[i][L:], 0.0).astype(BF16) for i in ids]
    a_rk = [jnp.where(incl_u, a_k[i][L:], 0.0).astype(BF16) for i in ids]

    t_inv = [eye + pw[i] for i in ids]
    for _ in range(n_sq):
        pw = [_dot(pw[i].astype(BF16), pw[i].astype(BF16)) for i in ids]
        t_inv = [t_inv[i] + _dot(t_inv[i].astype(BF16), pw[i].astype(BF16)) for i in ids]
    t_u = [unstack(t_inv[i]).astype(BF16) for i in ids]
    v_s = [stack(v[i]).astype(BF16) for i in ids]

    s_cur = [s_ref[u] for u in units]
    y = [None] * len(chains)
    for ch in range(n_ch):
        cid = [ch * n_units + u for u in units]
        xr = [_dot(lhs[i], s_cur[u].astype(BF16), _NT) for u, i in zip(units, cid)]
        x = [xr[u][:L] + _dot(a_ak[i], v_s[i]) for u, i in zip(units, cid)]
        uu = [_dot(t_u[i], stack(x[u]).astype(BF16)) for u, i in zip(units, cid)]
        for u, i in zip(units, cid):
            y[i] = xr[u][L:] + _dot(a_rb[i], stack(uu[u]).astype(BF16)) + _dot(a_rk[i], v_s[i])
            upd = _dot(jnp.concatenate([uu[u], v[i]], axis=0).astype(BF16), upd_rhs[i], _TN)
            s_cur[u] = jnp.where(bd_state, s_cur[u] * e_last[i] + upd, 0.0)
    for u in units:
        s_ref[u] = s_cur[u]

    inv_n = 1.0 / HEAD_DIM
    y = [jnp.concatenate([y[ch * n_units + u] for ch in range(n_ch)], axis=0) for u in units]
    d = [y[u] - headsum(y[u]) * inv_n for u in units]
    var = [headsum(d[u] * d[u]) * inv_n for u in units]
    bonus = [headsum(lanes(r_all, u) * lanes(k_all, u) * lanes(rk_ref[...], u)) * lanes(v_all, u) for u in units]
    for u in units:
        sl = slice(u * UNIT, (u + 1) * UNIT)
        yn = d[u] * lax.rsqrt(var[u] + GN_EPS) * lnw_ref[:, sl] + lnb_ref[:, sl]
        g = g_refs[u][0]
        o_ref[0, :, sl] = ((yn + bonus[u]) * (g * _sigmoid(g))).astype(o_ref.dtype)


def _rwkv_scan(pa3, off_r, off_lora, off_g, mu, w0, wdu, a0, wiu, k_k, k_a, r_k, ln_w, ln_b):
    b, t, _ = pa3.shape
    c = w0.shape[0]
    L = RWKV_CHUNK * min(RWKV_CHUNKS_PER_STEP, t // RWKV_CHUNK)
    n_units = min(RWKV_UNITS, c // UNIT)
    w = n_units * UNIT
    assert t % L == 0 and c % w == 0 and off_r % w == 0 and off_g % UNIT == 0 and off_lora % WIDE == 0
    rb, gb, lb, ng = off_r // w, off_g // UNIT, off_lora // WIDE, c // w

    def col(base):
        return pl.BlockSpec((1, L, w), lambda bi, gi, ci: (bi, ci, base + gi))

    def vec(base=0):
        return pl.BlockSpec((1, w), lambda bi, gi, ci: (0, base + gi))

    gates = [pl.BlockSpec((1, L, UNIT), lambda bi, gi, ci, u=u: (bi, ci, gb + gi * n_units + u))
             for u in range(n_units)]
    lora_w = pl.BlockSpec((wdu.shape[0], w), lambda bi, gi, ci: (0, gi))
    mu_rkv = mu[:3 * c].reshape(1, 3 * c)
    mu_lora = mu[3 * c:].reshape(1, 2 * LANES)
    row = lambda z: z.reshape(1, c)
    return pl.pallas_call(
        functools.partial(_rwkv_scan_kernel, n_units=n_units, L=RWKV_CHUNK),
        grid=(b, ng, t // L),
        in_specs=[col(rb), col(rb + ng), col(rb + 2 * ng),
                  pl.BlockSpec((1, L, WIDE), lambda bi, gi, ci: (bi, ci, lb))] + gates
                 + [vec(0), vec(ng), vec(2 * ng), pl.BlockSpec((1, 2 * LANES), lambda bi, gi, ci: (0, 0)),
                    vec(), lora_w, vec(), lora_w, vec(), vec(), vec(), vec(), vec()],
        out_specs=pl.BlockSpec((1, L, w), lambda bi, gi, ci: (bi, ci, gi)),
        out_shape=jax.ShapeDtypeStruct((b, t, c), BF16),
        scratch_shapes=[pltpu.VMEM((n_units, UNIT, UNIT), F32)] + [pltpu.VMEM((1, w), F32)] * 3
                       + [pltpu.VMEM((1, 2 * LANES), F32)],
        compiler_params=_params("parallel", "parallel", "arbitrary"),
        name="rwkv_scan",
    )(pa3, pa3, pa3, pa3, *([pa3] * n_units), mu_rkv, mu_rkv, mu_rkv, mu_lora,
      row(w0), wdu, row(a0), wiu, row(k_k), row(k_a), row(r_k), row(ln_w), row(ln_b))


def _fox_prefix_kernel(f_ref, bf_ref, k_ref, k0_ref, k1_ref, carry_ref):
    @pl.when(pl.program_id(1) == 0)
    def _():
        carry_ref[...] = jnp.zeros_like(carry_ref)

    x = f_ref[0][:, 2 * LANES:3 * LANES] + bf_ref[...]
    log_f = -_softplus(-x)
    tc = x.shape[0]
    tril = jnp.where(lax.broadcasted_iota(jnp.int32, (tc, tc), 0) >= lax.broadcasted_iota(jnp.int32, (tc, tc), 1),
                     1.0, 0.0).astype(BF16)
    c = carry_ref[...]
    for part in _split_bf16(log_f, 3):
        c = c + _dot(tril, part)
    carry_ref[...] = c[tc - 1:tc, :]

    fw = k_ref.shape[2]
    src = lax.broadcasted_iota(jnp.int32, (LANES, fw), 0)
    dst = lax.broadcasted_iota(jnp.int32, (LANES, fw), 1)
    lane = dst % LANES
    head = 2 * (dst // LANES) + jnp.where(lane < HEAD_DIM, 1, 0)
    bias = None
    for i, part in enumerate(_split_bf16(c * LOG2E, FOX_BIAS_PARTS)):
        sel = jnp.where((src == head) & (lane % HEAD_DIM == i), 1.0, 0.0).astype(BF16)
        d = _dot(part, sel)
        bias = d if bias is None else bias + d
    bias = bias.astype(BF16)
    k = k_ref[0]
    first = (lax.broadcasted_iota(jnp.int32, (tc, fw), 1) % LANES) < HEAD_DIM
    k0_ref[0] = jnp.where(first, k, bias)
    k1_ref[0] = jnp.where(first, bias, k)


def _fox_prefix(pt3, off_f, b_f, pb3, off_k, fw):
    b, t, _ = pt3.shape
    fh = b_f.shape[0]
    tc = min(ROWS_PREFIX, t)
    assert t % tc == 0 and fh <= LANES and fw == fh * HEAD_DIM and off_f % WIDE == 0 and off_k % fw == 0
    bf = jnp.zeros((1, LANES), F32).at[0, :fh].set(b_f)
    fb, kb = off_f // WIDE, off_k // fw
    out = pl.BlockSpec((1, tc, fw), lambda bi, i: (bi, i, 0))
    return pl.pallas_call(
        _fox_prefix_kernel,
        grid=(b, t // tc),
        in_specs=[pl.BlockSpec((1, tc, WIDE), lambda bi, i: (bi, i, fb)),
                  pl.BlockSpec((1, LANES), lambda bi, i: (0, 0)),
                  pl.BlockSpec((1, tc, fw), lambda bi, i: (bi, i, kb))],
        out_specs=[out, out],
        out_shape=[jax.ShapeDtypeStruct((b, t, fw), BF16)] * 2,
        scratch_shapes=[pltpu.VMEM((1, LANES), F32)],
        compiler_params=_params("parallel", "arbitrary"),
        name="fox_prefix",
    )(pt3, bf, pb3)


def _fox_attn_kernel(q_ref, k0_ref, k1_ref, v_ref, g_ref, o_ref, m_ref, acc_ref, *, tk):
    qi = pl.program_id(2)
    tq = q_ref.shape[1]
    n_pairs = q_ref.shape[2] // LANES
    n_sub = tq // tk
    q_lane = lax.broadcasted_iota(jnp.int32, (tq, LANES), 1)
    q_head0 = q_lane < HEAD_DIM
    bias0 = jnp.where((q_lane >= HEAD_DIM) & (q_lane < HEAD_DIM + FOX_BIAS_PARTS), -1.0, 0.0)
    bias1 = jnp.where(q_lane < FOX_BIAS_PARTS, -1.0, 0.0)
    k_refs = (k0_ref, k1_ref)
    chains = [(pr, h) for pr in range(n_pairs) for h in range(2)]

    qs = []
    for pr in range(n_pairs):
        q = q_ref[0, :, pr * LANES:(pr + 1) * LANES].astype(F32) * (HEAD_DIM ** -0.5 * LOG2E)
        qs += [jnp.where(q_head0, q, bias0).astype(BF16), jnp.where(q_head0, bias1, q).astype(BF16)]

    m_ref[...] = jnp.full_like(m_ref, -jnp.inf)
    acc_ref[...] = jnp.zeros_like(acc_ref)

    def tile(j, valid):
        start = pl.multiple_of(j * tk, tk)
        scores = []
        for ci, (pr, h) in enumerate(chains):
            s = _dot(k_refs[h][0, pl.ds(start, tk), pr * LANES:(pr + 1) * LANES], qs[ci], _NT)
            scores.append(s if valid is None else jnp.where(valid, s, -jnp.inf))
        for ci, s in enumerate(scores):
            m = m_ref[ci]
            m_new = jnp.maximum(m, jnp.max(s, axis=0, keepdims=True))
            m_ref[ci] = m_new
            p = jnp.exp2(s - m_new).astype(BF16)
            pv = _dot(v_ref[ci * FOX_V_ROWS:(ci + 1) * FOX_V_ROWS, pl.ds(start, tk)], p)
            acc_ref[ci] = jnp.exp2(m - m_new) * acc_ref[ci] + pv

    def body(j, carry):
        tile(j, None)
        return carry

    lax.fori_loop(0, qi * n_sub, body, 0)
    key = lax.broadcasted_iota(jnp.int32, (tk, tq), 0)
    qry = lax.broadcasted_iota(jnp.int32, (tk, tq), 1)
    for d in range(n_sub):
        tile(qi * n_sub + d, key + d * tk <= qry)

    for pr in range(n_pairs):
        sl = slice(pr * LANES, (pr + 1) * LANES)
        out_t = jnp.concatenate(
            [acc_ref[ci][:HEAD_DIM] * (1.0 / acc_ref[ci][HEAD_DIM:HEAD_DIM + 1]) for ci in (2 * pr, 2 * pr + 1)], axis=0)
        g = g_ref[0, :, sl]
        o_ref[0, :, sl] = (out_t.T * (g * _sigmoid(g))).astype(o_ref.dtype)


def _fox_attn(pb3, off_q, k0, k1, vt, pt3, off_g, fw):
    b, t, _ = pb3.shape
    tq = min(FOX_TQ, t)
    tk = min(FOX_TK, tq)
    n_pairs = min(FOX_PAIRS, fw // LANES)
    w = n_pairs * LANES
    v_rows = 2 * n_pairs * FOX_V_ROWS
    assert t % tq == 0 and tq % tk == 0 and fw % w == 0 and off_q % w == 0 and off_g % w == 0
    qb, gb = off_q // w, off_g // w
    keys = pl.BlockSpec((1, t, w), lambda bi, p, i: (bi, 0, p))
    return pl.pallas_call(
        functools.partial(_fox_attn_kernel, tk=tk),
        grid=(b, fw // w, t // tq),
        in_specs=[pl.BlockSpec((1, tq, w), lambda bi, p, i: (bi, i, qb + p)), keys, keys,
                  pl.BlockSpec((v_rows, t), lambda bi, p, i: (p, bi)),
                  pl.BlockSpec((1, tq, w), lambda bi, p, i: (bi, i, gb + p))],
        out_specs=pl.BlockSpec((1, tq, w), lambda bi, p, i: (bi, i, p)),
        out_shape=jax.ShapeDtypeStruct((b, t, fw), BF16),
        scratch_shapes=[pltpu.VMEM((2 * n_pairs, 1, tq), F32), pltpu.VMEM((2 * n_pairs, FOX_V_ROWS, tq), F32)],
        compiler_params=_params("parallel", "parallel", "arbitrary"),
        name="fox_attn",
    )(pb3, k0, k1, vt, pt3)


def _mem_attn_kernel(q_ref, kv_ref, g_ref, o_ref):
    mw = q_ref.shape[2]
    hd = mw // MEM_HEADS
    scale = hd ** -0.5
    outs = []
    for h in range(MEM_HEADS):
        q = q_ref[0, :, h * hd:(h + 1) * hd]
        mk = kv_ref[0, :, h * hd:(h + 1) * hd]
        mv = kv_ref[0, :, mw + h * hd:mw + (h + 1) * hd]
        s = _dot(q, mk, _NT) * scale
        p = jnp.exp(s - jnp.max(s, axis=1, keepdims=True))
        l = jnp.sum(p, axis=1, keepdims=True)
        outs.append(_dot(p.astype(BF16), mv) / l)
    g = g_ref[0]
    o_ref[0] = (jnp.concatenate(outs, axis=1) * (g * _sigmoid(g))).astype(o_ref.dtype)


def _mem_attn(pq3, off_q, mkv3, pt3, off_g, mw):
    b, t, _ = pq3.shape
    tm = min(ROWS_MEM, t)
    assert t % tm == 0 and off_q % mw == 0 and off_g % mw == 0 and (mw // MEM_HEADS) % LANES == 0
    n_mem = mkv3.shape[1]
    qb, gb = off_q // mw, off_g // mw
    return pl.pallas_call(
        _mem_attn_kernel,
        grid=(b, t // tm),
        in_specs=[pl.BlockSpec((1, tm, mw), lambda bi, i: (bi, i, qb)),
                  pl.BlockSpec((1, n_mem, 2 * mw), lambda bi, i: (bi, 0, 0)),
                  pl.BlockSpec((1, tm, mw), lambda bi, i: (bi, i, gb))],
        out_specs=pl.BlockSpec((1, tm, mw), lambda bi, i: (bi, i, 0)),
        out_shape=jax.ShapeDtypeStruct((b, t, mw), BF16),
        compiler_params=_params("parallel", "parallel"),
        name="mem_attn",
    )(pq3, mkv3, pt3)


def _layer(x, mem, g_pre, w_in, mu_rwkv, w0, w_decay_up, a0, w_iclr_up, k_k, k_a, r_k,
           ln_x_w, ln_x_b, b_f, g_mem, w_mem_kv, w_out, g_post):
    b, t, d = x.shape
    n_mem = mem.shape[1]
    c = w0.shape[0]
    fh = b_f.shape[0]
    fw = fh * HEAD_DIM
    mw = w_mem_kv.shape[1] // 2
    lora = w_decay_up.shape[0]
    assert lora == LANES and w_iclr_up.shape[0] == LANES and fh <= WIDE
    assert w_in.shape[1] == 4 * c + 2 * lora + 4 * fw + fh + 2 * mw
    assert c % WIDE == 0 and fw % WIDE == 0 and mw % WIDE == 0

    o_g = 3 * c + 2 * lora
    o_fq = o_g + c
    o_fv = o_fq + 2 * fw
    o_f = o_fv + fw
    o_gfox = o_f + fh
    o_mq = o_gfox + fw
    o_gmq = o_mq + mw

    x2 = x.reshape(b * t, d)
    h = _rmsnorm(x2, g_pre, BF16)

    lora_f = WIDE - 2 * lora
    assert o_f % LANES == 0 and o_f + lora_f <= w_in.shape[1]
    pieces_a = ((o_g, c), (0, 3 * c), (o_gmq, mw), (o_gfox, fw), (3 * c, 2 * lora), (o_f, lora_f))
    pieces_b = ((o_fq, 2 * fw), (o_mq, mw))
    w_a, w_b, w_vt = _weight_prep(w_in.T, pieces_a, pieces_b, (o_fv, fw))
    off_a = {"g_rwkv": 0, "r": c, "g_mq": 4 * c, "g_fox": 4 * c + mw, "lora": 4 * c + mw + fw}
    off_b = {"fq": 0, "fk": fw, "mq": 2 * fw}

    pa3 = _matmul(h, w_a, F32, 2 * WIDE, "in_proj_a").reshape(b, t, -1)
    pb3 = _matmul(h, w_b, BF16, 2 * WIDE, "in_proj_b").reshape(b, t, -1)
    vt = _proj_t(w_vt, h)

    y_rwkv = _rwkv_scan(pa3, off_a["r"], off_a["lora"], off_a["g_rwkv"], mu_rwkv, w0, w_decay_up, a0, w_iclr_up,
                        k_k, k_a, r_k.reshape(-1), ln_x_w, ln_x_b)

    k0, k1 = _fox_prefix(pa3, off_a["lora"], b_f, pb3, off_b["fk"], fw)
    y_fox = _fox_attn(pb3, off_b["fq"], k0, k1, vt, pa3, off_a["g_fox"], fw)

    hm = _rmsnorm(mem.reshape(b * n_mem, d), g_mem, BF16)
    mkv3 = _proj([hm], [(w_mem_kv, (d, WIDE), lambda j: (0, j), (0, 0))], 2 * mw, WIDE, BF16,
                 "mem_kv_proj").reshape(b, n_mem, 2 * mw)
    y_mem = _mem_attn(pb3, off_b["mq"], mkv3, pa3, off_a["g_mq"], mw)

    ys = [y_rwkv.reshape(b * t, c), y_fox.reshape(b * t, fw), y_mem.reshape(b * t, mw)]
    return _out_proj(ys, _cast(w_out, BF16, "w_out_cast"), x2, g_post).reshape(b, t, d)


def kernel(x, mem, g_pre, w_in, mu_rwkv, w0, w_decay_up, a0, w_iclr_up, k_k, k_a, r_k, ln_x_w, ln_x_b, b_f,
           g_mem, w_mem_kv, w_out, g_post):
    for l in range(g_pre.shape[0]):
        x = _layer(x, mem, g_pre[l], w_in[l], mu_rwkv[l], w0[l], w_decay_up[l], a0[l], w_iclr_up[l], k_k[l],
                   k_a[l], r_k[l], ln_x_w[l], ln_x_b[l], b_f[l], g_mem[l], w_mem_kv[l], w_out[l], g_post[l])
    return x
```

```python
import functools

import jax
import jax.numpy as jnp
from jax import lax
from jax.experimental import pallas as pl
from jax.experimental.pallas import tpu as pltpu

HEAD_DIM = 64
MEM_HEADS = 4
RMS_EPS = 1e-6
GN_EPS = 64e-5
LOG2E = 1.4426950408889634

LANES = 128
MXU = 256
UNIT_HEADS = MXU // HEAD_DIM
UNIT = UNIT_HEADS * HEAD_DIM
RWKV_CHUNK = 64
WIDE = 512
VMEM_LIMIT_BYTES = 56 * 1024 * 1024
FOX_BIAS_PARTS = 3
FOX_V_ROWS = HEAD_DIM + 16

ROWS_ELEMENTWISE = 256
ROWS_PROJ = 1024
ROWS_PROJ_T = 512
ROWS_OUT = 512
COLS_WEIGHT_PREP = 128
RWKV_CHUNKS_PER_STEP = 2
ROWS_PREFIX = 512
ROWS_MEM = 512
FOX_TQ = 512
FOX_TK = 256
FOX_PAIRS = 4
RWKV_UNITS = 6

F32 = jnp.float32
BF16 = jnp.bfloat16

_NN = (((1,), (0,)), ((), ()))
_NT = (((1,), (1,)), ((), ()))
_TN = (((0,), (0,)), ((), ()))


def _dot(a, b, dims=_NN):
    return lax.dot_general(a, b, dims, preferred_element_type=F32)


def _split_bf16(x, n):
    parts = []
    rem = x
    for _ in range(n):
        p = rem.astype(BF16)
        parts.append(p)
        rem = rem - p.astype(F32)
    return parts


def _dot_split(a, b_bf16, n, dims=_NN):
    acc = None
    for p in _split_bf16(a, n):
        d = _dot(p, b_bf16, dims)
        acc = d if acc is None else acc + d
    return acc


def _dot_f32(a, b):
    a_hi, a_lo = _split_bf16(a, 2)
    b_hi, b_lo = _split_bf16(b, 2)
    return _dot(a_hi, b_hi) + (_dot(a_lo, b_hi) + _dot(a_hi, b_lo))


def _sigmoid(x):
    return 1.0 / (1.0 + jnp.exp(-x))


def _softplus(x):
    return jnp.maximum(x, 0.0) + jnp.log(1.0 + jnp.exp(-jnp.abs(x)))


def _params(*semantics):
    return pltpu.CompilerParams(dimension_semantics=semantics, vmem_limit_bytes=VMEM_LIMIT_BYTES)


def _rmsnorm_kernel(x_ref, g_ref, o_ref):
    x = x_ref[...]
    ms = jnp.mean(x * x, axis=-1, keepdims=True)
    o_ref[...] = (x * lax.rsqrt(ms + RMS_EPS) * g_ref[...]).astype(o_ref.dtype)


def _rmsnorm(x2d, g, out_dtype):
    m, d = x2d.shape
    tm = min(ROWS_ELEMENTWISE, m)
    assert m % tm == 0
    return pl.pallas_call(
        _rmsnorm_kernel,
        grid=(m // tm,),
        in_specs=[pl.BlockSpec((tm, d), lambda i: (i, 0)), pl.BlockSpec((1, d), lambda i: (0, 0))],
        out_specs=pl.BlockSpec((tm, d), lambda i: (i, 0)),
        out_shape=jax.ShapeDtypeStruct((m, d), out_dtype),
        compiler_params=_params("parallel"),
        name="rmsnorm",
    )(x2d, g.reshape(1, d))


def _proj_kernel(*refs, n_x, placement):
    xs = refs[:n_x]
    ws = refs[n_x:n_x + len(placement)]
    o_ref, wb_ref = refs[n_x + len(placement):]

    @pl.when(pl.program_id(1) == 0)
    def _():
        for w_ref, (r0, c0) in zip(ws, placement):
            wb_ref[r0:r0 + w_ref.shape[0], c0:c0 + w_ref.shape[1]] = w_ref[...].astype(BF16)

    acc = None
    r0 = 0
    for x_ref in xs:
        kx = x_ref.shape[1]
        d = _dot(x_ref[...], wb_ref[r0:r0 + kx, :])
        acc = d if acc is None else acc + d
        r0 += kx
    o_ref[...] = acc.astype(o_ref.dtype)


def _proj(xs, weights, n_cols, tn, out_dtype, name):
    m = xs[0].shape[0]
    k_total = sum(x.shape[1] for x in xs)
    tm = min(ROWS_PROJ, m)
    tn = min(tn, n_cols)
    assert m % tm == 0 and n_cols % tn == 0
    in_specs = [pl.BlockSpec((tm, x.shape[1]), lambda j, i: (i, 0)) for x in xs]
    for _, shape, index_fn, _ in weights:
        in_specs.append(pl.BlockSpec(shape, lambda j, i, f=index_fn: f(j)))
    return pl.pallas_call(
        functools.partial(_proj_kernel, n_x=len(xs), placement=tuple(p for _, _, _, p in weights)),
        grid=(n_cols // tn, m // tm),
        in_specs=in_specs,
        out_specs=pl.BlockSpec((tm, tn), lambda j, i: (i, j)),
        out_shape=jax.ShapeDtypeStruct((m, n_cols), out_dtype),
        scratch_shapes=[pltpu.VMEM((k_total, tn), BF16)],
        compiler_params=_params("parallel", "arbitrary"),
        name=name,
    )(*xs, *[w for w, _, _, _ in weights])


def _matmul_kernel(x_ref, w_ref, o_ref):
    o_ref[...] = _dot(x_ref[...], w_ref[...]).astype(o_ref.dtype)


def _matmul(x, w, out_dtype, tn, name):
    m, k = x.shape
    n = w.shape[1]
    tm = min(ROWS_PROJ, m)
    tn = min(tn, n)
    assert m % tm == 0 and n % tn == 0
    return pl.pallas_call(
        _matmul_kernel,
        grid=(n // tn, m // tm),
        in_specs=[pl.BlockSpec((tm, k), lambda j, i: (i, 0)), pl.BlockSpec((k, tn), lambda j, i: (0, j))],
        out_specs=pl.BlockSpec((tm, tn), lambda j, i: (i, j)),
        out_shape=jax.ShapeDtypeStruct((m, n), out_dtype),
        compiler_params=_params("parallel", "parallel"),
        name=name,
    )(x, w)


def _weight_prep_kernel(wt_ref, wa_ref, wb_ref, wvt_ref, *, pieces_a, pieces_b, piece_vt):
    def gather(pieces, o_ref):
        dst = 0
        for src, width in pieces:
            o_ref[:, dst:dst + width] = wt_ref[src:src + width, :].T.astype(o_ref.dtype)
            dst += width

    gather(pieces_a, wa_ref)
    gather(pieces_b, wb_ref)
    src, width = piece_vt
    wvt_ref[...] = wt_ref[src:src + width, :].astype(wvt_ref.dtype)


def _weight_prep(w_in_t, pieces_a, pieces_b, piece_vt):
    n, k = w_in_t.shape
    tc = min(COLS_WEIGHT_PREP, k)
    assert k % tc == 0 and all(s % 8 == 0 and wd % 16 == 0 for s, wd in pieces_a + pieces_b + (piece_vt,))
    na = sum(wd for _, wd in pieces_a)
    nb = sum(wd for _, wd in pieces_b)
    return pl.pallas_call(
        functools.partial(_weight_prep_kernel, pieces_a=pieces_a, pieces_b=pieces_b, piece_vt=piece_vt),
        grid=(k // tc,),
        in_specs=[pl.BlockSpec((n, tc), lambda i: (0, i))],
        out_specs=[pl.BlockSpec((tc, na), lambda i: (i, 0)), pl.BlockSpec((tc, nb), lambda i: (i, 0)),
                   pl.BlockSpec((piece_vt[1], tc), lambda i: (0, i))],
        out_shape=[jax.ShapeDtypeStruct((k, na), BF16), jax.ShapeDtypeStruct((k, nb), BF16),
                   jax.ShapeDtypeStruct((piece_vt[1], k), BF16)],
        compiler_params=_params("parallel"),
        name="weight_prep",
    )(w_in_t)


def _proj_t_kernel(wt_ref, x_ref, v_ref):
    vt = _dot(wt_ref[...], x_ref[...], _NT)
    ones = jnp.ones((FOX_V_ROWS - HEAD_DIM, vt.shape[1]), v_ref.dtype)
    for h in range(vt.shape[0] // HEAD_DIM):
        v_ref[h * FOX_V_ROWS:h * FOX_V_ROWS + HEAD_DIM, :] = vt[h * HEAD_DIM:(h + 1) * HEAD_DIM].astype(v_ref.dtype)
        v_ref[h * FOX_V_ROWS + HEAD_DIM:(h + 1) * FOX_V_ROWS, :] = ones


def _proj_t(wt, x):
    n, k = wt.shape
    m = x.shape[0]
    tm = min(ROWS_PROJ_T, m)
    assert m % tm == 0 and n % HEAD_DIM == 0
    n_aug = n // HEAD_DIM * FOX_V_ROWS
    return pl.pallas_call(
        _proj_t_kernel,
        grid=(m // tm,),
        in_specs=[pl.BlockSpec((n, k), lambda i: (0, 0)), pl.BlockSpec((tm, k), lambda i: (i, 0))],
        out_specs=pl.BlockSpec((n_aug, tm), lambda i: (0, i)),
        out_shape=jax.ShapeDtypeStruct((n_aug, m), BF16),
        compiler_params=_params("parallel"),
        name="in_proj_vt",
    )(wt, x)


def _cast_kernel(x_ref, o_ref):
    o_ref[...] = x_ref[...].astype(o_ref.dtype)


def _cast(x, dtype, name):
    m, n = x.shape
    tm = min(ROWS_OUT, m)
    assert m % tm == 0
    return pl.pallas_call(
        _cast_kernel,
        grid=(m // tm,),
        in_specs=[pl.BlockSpec((tm, n), lambda i: (i, 0))],
        out_specs=pl.BlockSpec((tm, n), lambda i: (i, 0)),
        out_shape=jax.ShapeDtypeStruct((m, n), dtype),
        compiler_params=_params("parallel"),
        name=name,
    )(x)


def _out_proj_kernel(*refs, n_y):
    ys = refs[:n_y]
    w_ref, x_ref, g_ref, o_ref, ss_ref = refs[n_y:]
    j = pl.program_id(1)
    tn = w_ref.shape[1]
    d = o_ref.shape[1]

    acc = None
    r0 = 0
    for y_ref in ys:
        ky = y_ref.shape[1]
        part = _dot(y_ref[...], w_ref[r0:r0 + ky, :])
        acc = part if acc is None else acc + part
        r0 += ky

    @pl.when(j == 0)
    def _():
        ss_ref[...] = jnp.zeros_like(ss_ref)

    ss_ref[...] += jnp.sum(acc * acc, axis=1, keepdims=True)
    o_ref[:, pl.ds(pl.multiple_of(j * tn, tn), tn)] = acc

    @pl.when(j == pl.num_programs(1) - 1)
    def _():
        scale = lax.rsqrt(ss_ref[...] * (1.0 / d) + RMS_EPS)
        for c0 in range(0, d, tn):
            sl = slice(c0, c0 + tn)
            o_ref[:, sl] = x_ref[:, sl] + o_ref[:, sl] * scale * g_ref[:, sl]


def _out_proj(ys, w_bf16, x2d, g):
    m, d = x2d.shape
    tm = min(ROWS_OUT, m)
    tn = min(WIDE, d)
    assert m % tm == 0 and d % tn == 0 and sum(y.shape[1] for y in ys) == w_bf16.shape[0]
    row = pl.BlockSpec((tm, d), lambda i, j: (i, 0))
    return pl.pallas_call(
        functools.partial(_out_proj_kernel, n_y=len(ys)),
        grid=(m // tm, d // tn),
        in_specs=[pl.BlockSpec((tm, y.shape[1]), lambda i, j: (i, 0)) for y in ys]
                 + [pl.BlockSpec((w_bf16.shape[0], tn), lambda i, j: (0, j)), row,
                    pl.BlockSpec((1, d), lambda i, j: (0, 0))],
        out_specs=row,
        out_shape=jax.ShapeDtypeStruct((m, d), x2d.dtype),
        scratch_shapes=[pltpu.VMEM((tm, 1), F32)],
        compiler_params=_params("parallel", "arbitrary"),
        name="out_proj_post",
    )(*ys, w_bf16, x2d, g.reshape(1, d))


def _rwkv_scan_kernel(*refs, n_units, L):
    r_ref, k_ref, v_ref, lora_ref = refs[:4]
    g_refs = refs[4:4 + n_units]
    (mu_r_ref, mu_k_ref, mu_v_ref, mu_lora_ref, w0_ref, wdu_ref, a0_ref, wiu_ref, kk_ref, ka_ref,
     rk_ref, lnw_ref, lnb_ref, o_ref, s_ref, carry_r, carry_k, carry_v, carry_lora) = refs[4 + n_units:]

    @pl.when(pl.program_id(2) == 0)
    def _():
        for z in (s_ref, carry_r, carry_k, carry_v, carry_lora):
            z[...] = jnp.zeros_like(z)

    rows = r_ref.shape[1]
    n_ch = rows // L
    first_row = lax.broadcasted_iota(jnp.int32, (rows, 1), 0) == 0

    def shift(x, carry_ref, mu):
        prev = jnp.where(first_row, carry_ref[...], pltpu.roll(x, 1, 0))
        carry_ref[...] = x[rows - 1:rows, :]
        return x + (prev - x) * mu

    r_all = shift(r_ref[0], carry_r, mu_r_ref[...])
    k_raw = shift(k_ref[0], carry_k, mu_k_ref[...])
    v_all = shift(v_ref[0], carry_v, mu_v_ref[...])
    lora = shift(lora_ref[0][:, :2 * LANES], carry_lora, mu_lora_ref[...])
    w_pre = -_softplus(-(w0_ref[...] + _dot_f32(jnp.tanh(lora[:, :LANES]), wdu_ref[...]))) - 0.5
    lw_all = -jnp.exp(w_pre)
    alpha = _sigmoid(a0_ref[...] + _dot_f32(lora[:, LANES:], wiu_ref[...]))
    k_all = k_raw * (1.0 + (alpha - 1.0) * ka_ref[...])
    kk_all = k_raw * kk_ref[...]
    hl = UNIT_HEADS * L
    n_sq = L.bit_length() - 2
    assert 2 ** (n_sq + 1) == L
    units = range(n_units)

    def blk(shape, d0, d1):
        return (lax.broadcasted_iota(jnp.int32, shape, 0) // d0) == (lax.broadcasted_iota(jnp.int32, shape, 1) // d1)

    same = blk((hl, hl), L, L)
    u_row = lax.broadcasted_iota(jnp.int32, (L, hl), 0)
    u_col = lax.broadcasted_iota(jnp.int32, (L, hl), 1) % L
    strict_u = u_col < u_row
    incl_u = u_col <= u_row
    eye_u = jnp.where(u_col == u_row, 1.0, 0.0)
    bd_rows = blk((hl, UNIT), L, HEAD_DIM)
    bd_state = blk((UNIT, UNIT), HEAD_DIM, HEAD_DIM)
    ones_bd = jnp.where(bd_state, 1.0, 0.0).astype(BF16)
    c_row = lax.broadcasted_iota(jnp.int32, (rows, rows), 0)
    c_col = lax.broadcasted_iota(jnp.int32, (rows, rows), 1)
    tril = jnp.where((c_row // L == c_col // L) & (c_row >= c_col), 1.0, 0.0).astype(BF16)

    def rep(x):
        return jnp.concatenate([x] * UNIT_HEADS, axis=0)

    def stack(x):
        return jnp.where(bd_rows, rep(x), 0.0)

    def headsum(x):
        return _dot(x.astype(BF16), ones_bd)

    def lanes(x, u):
        return x[:, u * UNIT:(u + 1) * UNIT]

    chains = [(ch, u) for ch in range(n_ch) for u in units]
    lw_hi, lw_lo = _split_bf16(lw_all, 2)
    cl_all = _dot(tril, lw_hi) + _dot(tril, lw_lo)
    kk_norm = [headsum(lanes(kk_all, u) * lanes(kk_all, u)) for u in units]
    kk_unit = [lanes(kk_all, u) * lax.rsqrt(jnp.maximum(kk_norm[u], 1e-24)) for u in units]

    ah, rh, e_last, lhs, rhs_b, rhs_k, upd_rhs, v = [], [], [], [], [], [], [], []
    for ch, u in chains:
        rs = slice(ch * L, (ch + 1) * L)
        sl = slice(u * UNIT, (u + 1) * UNIT)
        cl = cl_all[rs, sl]
        e_pos = jnp.exp(cl)
        e_neg = jnp.exp(-cl)
        e_rem = jnp.exp(cl[L - 1:L, :] - cl)
        kk = kk_unit[u][rs]
        a, b, k = -kk, kk * alpha[rs, sl], k_all[rs, sl]
        ah.append(a * jnp.exp(cl - lw_all[rs, sl]))
        rh.append(r_all[rs, sl] * e_pos)
        e_last.append(e_pos[L - 1:L, :])
        lhs.append(jnp.concatenate([ah[-1], rh[-1]], axis=0).astype(BF16))
        rhs_b.append(stack(b * e_neg).astype(BF16))
        rhs_k.append(stack(k * e_neg).astype(BF16))
        upd_rhs.append(jnp.concatenate([b * e_rem, k * e_rem], axis=0).astype(BF16))
        v.append(v_all[rs, sl])

    ids = range(len(chains))
    a_b = [_dot(lhs[i], rhs_b[i], _NT) for i in ids]
    a_k = [_dot(lhs[i], rhs_k[i], _NT) for i in ids]
    pw = [jnp.where(strict_u, a_b[i][:L], 0.0) for i in ids]
    a_ak = [jnp.where(strict_u, a_k[i][:L], 0.0).astype(BF16) for i in ids]
    a_rb = [jnp.where(incl_u, a_b[i][L:], 0.0).astype(BF16) for i in ids]
    a_rk = [jnp.where(incl_u, a_k[i][L:], 0.0).astype(BF16) for i in ids]

    def blockdiag(m_u):
        return jnp.where(same, rep(m_u), 0.0).astype(BF16)

    t_u = [eye_u + pw[i] for i in ids]
    pw_bd = [blockdiag(pw[i]) for i in ids]
    for _ in range(n_sq):
        pw = [_dot(pw[i].astype(BF16), pw_bd[i]) for i in ids]
        pw_bd = [blockdiag(pw[i]) for i in ids]
        t_u = [t_u[i] + _dot(t_u[i].astype(BF16), pw_bd[i]) for i in ids]
    t_u = [t_u[i].astype(BF16) for i in ids]
    v_s = [stack(v[i]).astype(BF16) for i in ids]

    s_cur = [s_ref[u] for u in units]
    y = [None] * len(chains)
    for ch in range(n_ch):
        cid = [ch * n_units + u for u in units]
        xr = [_dot(lhs[i], s_cur[u].astype(BF16), _NT) for u, i in zip(units, cid)]
        x = [xr[u][:L] + _dot(a_ak[i], v_s[i]) for u, i in zip(units, cid)]
        uu = [_dot(t_u[i], stack(x[u]).astype(BF16)) for u, i in zip(units, cid)]
        for u, i in zip(units, cid):
            y[i] = xr[u][L:] + _dot(a_rb[i], stack(uu[u]).astype(BF16)) + _dot(a_rk[i], v_s[i])
            upd = _dot(jnp.concatenate([uu[u], v[i]], axis=0).astype(BF16), upd_rhs[i], _TN)
            s_cur[u] = jnp.where(bd_state, s_cur[u] * e_last[i] + upd, 0.0)
    for u in units:
        s_ref[u] = s_cur[u]

    inv_n = 1.0 / HEAD_DIM
    y = [jnp.concatenate([y[ch * n_units + u] for ch in range(n_ch)], axis=0) for u in units]
    d = [y[u] - headsum(y[u]) * inv_n for u in units]
    var = [headsum(d[u] * d[u]) * inv_n for u in units]
    bonus = [headsum(lanes(r_all, u) * lanes(k_all, u) * lanes(rk_ref[...], u)) * lanes(v_all, u) for u in units]
    for u in units:
        sl = slice(u * UNIT, (u + 1) * UNIT)
        yn = d[u] * lax.rsqrt(var[u] + GN_EPS) * lnw_ref[:, sl] + lnb_ref[:, sl]
        g = g_refs[u][0]
        o_ref[0, :, sl] = ((yn + bonus[u]) * (g * _sigmoid(g))).astype(o_ref.dtype)


def _rwkv_scan(pa3, off_r, off_lora, off_g, mu, w0, wdu, a0, wiu, k_k, k_a, r_k, ln_w, ln_b):
    b, t, _ = pa3.shape
    c = w0.shape[0]
    L = RWKV_CHUNK * min(RWKV_CHUNKS_PER_STEP, t // RWKV_CHUNK)
    n_units = min(RWKV_UNITS, c // UNIT)
    w = n_units * UNIT
    assert t % L == 0 and c % w == 0 and off_r % w == 0 and off_g % UNIT == 0 and off_lora % WIDE == 0
    rb, gb, lb, ng = off_r // w, off_g // UNIT, off_lora // WIDE, c // w

    def col(base):
        return pl.BlockSpec((1, L, w), lambda bi, gi, ci: (bi, ci, base + gi))

    def vec(base=0):
        return pl.BlockSpec((1, w), lambda bi, gi, ci: (0, base + gi))

    gates = [pl.BlockSpec((1, L, UNIT), lambda bi, gi, ci, u=u: (bi, ci, gb + gi * n_units + u))
             for u in range(n_units)]
    lora_w = pl.BlockSpec((wdu.shape[0], w), lambda bi, gi, ci: (0, gi))
    mu_rkv = mu[:3 * c].reshape(1, 3 * c)
    mu_lora = mu[3 * c:].reshape(1, 2 * LANES)
    row = lambda z: z.reshape(1, c)
    return pl.pallas_call(
        functools.partial(_rwkv_scan_kernel, n_units=n_units, L=RWKV_CHUNK),
        grid=(b, ng, t // L),
        in_specs=[col(rb), col(rb + ng), col(rb + 2 * ng),
                  pl.BlockSpec((1, L, WIDE), lambda bi, gi, ci: (bi, ci, lb))] + gates
                 + [vec(0), vec(ng), vec(2 * ng), pl.BlockSpec((1, 2 * LANES), lambda bi, gi, ci: (0, 0)),
                    vec(), lora_w, vec(), lora_w, vec(), vec(), vec(), vec(), vec()],
        out_specs=pl.BlockSpec((1, L, w), lambda bi, gi, ci: (bi, ci, gi)),
        out_shape=jax.ShapeDtypeStruct((b, t, c), BF16),
        scratch_shapes=[pltpu.VMEM((n_units, UNIT, UNIT), F32)] + [pltpu.VMEM((1, w), F32)] * 3
                       + [pltpu.VMEM((1, 2 * LANES), F32)],
        compiler_params=_params("parallel", "parallel", "arbitrary"),
        name="rwkv_scan",
    )(pa3, pa3, pa3, pa3, *([pa3] * n_units), mu_rkv, mu_rkv, mu_rkv, mu_lora,
      row(w0), wdu, row(a0), wiu, row(k_k), row(k_a), row(r_k), row(ln_w), row(ln_b))


def _fox_prefix_kernel(f_ref, bf_ref, k_ref, k0_ref, k1_ref, carry_ref):
    @pl.when(pl.program_id(1) == 0)
    def _():
        carry_ref[...] = jnp.zeros_like(carry_ref)

    x = f_ref[0][:, 2 * LANES:3 * LANES] + bf_ref[...]
    log_f = -_softplus(-x)
    tc = x.shape[0]
    tril = jnp.where(lax.broadcasted_iota(jnp.int32, (tc, tc), 0) >= lax.broadcasted_iota(jnp.int32, (tc, tc), 1),
                     1.0, 0.0).astype(BF16)
    c = carry_ref[...]
    for part in _split_bf16(log_f, 3):
        c = c + _dot(tril, part)
    carry_ref[...] = c[tc - 1:tc, :]

    fw = k_ref.shape[2]
    src = lax.broadcasted_iota(jnp.int32, (LANES, fw), 0)
    dst = lax.broadcasted_iota(jnp.int32, (LANES, fw), 1)
    lane = dst % LANES
    head = 2 * (dst // LANES) + jnp.where(lane < HEAD_DIM, 1, 0)
    bias = None
    for i, part in enumerate(_split_bf16(c * LOG2E, FOX_BIAS_PARTS)):
        sel = jnp.where((src == head) & (lane % HEAD_DIM == i), 1.0, 0.0).astype(BF16)
        d = _dot(part, sel)
        bias = d if bias is None else bias + d
    bias = bias.astype(BF16)
    k = k_ref[0]
    first = (lax.broadcasted_iota(jnp.int32, (tc, fw), 1) % LANES) < HEAD_DIM
    k0_ref[0] = jnp.where(first, k, bias)
    k1_ref[0] = jnp.where(first, bias, k)


def _fox_prefix(pt3, off_f, b_f, pb3, off_k, fw):
    b, t, _ = pt3.shape
    fh = b_f.shape[0]
    tc = min(ROWS_PREFIX, t)
    assert t % tc == 0 and fh <= LANES and fw == fh * HEAD_DIM and off_f % WIDE == 0 and off_k % fw == 0
    bf = jnp.zeros((1, LANES), F32).at[0, :fh].set(b_f)
    fb, kb = off_f // WIDE, off_k // fw
    out = pl.BlockSpec((1, tc, fw), lambda bi, i: (bi, i, 0))
    return pl.pallas_call(
        _fox_prefix_kernel,
        grid=(b, t // tc),
        in_specs=[pl.BlockSpec((1, tc, WIDE), lambda bi, i: (bi, i, fb)),
                  pl.BlockSpec((1, LANES), lambda bi, i: (0, 0)),
                  pl.BlockSpec((1, tc, fw), lambda bi, i: (bi, i, kb))],
        out_specs=[out, out],
        out_shape=[jax.ShapeDtypeStruct((b, t, fw), BF16)] * 2,
        scratch_shapes=[pltpu.VMEM((1, LANES), F32)],
        compiler_params=_params("parallel", "arbitrary"),
        name="fox_prefix",
    )(pt3, bf, pb3)


def _fox_attn_kernel(q_ref, k0_ref, k1_ref, v_ref, g_ref, o_ref, m_ref, acc_ref, *, tk):
    qi = pl.program_id(2)
    tq = q_ref.shape[1]
    n_pairs = q_ref.shape[2] // LANES
    n_sub = tq // tk
    q_lane = lax.broadcasted_iota(jnp.int32, (tq, LANES), 1)
    q_head0 = q_lane < HEAD_DIM
    bias0 = jnp.where((q_lane >= HEAD_DIM) & (q_lane < HEAD_DIM + FOX_BIAS_PARTS), -1.0, 0.0)
    bias1 = jnp.where(q_lane < FOX_BIAS_PARTS, -1.0, 0.0)
    k_refs = (k0_ref, k1_ref)
    chains = [(pr, h) for pr in range(n_pairs) for h in range(2)]

    qs = []
    for pr in range(n_pairs):
        q = q_ref[0, :, pr * LANES:(pr + 1) * LANES].astype(F32) * (HEAD_DIM ** -0.5 * LOG2E)
        qs += [jnp.where(q_head0, q, bias0).astype(BF16), jnp.where(q_head0, bias1, q).astype(BF16)]

    m_ref[...] = jnp.full_like(m_ref, -jnp.inf)
    acc_ref[...] = jnp.zeros_like(acc_ref)

    def tile(j, valid):
        start = pl.multiple_of(j * tk, tk)
        scores = []
        for ci, (pr, h) in enumerate(chains):
            s = _dot(k_refs[h][0, pl.ds(start, tk), pr * LANES:(pr + 1) * LANES], qs[ci], _NT)
            scores.append(s if valid is None else jnp.where(valid, s, -jnp.inf))
        for ci, s in enumerate(scores):
            m = m_ref[ci]
            m_new = jnp.maximum(m, jnp.max(s, axis=0, keepdims=True))
            m_ref[ci] = m_new
            p = jnp.exp2(s - m_new).astype(BF16)
            pv = _dot(v_ref[ci * FOX_V_ROWS:(ci + 1) * FOX_V_ROWS, pl.ds(start, tk)], p)
            acc_ref[ci] = jnp.exp2(m - m_new) * acc_ref[ci] + pv

    def body(j, carry):
        tile(j, None)
        return carry

    lax.fori_loop(0, qi * n_sub, body, 0)
    key = lax.broadcasted_iota(jnp.int32, (tk, tq), 0)
    qry = lax.broadcasted_iota(jnp.int32, (tk, tq), 1)
    for d in range(n_sub):
        tile(qi * n_sub + d, key + d * tk <= qry)

    for pr in range(n_pairs):
        sl = slice(pr * LANES, (pr + 1) * LANES)
        out_t = jnp.concatenate(
            [acc_ref[ci][:HEAD_DIM] * (1.0 / acc_ref[ci][HEAD_DIM:HEAD_DIM + 1]) for ci in (2 * pr, 2 * pr + 1)], axis=0)
        g = g_ref[0, :, sl]
        o_ref[0, :, sl] = (out_t.T * (g * _sigmoid(g))).astype(o_ref.dtype)


def _fox_attn(pb3, off_q, k0, k1, vt, pt3, off_g, fw):
    b, t, _ = pb3.shape
    tq = min(FOX_TQ, t)
    tk = min(FOX_TK, tq)
    n_pairs = min(FOX_PAIRS, fw // LANES)
    w = n_pairs * LANES
    v_rows = 2 * n_pairs * FOX_V_ROWS
    assert t % tq == 0 and tq % tk == 0 and fw % w == 0 and off_q % w == 0 and off_g % w == 0
    qb, gb = off_q // w, off_g // w
    keys = pl.BlockSpec((1, t, w), lambda bi, p, i: (bi, 0, p))
    return pl.pallas_call(
        functools.partial(_fox_attn_kernel, tk=tk),
        grid=(b, fw // w, t // tq),
        in_specs=[pl.BlockSpec((1, tq, w), lambda bi, p, i: (bi, i, qb + p)), keys, keys,
                  pl.BlockSpec((v_rows, t), lambda bi, p, i: (p, bi)),
                  pl.BlockSpec((1, tq, w), lambda bi, p, i: (bi, i, gb + p))],
        out_specs=pl.BlockSpec((1, tq, w), lambda bi, p, i: (bi, i, p)),
        out_shape=jax.ShapeDtypeStruct((b, t, fw), BF16),
        scratch_shapes=[pltpu.VMEM((2 * n_pairs, 1, tq), F32), pltpu.VMEM((2 * n_pairs, FOX_V_ROWS, tq), F32)],
        compiler_params=_params("parallel", "parallel", "arbitrary"),
        name="fox_attn",
    )(pb3, k0, k1, vt, pt3)


def _mem_attn_kernel(q_ref, kv_ref, g_ref, o_ref):
    mw = q_ref.shape[2]
    hd = mw // MEM_HEADS
    scale = hd ** -0.5
    outs = []
    for h in range(MEM_HEADS):
        q = q_ref[0, :, h * hd:(h + 1) * hd]
        mk = kv_ref[0, :, h * hd:(h + 1) * hd]
        mv = kv_ref[0, :, mw + h * hd:mw + (h + 1) * hd]
        s = _dot(q, mk, _NT) * scale
        p = jnp.exp(s - jnp.max(s, axis=1, keepdims=True))
        l = jnp.sum(p, axis=1, keepdims=True)
        outs.append(_dot(p.astype(BF16), mv) / l)
    g = g_ref[0]
    o_ref[0] = (jnp.concatenate(outs, axis=1) * (g * _sigmoid(g))).astype(o_ref.dtype)


def _mem_attn(pq3, off_q, mkv3, pt3, off_g, mw):
    b, t, _ = pq3.shape
    tm = min(ROWS_MEM, t)
    assert t % tm == 0 and off_q % mw == 0 and off_g % mw == 0 and (mw // MEM_HEADS) % LANES == 0
    n_mem = mkv3.shape[1]
    qb, gb = off_q // mw, off_g // mw
    return pl.pallas_call(
        _mem_attn_kernel,
        grid=(b, t // tm),
        in_specs=[pl.BlockSpec((1, tm, mw), lambda bi, i: (bi, i, qb)),
                  pl.BlockSpec((1, n_mem, 2 * mw), lambda bi, i: (bi, 0, 0)),
                  pl.BlockSpec((1, tm, mw), lambda bi, i: (bi, i, gb))],
        out_specs=pl.BlockSpec((1, tm, mw), lambda bi, i: (bi, i, 0)),
        out_shape=jax.ShapeDtypeStruct((b, t, mw), BF16),
        compiler_params=_params("parallel", "parallel"),
        name="mem_attn",
    )(pq3, mkv3, pt3)


def _layer(x, mem, g_pre, w_in, mu_rwkv, w0, w_decay_up, a0, w_iclr_up, k_k, k_a, r_k,
           ln_x_w, ln_x_b, b_f, g_mem, w_mem_kv, w_out, g_post):
    b, t, d = x.shape
    n_mem = mem.shape[1]
    c = w0.shape[0]
    fh = b_f.shape[0]
    fw = fh * HEAD_DIM
    mw = w_mem_kv.shape[1] // 2
    lora = w_decay_up.shape[0]
    assert lora == LANES and w_iclr_up.shape[0] == LANES and fh <= WIDE
    assert w_in.shape[1] == 4 * c + 2 * lora + 4 * fw + fh + 2 * mw
    assert c % WIDE == 0 and fw % WIDE == 0 and mw % WIDE == 0

    o_g = 3 * c + 2 * lora
    o_fq = o_g + c
    o_fv = o_fq + 2 * fw
    o_f = o_fv + fw
    o_gfox = o_f + fh
    o_mq = o_gfox + fw
    o_gmq = o_mq + mw

    x2 = x.reshape(b * t, d)
    h = _rmsnorm(x2, g_pre, BF16)

    lora_f = WIDE - 2 * lora
    assert o_f % LANES == 0 and o_f + lora_f <= w_in.shape[1]
    pieces_a = ((o_g, c), (0, 3 * c), (o_gmq, mw), (o_gfox, fw), (3 * c, 2 * lora), (o_f, lora_f))
    pieces_b = ((o_fq, 2 * fw), (o_mq, mw))
    w_a, w_b, w_vt = _weight_prep(w_in.T, pieces_a, pieces_b, (o_fv, fw))
    off_a = {"g_rwkv": 0, "r": c, "g_mq": 4 * c, "g_fox": 4 * c + mw, "lora": 4 * c + mw + fw}
    off_b = {"fq": 0, "fk": fw, "mq": 2 * fw}

    pa3 = _matmul(h, w_a, F32, 2 * WIDE, "in_proj_a").reshape(b, t, -1)
    pb3 = _matmul(h, w_b, BF16, 2 * WIDE, "in_proj_b").reshape(b, t, -1)
    vt = _proj_t(w_vt, h)

    y_rwkv = _rwkv_scan(pa3, off_a["r"], off_a["lora"], off_a["g_rwkv"], mu_rwkv, w0, w_decay_up, a0, w_iclr_up,
                        k_k, k_a, r_k.reshape(-1), ln_x_w, ln_x_b)

    k0, k1 = _fox_prefix(pa3, off_a["lora"], b_f, pb3, off_b["fk"], fw)
    y_fox = _fox_attn(pb3, off_b["fq"], k0, k1, vt, pa3, off_a["g_fox"], fw)

    hm = _rmsnorm(mem.reshape(b * n_mem, d), g_mem, BF16)
    mkv3 = _proj([hm], [(w_mem_kv, (d, WIDE), lambda j: (0, j), (0, 0))], 2 * mw, WIDE, BF16,
                 "mem_kv_proj").reshape(b, n_mem, 2 * mw)
    y_mem = _mem_attn(pb3, off_b["mq"], mkv3, pa3, off_a["g_mq"], mw)

    ys = [y_rwkv.reshape(b * t, c), y_fox.reshape(b * t, fw), y_mem.reshape(b * t, mw)]
    return _out_proj(ys, _cast(w_out, BF16, "w_out_cast"), x2, g_post).reshape(b, t, d)


def kernel(x, mem, g_pre, w_in, mu_rwkv, w0, w_decay_up, a0, w_iclr_up, k_k, k_a, r_k, ln_x_w, ln_x_b, b_f,
           g_mem, w_mem_kv, w_out, g_post):
    for l in range(g_pre.shape[0]):
        x = _layer(x, mem, g_pre[l], w_in[l], mu_rwkv[l], w0[l], w_decay_up[l], a0[l], w_iclr_up[l], k_k[l],
                   k_a[l], r_k[l], ln_x_w[l], ln_x_b[l], b_f[l], g_mem[l], w_mem_kv[l], w_out[l], g_post[l])
    return x
```

```python
import functools

import jax
import jax.numpy as jnp
from jax import lax
from jax.experimental import pallas as pl
from jax.experimental.pallas import tpu as pltpu

HEAD_DIM = 64
MEM_HEADS = 4
RMS_EPS = 1e-6
GN_EPS = 64e-5
LOG2E = 1.4426950408889634

LANES = 128
MXU = 256
UNIT_HEADS = MXU // HEAD_DIM
UNIT = UNIT_HEADS * HEAD_DIM
RWKV_CHUNK = 64
WIDE = 512
VMEM_LIMIT_BYTES = 56 * 1024 * 1024
FOX_BIAS_PARTS = 3
FOX_V_ROWS = HEAD_DIM + 16

ROWS_ELEMENTWISE = 256
ROWS_PROJ = 1024
ROWS_PROJ_T = 1024
ROWS_OUT = 512
COLS_WEIGHT_PREP = 128
RWKV_CHUNKS_PER_STEP = 4
ROWS_PREFIX = 512
ROWS_MEM = 512
FOX_TQ = 512
FOX_TK = 256
FOX_PAIRS = 4
RWKV_UNITS = 6

F32 = jnp.float32
BF16 = jnp.bfloat16

_NN = (((1,), (0,)), ((), ()))
_NT = (((1,), (1,)), ((), ()))
_TN = (((0,), (0,)), ((), ()))


def _dot(a, b, dims=_NN):
    return lax.dot_general(a, b, dims, preferred_element_type=F32)


def _split_bf16(x, n):
    parts = []
    rem = x
    for _ in range(n):
        p = rem.astype(BF16)
        parts.append(p)
        rem = rem - p.astype(F32)
    return parts


def _dot_split(a, b_bf16, n, dims=_NN):
    acc = None
    for p in _split_bf16(a, n):
        d = _dot(p, b_bf16, dims)
        acc = d if acc is None else acc + d
    return acc


def _dot_f32(a, b):
    a_hi, a_lo = _split_bf16(a, 2)
    b_hi, b_lo = _split_bf16(b, 2)
    return _dot(a_hi, b_hi) + (_dot(a_lo, b_hi) + _dot(a_hi, b_lo))


def _sigmoid(x):
    return 1.0 / (1.0 + jnp.exp(-x))


def _softplus(x):
    return jnp.maximum(x, 0.0) + jnp.log(1.0 + jnp.exp(-jnp.abs(x)))


def _params(*semantics):
    return pltpu.CompilerParams(dimension_semantics=semantics, vmem_limit_bytes=VMEM_LIMIT_BYTES)


def _rmsnorm_kernel(x_ref, g_ref, o_ref):
    x = x_ref[...]
    ms = jnp.mean(x * x, axis=-1, keepdims=True)
    o_ref[...] = (x * lax.rsqrt(ms + RMS_EPS) * g_ref[...]).astype(o_ref.dtype)


def _rmsnorm(x2d, g, out_dtype):
    m, d = x2d.shape
    tm = min(ROWS_ELEMENTWISE, m)
    assert m % tm == 0
    return pl.pallas_call(
        _rmsnorm_kernel,
        grid=(m // tm,),
        in_specs=[pl.BlockSpec((tm, d), lambda i: (i, 0)), pl.BlockSpec((1, d), lambda i: (0, 0))],
        out_specs=pl.BlockSpec((tm, d), lambda i: (i, 0)),
        out_shape=jax.ShapeDtypeStruct((m, d), out_dtype),
        compiler_params=_params("parallel"),
        name="rmsnorm",
    )(x2d, g.reshape(1, d))


def _proj_kernel(*refs, n_x, placement):
    xs = refs[:n_x]
    ws = refs[n_x:n_x + len(placement)]
    o_ref, wb_ref = refs[n_x + len(placement):]

    @pl.when(pl.program_id(1) == 0)
    def _():
        for w_ref, (r0, c0) in zip(ws, placement):
            wb_ref[r0:r0 + w_ref.shape[0], c0:c0 + w_ref.shape[1]] = w_ref[...].astype(BF16)

    acc = None
    r0 = 0
    for x_ref in xs:
        kx = x_ref.shape[1]
        d = _dot(x_ref[...], wb_ref[r0:r0 + kx, :])
        acc = d if acc is None else acc + d
        r0 += kx
    o_ref[...] = acc.astype(o_ref.dtype)


def _proj(xs, weights, n_cols, tn, out_dtype, name):
    m = xs[0].shape[0]
    k_total = sum(x.shape[1] for x in xs)
    tm = min(ROWS_PROJ, m)
    tn = min(tn, n_cols)
    assert m % tm == 0 and n_cols % tn == 0
    in_specs = [pl.BlockSpec((tm, x.shape[1]), lambda j, i: (i, 0)) for x in xs]
    for _, shape, index_fn, _ in weights:
        in_specs.append(pl.BlockSpec(shape, lambda j, i, f=index_fn: f(j)))
    return pl.pallas_call(
        functools.partial(_proj_kernel, n_x=len(xs), placement=tuple(p for _, _, _, p in weights)),
        grid=(n_cols // tn, m // tm),
        in_specs=in_specs,
        out_specs=pl.BlockSpec((tm, tn), lambda j, i: (i, j)),
        out_shape=jax.ShapeDtypeStruct((m, n_cols), out_dtype),
        scratch_shapes=[pltpu.VMEM((k_total, tn), BF16)],
        compiler_params=_params("parallel", "arbitrary"),
        name=name,
    )(*xs, *[w for w, _, _, _ in weights])


def _matmul_kernel(x_ref, w_ref, o_ref):
    o_ref[...] = _dot(x_ref[...], w_ref[...]).astype(o_ref.dtype)


def _matmul(x, w, out_dtype, tn, name):
    m, k = x.shape
    n = w.shape[1]
    tm = min(ROWS_PROJ, m)
    tn = min(tn, n)
    assert m % tm == 0 and n % tn == 0
    return pl.pallas_call(
        _matmul_kernel,
        grid=(n // tn, m // tm),
        in_specs=[pl.BlockSpec((tm, k), lambda j, i: (i, 0)), pl.BlockSpec((k, tn), lambda j, i: (0, j))],
        out_specs=pl.BlockSpec((tm, tn), lambda j, i: (i, j)),
        out_shape=jax.ShapeDtypeStruct((m, n), out_dtype),
        compiler_params=_params("parallel", "parallel"),
        name=name,
    )(x, w)


def _weight_prep_kernel(wt_ref, wa_ref, wb_ref, wvt_ref, *, pieces_a, pieces_b, piece_vt):
    def gather(pieces, o_ref):
        dst = 0
        for src, width in pieces:
            o_ref[:, dst:dst + width] = wt_ref[src:src + width, :].T.astype(o_ref.dtype)
            dst += width

    gather(pieces_a, wa_ref)
    gather(pieces_b, wb_ref)
    src, width = piece_vt
    wvt_ref[...] = wt_ref[src:src + width, :].astype(wvt_ref.dtype)


def _weight_prep(w_in_t, pieces_a, pieces_b, piece_vt):
    n, k = w_in_t.shape
    tc = min(COLS_WEIGHT_PREP, k)
    assert k % tc == 0 and all(s % 8 == 0 and wd % 16 == 0 for s, wd in pieces_a + pieces_b + (piece_vt,))
    na = sum(wd for _, wd in pieces_a)
    nb = sum(wd for _, wd in pieces_b)
    return pl.pallas_call(
        functools.partial(_weight_prep_kernel, pieces_a=pieces_a, pieces_b=pieces_b, piece_vt=piece_vt),
        grid=(k // tc,),
        in_specs=[pl.BlockSpec((n, tc), lambda i: (0, i))],
        out_specs=[pl.BlockSpec((tc, na), lambda i: (i, 0)), pl.BlockSpec((tc, nb), lambda i: (i, 0)),
                   pl.BlockSpec((piece_vt[1], tc), lambda i: (0, i))],
        out_shape=[jax.ShapeDtypeStruct((k, na), BF16), jax.ShapeDtypeStruct((k, nb), BF16),
                   jax.ShapeDtypeStruct((piece_vt[1], k), BF16)],
        compiler_params=_params("parallel"),
        name="weight_prep",
    )(w_in_t)


def _proj_t_kernel(wt_ref, x_ref, v_ref):
    vt = _dot(wt_ref[...], x_ref[...], _NT)
    ones = jnp.ones((FOX_V_ROWS - HEAD_DIM, vt.shape[1]), v_ref.dtype)
    for h in range(vt.shape[0] // HEAD_DIM):
        v_ref[h * FOX_V_ROWS:h * FOX_V_ROWS + HEAD_DIM, :] = vt[h * HEAD_DIM:(h + 1) * HEAD_DIM].astype(v_ref.dtype)
        v_ref[h * FOX_V_ROWS + HEAD_DIM:(h + 1) * FOX_V_ROWS, :] = ones


def _proj_t(wt, x):
    n, k = wt.shape
    m = x.shape[0]
    tm = min(ROWS_PROJ_T, m)
    assert m % tm == 0 and n % HEAD_DIM == 0
    n_aug = n // HEAD_DIM * FOX_V_ROWS
    return pl.pallas_call(
        _proj_t_kernel,
        grid=(m // tm,),
        in_specs=[pl.BlockSpec((n, k), lambda i: (0, 0)), pl.BlockSpec((tm, k), lambda i: (i, 0))],
        out_specs=pl.BlockSpec((n_aug, tm), lambda i: (0, i)),
        out_shape=jax.ShapeDtypeStruct((n_aug, m), BF16),
        compiler_params=_params("parallel"),
        name="in_proj_vt",
    )(wt, x)


def _cast_kernel(x_ref, o_ref):
    o_ref[...] = x_ref[...].astype(o_ref.dtype)


def _cast(x, dtype, name):
    m, n = x.shape
    tm = min(ROWS_OUT, m)
    assert m % tm == 0
    return pl.pallas_call(
        _cast_kernel,
        grid=(m // tm,),
        in_specs=[pl.BlockSpec((tm, n), lambda i: (i, 0))],
        out_specs=pl.BlockSpec((tm, n), lambda i: (i, 0)),
        out_shape=jax.ShapeDtypeStruct((m, n), dtype),
        compiler_params=_params("parallel"),
        name=name,
    )(x)


def _out_proj_kernel(*refs, n_y):
    ys = refs[:n_y]
    w_ref, x_ref, g_ref, o_ref, ss_ref = refs[n_y:]
    j = pl.program_id(1)
    tn = w_ref.shape[1]
    d = o_ref.shape[1]

    acc = None
    r0 = 0
    for y_ref in ys:
        ky = y_ref.shape[1]
        part = _dot(y_ref[...], w_ref[r0:r0 + ky, :])
        acc = part if acc is None else acc + part
        r0 += ky

    @pl.when(j == 0)
    def _():
        ss_ref[...] = jnp.zeros_like(ss_ref)

    ss_ref[...] += jnp.sum(acc * acc, axis=1, keepdims=True)
    o_ref[:, pl.ds(pl.multiple_of(j * tn, tn), tn)] = acc

    @pl.when(j == pl.num_programs(1) - 1)
    def _():
        scale = lax.rsqrt(ss_ref[...] * (1.0 / d) + RMS_EPS)
        for c0 in range(0, d, tn):
            sl = slice(c0, c0 + tn)
            o_ref[:, sl] = x_ref[:, sl] + o_ref[:, sl] * scale * g_ref[:, sl]


def _out_proj(ys, w_bf16, x2d, g):
    m, d = x2d.shape
    tm = min(ROWS_OUT, m)
    tn = min(WIDE, d)
    assert m % tm == 0 and d % tn == 0 and sum(y.shape[1] for y in ys) == w_bf16.shape[0]
    row = pl.BlockSpec((tm, d), lambda i, j: (i, 0))
    return pl.pallas_call(
        functools.partial(_out_proj_kernel, n_y=len(ys)),
        grid=(m // tm, d // tn),
        in_specs=[pl.BlockSpec((tm, y.shape[1]), lambda i, j: (i, 0)) for y in ys]
                 + [pl.BlockSpec((w_bf16.shape[0], tn), lambda i, j: (0, j)), row,
                    pl.BlockSpec((1, d), lambda i, j: (0, 0))],
        out_specs=row,
        out_shape=jax.ShapeDtypeStruct((m, d), x2d.dtype),
        scratch_shapes=[pltpu.VMEM((tm, 1), F32)],
        compiler_params=_params("parallel", "arbitrary"),
        name="out_proj_post",
    )(*ys, w_bf16, x2d, g.reshape(1, d))


def _rwkv_scan_kernel(*refs, n_units, L):
    r_ref, k_ref, v_ref, lora_ref = refs[:4]
    g_refs = refs[4:4 + n_units]
    (mu_r_ref, mu_k_ref, mu_v_ref, mu_lora_ref, w0_ref, wdu_ref, a0_ref, wiu_ref, kk_ref, ka_ref,
     rk_ref, lnw_ref, lnb_ref, o_ref, s_ref, carry_r, carry_k, carry_v, carry_lora) = refs[4 + n_units:]

    @pl.when(pl.program_id(2) == 0)
    def _():
        for z in (s_ref, carry_r, carry_k, carry_v, carry_lora):
            z[...] = jnp.zeros_like(z)

    rows = r_ref.shape[1]
    n_ch = rows // L
    first_row = lax.broadcasted_iota(jnp.int32, (rows, 1), 0) == 0

    def shift(x, carry_ref, mu):
        prev = jnp.where(first_row, carry_ref[...], pltpu.roll(x, 1, 0))
        carry_ref[...] = x[rows - 1:rows, :]
        return x + (prev - x) * mu

    r_all = shift(r_ref[0], carry_r, mu_r_ref[...])
    k_raw = shift(k_ref[0], carry_k, mu_k_ref[...])
    v_all = shift(v_ref[0], carry_v, mu_v_ref[...])
    lora = shift(lora_ref[0][:, :2 * LANES], carry_lora, mu_lora_ref[...])
    w_pre = -_softplus(-(w0_ref[...] + _dot_f32(jnp.tanh(lora[:, :LANES]), wdu_ref[...]))) - 0.5
    lw_all = -jnp.exp(w_pre)
    alpha = _sigmoid(a0_ref[...] + _dot_f32(lora[:, LANES:], wiu_ref[...]))
    k_all = k_raw * (1.0 + (alpha - 1.0) * ka_ref[...])
    kk_all = k_raw * kk_ref[...]
    hl = UNIT_HEADS * L
    n_sq = L.bit_length() - 2
    assert 2 ** (n_sq + 1) == L
    units = range(n_units)

    def blk(shape, d0, d1):
        return (lax.broadcasted_iota(jnp.int32, shape, 0) // d0) == (lax.broadcasted_iota(jnp.int32, shape, 1) // d1)

    same = blk((hl, hl), L, L)
    u_row = lax.broadcasted_iota(jnp.int32, (L, hl), 0)
    u_col = lax.broadcasted_iota(jnp.int32, (L, hl), 1) % L
    strict_u = u_col < u_row
    incl_u = u_col <= u_row
    eye_u = jnp.where(u_col == u_row, 1.0, 0.0)
    bd_rows = blk((hl, UNIT), L, HEAD_DIM)
    bd_state = blk((UNIT, UNIT), HEAD_DIM, HEAD_DIM)
    ones_bd = jnp.where(bd_state, 1.0, 0.0).astype(BF16)
    c_row = lax.broadcasted_iota(jnp.int32, (rows, rows), 0)
    c_col = lax.broadcasted_iota(jnp.int32, (rows, rows), 1)
    tril = jnp.where((c_row // L == c_col // L) & (c_row >= c_col), 1.0, 0.0).astype(BF16)

    def rep(x):
        return jnp.concatenate([x] * UNIT_HEADS, axis=0)

    def stack(x):
        return jnp.where(bd_rows, rep(x), 0.0)

    def headsum(x):
        return _dot(x.astype(BF16), ones_bd)

    def lanes(x, u):
        return x[:, u * UNIT:(u + 1) * UNIT]

    chains = [(ch, u) for ch in range(n_ch) for u in units]
    lw_hi, lw_lo = _split_bf16(lw_all, 2)
    cl_all = _dot(tril, lw_hi) + _dot(tril, lw_lo)
    kk_norm = [headsum(lanes(kk_all, u) * lanes(kk_all, u)) for u in units]
    kk_unit = [lanes(kk_all, u) * lax.rsqrt(jnp.maximum(kk_norm[u], 1e-24)) for u in units]

    ah, rh, e_last, lhs, rhs_b, rhs_k, upd_rhs, v = [], [], [], [], [], [], [], []
    for ch, u in chains:
        rs = slice(ch * L, (ch + 1) * L)
        sl = slice(u * UNIT, (u + 1) * UNIT)
        cl = cl_all[rs, sl]
        e_pos = jnp.exp(cl)
        e_neg = jnp.exp(-cl)
        e_rem = jnp.exp(cl[L - 1:L, :] - cl)
        kk = kk_unit[u][rs]
        a, b, k = -kk, kk * alpha[rs, sl], k_all[rs, sl]
        ah.append(a * jnp.exp(cl - lw_all[rs, sl]))
        rh.append(r_all[rs, sl] * e_pos)
        e_last.append(e_pos[L - 1:L, :])
        lhs.append(jnp.concatenate([ah[-1], rh[-1]], axis=0).astype(BF16))
        rhs_b.append(stack(b * e_neg).astype(BF16))
        rhs_k.append(stack(k * e_neg).astype(BF16))
        upd_rhs.append(jnp.concatenate([b * e_rem, k * e_rem], axis=0).astype(BF16))
        v.append(v_all[rs, sl])

    ids = range(len(chains))
    a_b = [_dot(lhs[i], rhs_b[i], _NT) for i in ids]
    a_k = [_dot(lhs[i], rhs_k[i], _NT) for i in ids]
    pw = [jnp.where(strict_u, a_b[i][:L], 0.0) for i in ids]
    a_ak = [jnp.where(strict_u, a_k[i][:L], 0.0).astype(BF16) for i in ids]
    a_rb = [jnp.where(incl_u, a_b[i][L:], 0.0).astype(BF16) for i in ids]
    a_rk = [jnp.where(incl_u, a_k[i][L:], 0.0).astype(BF16) for i in ids]

    def blockdiag(m_u):
        return jnp.where(same, rep(m_u), 0.0).astype(BF16)

    t_u = [eye_u + pw[i] for i in ids]
    pw_bd = [blockdiag(pw[i]) for i in ids]
    for _ in range(n_sq):
        pw = [_dot(pw[i].astype(BF16), pw_bd[i]) for i in ids]
        pw_bd = [blockdiag(pw[i]) for i in ids]
        t_u = [t_u[i] + _dot(t_u[i].astype(BF16), pw_bd[i]) for i in ids]
    t_u = [t_u[i].astype(BF16) for i in ids]
    v_s = [stack(v[i]).astype(BF16) for i in ids]

    s_cur = [s_ref[u] for u in units]
    y = [None] * len(chains)
    for ch in range(n_ch):
        cid = [ch * n_units + u for u in units]
        xr = [_dot(lhs[i], s_cur[u].astype(BF16), _NT) for u, i in zip(units, cid)]
        x = [xr[u][:L] + _dot(a_ak[i], v_s[i]) for u, i in zip(units, cid)]
        uu = [_dot(t_u[i], stack(x[u]).astype(BF16)) for u, i in zip(units, cid)]
        for u, i in zip(units, cid):
            y[i] = xr[u][L:] + _dot(a_rb[i], stack(uu[u]).astype(BF16)) + _dot(a_rk[i], v_s[i])
            upd = _dot(jnp.concatenate([uu[u], v[i]], axis=0).astype(BF16), upd_rhs[i], _TN)
            s_cur[u] = jnp.where(bd_state, s_cur[u] * e_last[i] + upd, 0.0)
    for u in units:
        s_ref[u] = s_cur[u]

    inv_n = 1.0 / HEAD_DIM
    y = [jnp.concatenate([y[ch * n_units + u] for ch in range(n_ch)], axis=0) for u in units]
    d = [y[u] - headsum(y[u]) * inv_n for u in units]
    var = [headsum(d[u] * d[u]) * inv_n for u in units]
    bonus = [headsum(lanes(r_all, u) * lanes(k_all, u) * lanes(rk_ref[...], u)) * lanes(v_all, u) for u in units]
    for u in units:
        sl = slice(u * UNIT, (u + 1) * UNIT)
        yn = d[u] * lax.rsqrt(var[u] + GN_EPS) * lnw_ref[:, sl] + lnb_ref[:, sl]
        g = g_refs[u][0]
        o_ref[0, :, sl] = ((yn + bonus[u]) * (g * _sigmoid(g))).astype(o_ref.dtype)


def _rwkv_scan(pa3, off_r, off_lora, off_g, mu, w0, wdu, a0, wiu, k_k, k_a, r_k, ln_w, ln_b):
    b, t, _ = pa3.shape
    c = w0.shape[0]
    L = RWKV_CHUNK * min(RWKV_CHUNKS_PER_STEP, t // RWKV_CHUNK)
    n_units = min(RWKV_UNITS, c // UNIT)
    w = n_units * UNIT
    assert t % L == 0 and c % w == 0 and off_r % w == 0 and off_g % UNIT == 0 and off_lora % WIDE == 0
    rb, gb, lb, ng = off_r // w, off_g // UNIT, off_lora // WIDE, c // w

    def col(base):
        return pl.BlockSpec((1, L, w), lambda bi, gi, ci: (bi, ci, base + gi))

    def vec(base=0):
        return pl.BlockSpec((1, w), lambda bi, gi, ci: (0, base + gi))

    gates = [pl.BlockSpec((1, L, UNIT), lambda bi, gi, ci, u=u: (bi, ci, gb + gi * n_units + u))
             for u in range(n_units)]
    lora_w = pl.BlockSpec((wdu.shape[0], w), lambda bi, gi, ci: (0, gi))
    mu_rkv = mu[:3 * c].reshape(1, 3 * c)
    mu_lora = mu[3 * c:].reshape(1, 2 * LANES)
    row = lambda z: z.reshape(1, c)
    return pl.pallas_call(
        functools.partial(_rwkv_scan_kernel, n_units=n_units, L=RWKV_CHUNK),
        grid=(b, ng, t // L),
        in_specs=[col(rb), col(rb + ng), col(rb + 2 * ng),
                  pl.BlockSpec((1, L, WIDE), lambda bi, gi, ci: (bi, ci, lb))] + gates
                 + [vec(0), vec(ng), vec(2 * ng), pl.BlockSpec((1, 2 * LANES), lambda bi, gi, ci: (0, 0)),
                    vec(), lora_w, vec(), lora_w, vec(), vec(), vec(), vec(), vec()],
        out_specs=pl.BlockSpec((1, L, w), lambda bi, gi, ci: (bi, ci, gi)),
        out_shape=jax.ShapeDtypeStruct((b, t, c), BF16),
        scratch_shapes=[pltpu.VMEM((n_units, UNIT, UNIT), F32)] + [pltpu.VMEM((1, w), F32)] * 3
                       + [pltpu.VMEM((1, 2 * LANES), F32)],
        compiler_params=_params("parallel", "parallel", "arbitrary"),
        name="rwkv_scan",
    )(pa3, pa3, pa3, pa3, *([pa3] * n_units), mu_rkv, mu_rkv, mu_rkv, mu_lora,
      row(w0), wdu, row(a0), wiu, row(k_k), row(k_a), row(r_k), row(ln_w), row(ln_b))


def _fox_prefix_kernel(f_ref, bf_ref, k_ref, k0_ref, k1_ref, carry_ref):
    @pl.when(pl.program_id(1) == 0)
    def _():
        carry_ref[...] = jnp.zeros_like(carry_ref)

    x = f_ref[0][:, 2 * LANES:3 * LANES] + bf_ref[...]
    log_f = -_softplus(-x)
    tc = x.shape[0]
    tril = jnp.where(lax.broadcasted_iota(jnp.int32, (tc, tc), 0) >= lax.broadcasted_iota(jnp.int32, (tc, tc), 1),
                     1.0, 0.0).astype(BF16)
    c = carry_ref[...]
    for part in _split_bf16(log_f, 3):
        c = c + _dot(tril, part)
    carry_ref[...] = c[tc - 1:tc, :]

    fw = k_ref.shape[2]
    src = lax.broadcasted_iota(jnp.int32, (LANES, fw), 0)
    dst = lax.broadcasted_iota(jnp.int32, (LANES, fw), 1)
    lane = dst % LANES
    head = 2 * (dst // LANES) + jnp.where(lane < HEAD_DIM, 1, 0)
    bias = None
    for i, part in enumerate(_split_bf16(c * LOG2E, FOX_BIAS_PARTS)):
        sel = jnp.where((src == head) & (lane % HEAD_DIM == i), 1.0, 0.0).astype(BF16)
        d = _dot(part, sel)
        bias = d if bias is None else bias + d
    bias = bias.astype(BF16)
    k = k_ref[0]
    first = (lax.broadcasted_iota(jnp.int32, (tc, fw), 1) % LANES) < HEAD_DIM
    k0_ref[0] = jnp.where(first, k, bias)
    k1_ref[0] = jnp.where(first, bias, k)


def _fox_prefix(pt3, off_f, b_f, pb3, off_k, fw):
    b, t, _ = pt3.shape
    fh = b_f.shape[0]
    tc = min(ROWS_PREFIX, t)
    assert t % tc == 0 and fh <= LANES and fw == fh * HEAD_DIM and off_f % WIDE == 0 and off_k % fw == 0
    bf = jnp.zeros((1, LANES), F32).at[0, :fh].set(b_f)
    fb, kb = off_f // WIDE, off_k // fw
    out = pl.BlockSpec((1, tc, fw), lambda bi, i: (bi, i, 0))
    return pl.pallas_call(
        _fox_prefix_kernel,
        grid=(b, t // tc),
        in_specs=[pl.BlockSpec((1, tc, WIDE), lambda bi, i: (bi, i, fb)),
                  pl.BlockSpec((1, LANES), lambda bi, i: (0, 0)),
                  pl.BlockSpec((1, tc, fw), lambda bi, i: (bi, i, kb))],
        out_specs=[out, out],
        out_shape=[jax.ShapeDtypeStruct((b, t, fw), BF16)] * 2,
        scratch_shapes=[pltpu.VMEM((1, LANES), F32)],
        compiler_params=_params("parallel", "arbitrary"),
        name="fox_prefix",
    )(pt3, bf, pb3)


def _fox_attn_kernel(q_ref, k0_ref, k1_ref, v_ref, g_ref, o_ref, m_ref, acc_ref, *, tk):
    qi = pl.program_id(2)
    tq = q_ref.shape[1]
    n_pairs = q_ref.shape[2] // LANES
    n_sub = tq // tk
    q_lane = lax.broadcasted_iota(jnp.int32, (tq, LANES), 1)
    q_head0 = q_lane < HEAD_DIM
    bias0 = jnp.where((q_lane >= HEAD_DIM) & (q_lane < HEAD_DIM + FOX_BIAS_PARTS), -1.0, 0.0)
    bias1 = jnp.where(q_lane < FOX_BIAS_PARTS, -1.0, 0.0)
    k_refs = (k0_ref, k1_ref)
    chains = [(pr, h) for pr in range(n_pairs) for h in range(2)]

    qs = []
    for pr in range(n_pairs):
        q = q_ref[0, :, pr * LANES:(pr + 1) * LANES].astype(F32) * (HEAD_DIM ** -0.5 * LOG2E)
        qs += [jnp.where(q_head0, q, bias0).astype(BF16), jnp.where(q_head0, bias1, q).astype(BF16)]

    m_ref[...] = jnp.full_like(m_ref, -jnp.inf)
    acc_ref[...] = jnp.zeros_like(acc_ref)

    def tile(j, q0, causal):
        start = pl.multiple_of(j * tk, tk)
        cols = slice(q0, tq)
        if causal:
            key = lax.broadcasted_iota(jnp.int32, (tk, tq - q0), 0)
            qry = lax.broadcasted_iota(jnp.int32, (tk, tq - q0), 1)
            valid = key <= qry
        scores = []
        for ci, (pr, h) in enumerate(chains):
            s = _dot(k_refs[h][0, pl.ds(start, tk), pr * LANES:(pr + 1) * LANES], qs[ci][q0:], _NT)
            scores.append(jnp.where(valid, s, -jnp.inf) if causal else s)
        for ci, s in enumerate(scores):
            m = m_ref[ci, :, cols]
            m_new = jnp.maximum(m, jnp.max(s, axis=0, keepdims=True))
            m_ref[ci, :, cols] = m_new
            p = jnp.exp2(s - m_new).astype(BF16)
            pv = _dot(v_ref[ci * FOX_V_ROWS:(ci + 1) * FOX_V_ROWS, pl.ds(start, tk)], p)
            acc_ref[ci, :, cols] = jnp.exp2(m - m_new) * acc_ref[ci, :, cols] + pv

    def body(j, carry):
        tile(j, 0, False)
        return carry

    lax.fori_loop(0, qi * n_sub, body, 0)
    for d in range(n_sub):
        tile(qi * n_sub + d, d * tk, True)

    for pr in range(n_pairs):
        sl = slice(pr * LANES, (pr + 1) * LANES)
        out_t = jnp.concatenate(
            [acc_ref[ci][:HEAD_DIM] * (1.0 / acc_ref[ci][HEAD_DIM:HEAD_DIM + 1]) for ci in (2 * pr, 2 * pr + 1)], axis=0)
        g = g_ref[0, :, sl]
        o_ref[0, :, sl] = (out_t.T * (g * _sigmoid(g))).astype(o_ref.dtype)


def _fox_attn(pb3, off_q, k0, k1, vt, pt3, off_g, fw):
    b, t, _ = pb3.shape
    tq = min(FOX_TQ, t)
    tk = min(FOX_TK, tq)
    n_pairs = min(FOX_PAIRS, fw // LANES)
    w = n_pairs * LANES
    v_rows = 2 * n_pairs * FOX_V_ROWS
    assert t % tq == 0 and tq % tk == 0 and fw % w == 0 and off_q % w == 0 and off_g % w == 0
    qb, gb = off_q // w, off_g // w
    keys = pl.BlockSpec((1, t, w), lambda bi, p, i: (bi, 0, p))
    return pl.pallas_call(
        functools.partial(_fox_attn_kernel, tk=tk),
        grid=(b, fw // w, t // tq),
        in_specs=[pl.BlockSpec((1, tq, w), lambda bi, p, i: (bi, i, qb + p)), keys, keys,
                  pl.BlockSpec((v_rows, t), lambda bi, p, i: (p, bi)),
                  pl.BlockSpec((1, tq, w), lambda bi, p, i: (bi, i, gb + p))],
        out_specs=pl.BlockSpec((1, tq, w), lambda bi, p, i: (bi, i, p)),
        out_shape=jax.ShapeDtypeStruct((b, t, fw), BF16),
        scratch_shapes=[pltpu.VMEM((2 * n_pairs, 1, tq), F32), pltpu.VMEM((2 * n_pairs, FOX_V_ROWS, tq), F32)],
        compiler_params=_params("parallel", "parallel", "arbitrary"),
        name="fox_attn",
    )(pb3, k0, k1, vt, pt3)


def _mem_attn_kernel(q_ref, kv_ref, g_ref, o_ref):
    mw = q_ref.shape[2]
    hd = mw // MEM_HEADS
    scale = hd ** -0.5
    outs = []
    for h in range(MEM_HEADS):
        q = q_ref[0, :, h * hd:(h + 1) * hd]
        mk = kv_ref[0, :, h * hd:(h + 1) * hd]
        mv = kv_ref[0, :, mw + h * hd:mw + (h + 1) * hd]
        s = _dot(q, mk, _NT) * scale
        p = jnp.exp(s - jnp.max(s, axis=1, keepdims=True))
        l = jnp.sum(p, axis=1, keepdims=True)
        outs.append(_dot(p.astype(BF16), mv) / l)
    g = g_ref[0]
    o_ref[0] = (jnp.concatenate(outs, axis=1) * (g * _sigmoid(g))).astype(o_ref.dtype)


def _mem_attn(pq3, off_q, mkv3, pt3, off_g, mw):
    b, t, _ = pq3.shape
    tm = min(ROWS_MEM, t)
    assert t % tm == 0 and off_q % mw == 0 and off_g % mw == 0 and (mw // MEM_HEADS) % LANES == 0
    n_mem = mkv3.shape[1]
    qb, gb = off_q // mw, off_g // mw
    return pl.pallas_call(
        _mem_attn_kernel,
        grid=(b, t // tm),
        in_specs=[pl.BlockSpec((1, tm, mw), lambda bi, i: (bi, i, qb)),
                  pl.BlockSpec((1, n_mem, 2 * mw), lambda bi, i: (bi, 0, 0)),
                  pl.BlockSpec((1, tm, mw), lambda bi, i: (bi, i, gb))],
        out_specs=pl.BlockSpec((1, tm, mw), lambda bi, i: (bi, i, 0)),
        out_shape=jax.ShapeDtypeStruct((b, t, mw), BF16),
        compiler_params=_params("parallel", "parallel"),
        name="mem_attn",
    )(pq3, mkv3, pt3)


def _layer(x, mem, g_pre, w_in, mu_rwkv, w0, w_decay_up, a0, w_iclr_up, k_k, k_a, r_k,
           ln_x_w, ln_x_b, b_f, g_mem, w_mem_kv, w_out, g_post):
    b, t, d = x.shape
    n_mem = mem.shape[1]
    c = w0.shape[0]
    fh = b_f.shape[0]
    fw = fh * HEAD_DIM
    mw = w_mem_kv.shape[1] // 2
    lora = w_decay_up.shape[0]
    assert lora == LANES and w_iclr_up.shape[0] == LANES and fh <= WIDE
    assert w_in.shape[1] == 4 * c + 2 * lora + 4 * fw + fh + 2 * mw
    assert c % WIDE == 0 and fw % WIDE == 0 and mw % WIDE == 0

    o_g = 3 * c + 2 * lora
    o_fq = o_g + c
    o_fv = o_fq + 2 * fw
    o_f = o_fv + fw
    o_gfox = o_f + fh
    o_mq = o_gfox + fw
    o_gmq = o_mq + mw

    x2 = x.reshape(b * t, d)
    h = _rmsnorm(x2, g_pre, BF16)

    lora_f = WIDE - 2 * lora
    assert o_f % LANES == 0 and o_f + lora_f <= w_in.shape[1]
    pieces_a = ((o_g, c), (0, 3 * c), (o_gmq, mw), (o_gfox, fw), (3 * c, 2 * lora), (o_f, lora_f))
    pieces_b = ((o_fq, 2 * fw), (o_mq, mw))
    w_a, w_b, w_vt = _weight_prep(w_in.T, pieces_a, pieces_b, (o_fv, fw))
    off_a = {"g_rwkv": 0, "r": c, "g_mq": 4 * c, "g_fox": 4 * c + mw, "lora": 4 * c + mw + fw}
    off_b = {"fq": 0, "fk": fw, "mq": 2 * fw}

    pa3 = _matmul(h, w_a, F32, 2 * WIDE, "in_proj_a").reshape(b, t, -1)
    pb3 = _matmul(h, w_b, BF16, 2 * WIDE, "in_proj_b").reshape(b, t, -1)
    vt = _proj_t(w_vt, h)

    y_rwkv = _rwkv_scan(pa3, off_a["r"], off_a["lora"], off_a["g_rwkv"], mu_rwkv, w0, w_decay_up, a0, w_iclr_up,
                        k_k, k_a, r_k.reshape(-1), ln_x_w, ln_x_b)

    k0, k1 = _fox_prefix(pa3, off_a["lora"], b_f, pb3, off_b["fk"], fw)
    y_fox = _fox_attn(pb3, off_b["fq"], k0, k1, vt, pa3, off_a["g_fox"], fw)

    hm = _rmsnorm(mem.reshape(b * n_mem, d), g_mem, BF16)
    mkv3 = _proj([hm], [(w_mem_kv, (d, WIDE), lambda j: (0, j), (0, 0))], 2 * mw, WIDE, BF16,
                 "mem_kv_proj").reshape(b, n_mem, 2 * mw)
    y_mem = _mem_attn(pb3, off_b["mq"], mkv3, pa3, off_a["g_mq"], mw)

    ys = [y_rwkv.reshape(b * t, c), y_fox.reshape(b * t, fw), y_mem.reshape(b * t, mw)]
    return _out_proj(ys, _cast(w_out, BF16, "w_out_cast"), x2, g_post).reshape(b, t, d)


def kernel(x, mem, g_pre, w_in, mu_rwkv, w0, w_decay_up, a0, w_iclr_up, k_k, k_a, r_k, ln_x_w, ln_x_b, b_f,
           g_mem, w_mem_kv, w_out, g_post):
    for l in range(g_pre.shape[0]):
        x = _layer(x, mem, g_pre[l], w_in[l], mu_rwkv[l], w0[l], w_decay_up[l], a0[l], w_iclr_up[l], k_k[l],
                   k_a[l], r_k[l], ln_x_w[l], ln_x_b[l], b_f[l], g_mem[l], w_mem_kv[l], w_out[l], g_post[l])
    return x
```

```python
import functools

import jax
import jax.numpy as jnp
from jax import lax
from jax.experimental import pallas as pl
from jax.experimental.pallas import tpu as pltpu

HEAD_DIM = 64
MEM_HEADS = 4
RMS_EPS = 1e-6
GN_EPS = 64e-5
LOG2E = 1.4426950408889634

LANES = 128
MXU = 256
UNIT_HEADS = MXU // HEAD_DIM
UNIT = UNIT_HEADS * HEAD_DIM
RWKV_CHUNK = 64
WIDE = 512
VMEM_LIMIT_BYTES = 56 * 1024 * 1024
FOX_BIAS_PARTS = 3
FOX_V_ROWS = HEAD_DIM + 16

ROWS_ELEMENTWISE = 512
ROWS_PROJ = 1024
ROWS_PROJ_T = 1024
ROWS_OUT = 512
COLS_WEIGHT_PREP = 256
RWKV_CHUNKS_PER_STEP = 4
ROWS_PREFIX = 512
ROWS_MEM = 512
FOX_TQ = 512
FOX_TK = 256
FOX_PAIRS = 4
RWKV_UNITS = 6

F32 = jnp.float32
BF16 = jnp.bfloat16

_NN = (((1,), (0,)), ((), ()))
_NT = (((1,), (1,)), ((), ()))
_TN = (((0,), (0,)), ((), ()))


def _dot(a, b, dims=_NN):
    return lax.dot_general(a, b, dims, preferred_element_type=F32)


def _split_bf16(x, n):
    parts = []
    rem = x
    for _ in range(n):
        p = rem.astype(BF16)
        parts.append(p)
        rem = rem - p.astype(F32)
    return parts


def _dot_split(a, b_bf16, n, dims=_NN):
    acc = None
    for p in _split_bf16(a, n):
        d = _dot(p, b_bf16, dims)
        acc = d if acc is None else acc + d
    return acc


def _dot_f32(a, b):
    a_hi, a_lo = _split_bf16(a, 2)
    b_hi, b_lo = _split_bf16(b, 2)
    return _dot(a_hi, b_hi) + (_dot(a_lo, b_hi) + _dot(a_hi, b_lo))


def _sigmoid(x):
    return 1.0 / (1.0 + jnp.exp(-x))


def _softplus(x):
    return jnp.maximum(x, 0.0) + jnp.log(1.0 + jnp.exp(-jnp.abs(x)))


def _params(*semantics):
    return pltpu.CompilerParams(dimension_semantics=semantics, vmem_limit_bytes=VMEM_LIMIT_BYTES)


def _rmsnorm_kernel(x_ref, g_ref, o_ref):
    x = x_ref[...]
    ms = jnp.mean(x * x, axis=-1, keepdims=True)
    o_ref[...] = (x * lax.rsqrt(ms + RMS_EPS) * g_ref[...]).astype(o_ref.dtype)


def _rmsnorm(x2d, g, out_dtype):
    m, d = x2d.shape
    tm = min(ROWS_ELEMENTWISE, m)
    assert m % tm == 0
    return pl.pallas_call(
        _rmsnorm_kernel,
        grid=(m // tm,),
        in_specs=[pl.BlockSpec((tm, d), lambda i: (i, 0)), pl.BlockSpec((1, d), lambda i: (0, 0))],
        out_specs=pl.BlockSpec((tm, d), lambda i: (i, 0)),
        out_shape=jax.ShapeDtypeStruct((m, d), out_dtype),
        compiler_params=_params("parallel"),
        name="rmsnorm",
    )(x2d, g.reshape(1, d))


def _proj_kernel(*refs, n_x, placement):
    xs = refs[:n_x]
    ws = refs[n_x:n_x + len(placement)]
    o_ref, wb_ref = refs[n_x + len(placement):]

    @pl.when(pl.program_id(1) == 0)
    def _():
        for w_ref, (r0, c0) in zip(ws, placement):
            wb_ref[r0:r0 + w_ref.shape[0], c0:c0 + w_ref.shape[1]] = w_ref[...].astype(BF16)

    acc = None
    r0 = 0
    for x_ref in xs:
        kx = x_ref.shape[1]
        d = _dot(x_ref[...], wb_ref[r0:r0 + kx, :])
        acc = d if acc is None else acc + d
        r0 += kx
    o_ref[...] = acc.astype(o_ref.dtype)


def _proj(xs, weights, n_cols, tn, out_dtype, name):
    m = xs[0].shape[0]
    k_total = sum(x.shape[1] for x in xs)
    tm = min(ROWS_PROJ, m)
    tn = min(tn, n_cols)
    assert m % tm == 0 and n_cols % tn == 0
    in_specs = [pl.BlockSpec((tm, x.shape[1]), lambda j, i: (i, 0)) for x in xs]
    for _, shape, index_fn, _ in weights:
        in_specs.append(pl.BlockSpec(shape, lambda j, i, f=index_fn: f(j)))
    return pl.pallas_call(
        functools.partial(_proj_kernel, n_x=len(xs), placement=tuple(p for _, _, _, p in weights)),
        grid=(n_cols // tn, m // tm),
        in_specs=in_specs,
        out_specs=pl.BlockSpec((tm, tn), lambda j, i: (i, j)),
        out_shape=jax.ShapeDtypeStruct((m, n_cols), out_dtype),
        scratch_shapes=[pltpu.VMEM((k_total, tn), BF16)],
        compiler_params=_params("parallel", "arbitrary"),
        name=name,
    )(*xs, *[w for w, _, _, _ in weights])


def _matmul_kernel(x_ref, w_ref, o_ref):
    o_ref[...] = _dot(x_ref[...], w_ref[...]).astype(o_ref.dtype)


def _matmul(x, w, out_dtype, tn, name):
    m, k = x.shape
    n = w.shape[1]
    tm = min(ROWS_PROJ, m)
    tn = min(tn, n)
    assert m % tm == 0 and n % tn == 0
    return pl.pallas_call(
        _matmul_kernel,
        grid=(n // tn, m // tm),
        in_specs=[pl.BlockSpec((tm, k), lambda j, i: (i, 0)), pl.BlockSpec((k, tn), lambda j, i: (0, j))],
        out_specs=pl.BlockSpec((tm, tn), lambda j, i: (i, j)),
        out_shape=jax.ShapeDtypeStruct((m, n), out_dtype),
        compiler_params=_params("parallel", "parallel"),
        name=name,
    )(x, w)


def _weight_prep_kernel(wt_ref, wa_ref, wb_ref, wvt_ref, *, pieces_a, pieces_b, piece_vt):
    def gather(pieces, o_ref):
        dst = 0
        for src, width in pieces:
            o_ref[:, dst:dst + width] = wt_ref[src:src + width, :].T.astype(o_ref.dtype)
            dst += width

    gather(pieces_a, wa_ref)
    gather(pieces_b, wb_ref)
    src, width = piece_vt
    wvt_ref[...] = wt_ref[src:src + width, :].astype(wvt_ref.dtype)


def _weight_prep(w_in_t, pieces_a, pieces_b, piece_vt):
    n, k = w_in_t.shape
    tc = min(COLS_WEIGHT_PREP, k)
    assert k % tc == 0 and all(s % 8 == 0 and wd % 16 == 0 for s, wd in pieces_a + pieces_b + (piece_vt,))
    na = sum(wd for _, wd in pieces_a)
    nb = sum(wd for _, wd in pieces_b)
    return pl.pallas_call(
        functools.partial(_weight_prep_kernel, pieces_a=pieces_a, pieces_b=pieces_b, piece_vt=piece_vt),
        grid=(k // tc,),
        in_specs=[pl.BlockSpec((n, tc), lambda i: (0, i))],
        out_specs=[pl.BlockSpec((tc, na), lambda i: (i, 0)), pl.BlockSpec((tc, nb), lambda i: (i, 0)),
                   pl.BlockSpec((piece_vt[1], tc), lambda i: (0, i))],
        out_shape=[jax.ShapeDtypeStruct((k, na), BF16), jax.ShapeDtypeStruct((k, nb), BF16),
                   jax.ShapeDtypeStruct((piece_vt[1], k), BF16)],
        compiler_params=_params("parallel"),
        name="weight_prep",
    )(w_in_t)


def _proj_t_kernel(wt_ref, x_ref, v_ref):
    vt = _dot(wt_ref[...], x_ref[...], _NT)
    ones = jnp.ones((FOX_V_ROWS - HEAD_DIM, vt.shape[1]), v_ref.dtype)
    for h in range(vt.shape[0] // HEAD_DIM):
        v_ref[h * FOX_V_ROWS:h * FOX_V_ROWS + HEAD_DIM, :] = vt[h * HEAD_DIM:(h + 1) * HEAD_DIM].astype(v_ref.dtype)
        v_ref[h * FOX_V_ROWS + HEAD_DIM:(h + 1) * FOX_V_ROWS, :] = ones


def _proj_t(wt, x):
    n, k = wt.shape
    m = x.shape[0]
    tm = min(ROWS_PROJ_T, m)
    assert m % tm == 0 and n % HEAD_DIM == 0
    n_aug = n // HEAD_DIM * FOX_V_ROWS
    return pl.pallas_call(
        _proj_t_kernel,
        grid=(m // tm,),
        in_specs=[pl.BlockSpec((n, k), lambda i: (0, 0)), pl.BlockSpec((tm, k), lambda i: (i, 0))],
        out_specs=pl.BlockSpec((n_aug, tm), lambda i: (0, i)),
        out_shape=jax.ShapeDtypeStruct((n_aug, m), BF16),
        compiler_params=_params("parallel"),
        name="in_proj_vt",
    )(wt, x)


def _cast_kernel(x_ref, o_ref):
    o_ref[...] = x_ref[...].astype(o_ref.dtype)


def _cast(x, dtype, name):
    m, n = x.shape
    tm = min(ROWS_OUT, m)
    assert m % tm == 0
    return pl.pallas_call(
        _cast_kernel,
        grid=(m // tm,),
        in_specs=[pl.BlockSpec((tm, n), lambda i: (i, 0))],
        out_specs=pl.BlockSpec((tm, n), lambda i: (i, 0)),
        out_shape=jax.ShapeDtypeStruct((m, n), dtype),
        compiler_params=_params("parallel"),
        name=name,
    )(x)


def _out_proj_kernel(*refs, n_y):
    ys = refs[:n_y]
    w_ref, x_ref, g_ref, o_ref, ss_ref = refs[n_y:]
    j = pl.program_id(1)
    tn = w_ref.shape[1]
    d = o_ref.shape[1]

    acc = None
    r0 = 0
    for y_ref in ys:
        ky = y_ref.shape[1]
        part = _dot(y_ref[...], w_ref[r0:r0 + ky, :])
        acc = part if acc is None else acc + part
        r0 += ky

    @pl.when(j == 0)
    def _():
        ss_ref[...] = jnp.zeros_like(ss_ref)

    ss_ref[...] += jnp.sum(acc * acc, axis=1, keepdims=True)
    o_ref[:, pl.ds(pl.multiple_of(j * tn, tn), tn)] = acc

    @pl.when(j == pl.num_programs(1) - 1)
    def _():
        scale = lax.rsqrt(ss_ref[...] * (1.0 / d) + RMS_EPS)
        for c0 in range(0, d, tn):
            sl = slice(c0, c0 + tn)
            o_ref[:, sl] = x_ref[:, sl] + o_ref[:, sl] * scale * g_ref[:, sl]


def _out_proj(ys, w_bf16, x2d, g):
    m, d = x2d.shape
    tm = min(ROWS_OUT, m)
    tn = min(WIDE, d)
    assert m % tm == 0 and d % tn == 0 and sum(y.shape[1] for y in ys) == w_bf16.shape[0]
    row = pl.BlockSpec((tm, d), lambda i, j: (i, 0))
    return pl.pallas_call(
        functools.partial(_out_proj_kernel, n_y=len(ys)),
        grid=(m // tm, d // tn),
        in_specs=[pl.BlockSpec((tm, y.shape[1]), lambda i, j: (i, 0)) for y in ys]
                 + [pl.BlockSpec((w_bf16.shape[0], tn), lambda i, j: (0, j)), row,
                    pl.BlockSpec((1, d), lambda i, j: (0, 0))],
        out_specs=row,
        out_shape=jax.ShapeDtypeStruct((m, d), x2d.dtype),
        scratch_shapes=[pltpu.VMEM((tm, 1), F32)],
        compiler_params=_params("parallel", "arbitrary"),
        name="out_proj_post",
    )(*ys, w_bf16, x2d, g.reshape(1, d))


def _rwkv_scan_kernel(*refs, n_units, L):
    r_ref, k_ref, v_ref, lora_ref = refs[:4]
    g_refs = refs[4:4 + n_units]
    (mu_r_ref, mu_k_ref, mu_v_ref, mu_lora_ref, w0_ref, wdu_ref, a0_ref, wiu_ref, kk_ref, ka_ref,
     rk_ref, lnw_ref, lnb_ref, o_ref, s_ref, carry_r, carry_k, carry_v, carry_lora) = refs[4 + n_units:]

    @pl.when(pl.program_id(2) == 0)
    def _():
        for z in (s_ref, carry_r, carry_k, carry_v, carry_lora):
            z[...] = jnp.zeros_like(z)

    rows = r_ref.shape[1]
    n_ch = rows // L
    first_row = lax.broadcasted_iota(jnp.int32, (rows, 1), 0) == 0

    def shift(x, carry_ref, mu):
        prev = jnp.where(first_row, carry_ref[...], pltpu.roll(x, 1, 0))
        carry_ref[...] = x[rows - 1:rows, :]
        return x + (prev - x) * mu

    r_all = shift(r_ref[0], carry_r, mu_r_ref[...])
    k_raw = shift(k_ref[0], carry_k, mu_k_ref[...])
    v_all = shift(v_ref[0], carry_v, mu_v_ref[...])
    lora = shift(lora_ref[0][:, :2 * LANES], carry_lora, mu_lora_ref[...])
    w_pre = -_softplus(-(w0_ref[...] + _dot_f32(jnp.tanh(lora[:, :LANES]), wdu_ref[...]))) - 0.5
    lw_all = -jnp.exp(w_pre)
    alpha = _sigmoid(a0_ref[...] + _dot_f32(lora[:, LANES:], wiu_ref[...]))
    k_all = k_raw * (1.0 + (alpha - 1.0) * ka_ref[...])
    kk_all = k_raw * kk_ref[...]
    hl = UNIT_HEADS * L
    n_sq = L.bit_length() - 2
    assert 2 ** (n_sq + 1) == L
    units = range(n_units)

    def blk(shape, d0, d1):
        return (lax.broadcasted_iota(jnp.int32, shape, 0) // d0) == (lax.broadcasted_iota(jnp.int32, shape, 1) // d1)

    same = blk((hl, hl), L, L)
    u_row = lax.broadcasted_iota(jnp.int32, (L, hl), 0)
    u_col = lax.broadcasted_iota(jnp.int32, (L, hl), 1) % L
    strict_u = u_col < u_row
    incl_u = u_col <= u_row
    eye_u = jnp.where(u_col == u_row, 1.0, 0.0)
    bd_rows = blk((hl, UNIT), L, HEAD_DIM)
    bd_state = blk((UNIT, UNIT), HEAD_DIM, HEAD_DIM)
    ones_bd = jnp.where(bd_state, 1.0, 0.0).astype(BF16)
    c_row = lax.broadcasted_iota(jnp.int32, (rows, rows), 0)
    c_col = lax.broadcasted_iota(jnp.int32, (rows, rows), 1)
    tril = jnp.where((c_row // L == c_col // L) & (c_row >= c_col), 1.0, 0.0).astype(BF16)

    def rep(x):
        return jnp.concatenate([x] * UNIT_HEADS, axis=0)

    def stack(x):
        return jnp.where(bd_rows, rep(x), 0.0)

    def headsum(x):
        return _dot(x.astype(BF16), ones_bd)

    def lanes(x, u):
        return x[:, u * UNIT:(u + 1) * UNIT]

    chains = [(ch, u) for ch in range(n_ch) for u in units]
    lw_hi, lw_lo = _split_bf16(lw_all, 2)
    cl_all = _dot(tril, lw_hi) + _dot(tril, lw_lo)
    kk_norm = [headsum(lanes(kk_all, u) * lanes(kk_all, u)) for u in units]
    kk_unit = [lanes(kk_all, u) * lax.rsqrt(jnp.maximum(kk_norm[u], 1e-24)) for u in units]

    ah, rh, e_last, lhs, rhs_b, rhs_k, upd_rhs, v = [], [], [], [], [], [], [], []
    for ch, u in chains:
        rs = slice(ch * L, (ch + 1) * L)
        sl = slice(u * UNIT, (u + 1) * UNIT)
        cl = cl_all[rs, sl]
        e_pos = jnp.exp(cl)
        e_neg = jnp.exp(-cl)
        e_rem = jnp.exp(cl[L - 1:L, :] - cl)
        kk = kk_unit[u][rs]
        a, b, k = -kk, kk * alpha[rs, sl], k_all[rs, sl]
        ah.append(a * jnp.exp(cl - lw_all[rs, sl]))
        rh.append(r_all[rs, sl] * e_pos)
        e_last.append(e_pos[L - 1:L, :])
        lhs.append(jnp.concatenate([ah[-1], rh[-1]], axis=0).astype(BF16))
        rhs_b.append(stack(b * e_neg).astype(BF16))
        rhs_k.append(stack(k * e_neg).astype(BF16))
        upd_rhs.append(jnp.concatenate([b * e_rem, k * e_rem], axis=0).astype(BF16))
        v.append(v_all[rs, sl])

    ids = range(len(chains))
    a_b = [_dot(lhs[i], rhs_b[i], _NT) for i in ids]
    a_k = [_dot(lhs[i], rhs_k[i], _NT) for i in ids]
    pw = [jnp.where(strict_u, a_b[i][:L], 0.0) for i in ids]
    a_ak = [jnp.where(strict_u, a_k[i][:L], 0.0).astype(BF16) for i in ids]
    a_rb = [jnp.where(incl_u, a_b[i][L:], 0.0).astype(BF16) for i in ids]
    a_rk = [jnp.where(incl_u, a_k[i][L:], 0.0).astype(BF16) for i in ids]

    def blockdiag(m_u):
        return jnp.where(same, rep(m_u), 0.0).astype(BF16)

    t_u = [eye_u + pw[i] for i in ids]
    pw_bd = [blockdiag(pw[i]) for i in ids]
    for _ in range(n_sq):
        pw = [_dot(pw[i].astype(BF16), pw_bd[i]) for i in ids]
        pw_bd = [blockdiag(pw[i]) for i in ids]
        t_u = [t_u[i] + _dot(t_u[i].astype(BF16), pw_bd[i]) for i in ids]
    t_u = [t_u[i].astype(BF16) for i in ids]
    v_s = [stack(v[i]).astype(BF16) for i in ids]

    s_cur = [s_ref[u] for u in units]
    y = [None] * len(chains)
    for ch in range(n_ch):
        cid = [ch * n_units + u for u in units]
        xr = [_dot(lhs[i], s_cur[u].astype(BF16), _NT) for u, i in zip(units, cid)]
        x = [xr[u][:L] + _dot(a_ak[i], v_s[i]) for u, i in zip(units, cid)]
        uu = [_dot(t_u[i], stack(x[u]).astype(BF16)) for u, i in zip(units, cid)]
        for u, i in zip(units, cid):
            y[i] = xr[u][L:] + _dot(a_rb[i], stack(uu[u]).astype(BF16)) + _dot(a_rk[i], v_s[i])
            upd = _dot(jnp.concatenate([uu[u], v[i]], axis=0).astype(BF16), upd_rhs[i], _TN)
            s_cur[u] = jnp.where(bd_state, s_cur[u] * e_last[i] + upd, 0.0)
    for u in units:
        s_ref[u] = s_cur[u]

    inv_n = 1.0 / HEAD_DIM
    y = [jnp.concatenate([y[ch * n_units + u] for ch in range(n_ch)], axis=0) for u in units]
    d = [y[u] - headsum(y[u]) * inv_n for u in units]
    var = [headsum(d[u] * d[u]) * inv_n for u in units]
    bonus = [headsum(lanes(r_all, u) * lanes(k_all, u) * lanes(rk_ref[...], u)) * lanes(v_all, u) for u in units]
    for u in units:
        sl = slice(u * UNIT, (u + 1) * UNIT)
        yn = d[u] * lax.rsqrt(var[u] + GN_EPS) * lnw_ref[:, sl] + lnb_ref[:, sl]
        g = g_refs[u][0]
        o_ref[0, :, sl] = ((yn + bonus[u]) * (g * _sigmoid(g))).astype(o_ref.dtype)


def _rwkv_scan(pa3, off_r, off_lora, off_g, mu, w0, wdu, a0, wiu, k_k, k_a, r_k, ln_w, ln_b):
    b, t, _ = pa3.shape
    c = w0.shape[0]
    L = RWKV_CHUNK * min(RWKV_CHUNKS_PER_STEP, t // RWKV_CHUNK)
    n_units = min(RWKV_UNITS, c // UNIT)
    w = n_units * UNIT
    assert t % L == 0 and c % w == 0 and off_r % w == 0 and off_g % UNIT == 0 and off_lora % WIDE == 0
    rb, gb, lb, ng = off_r // w, off_g // UNIT, off_lora // WIDE, c // w

    def col(base):
        return pl.BlockSpec((1, L, w), lambda bi, gi, ci: (bi, ci, base + gi))

    def vec(base=0):
        return pl.BlockSpec((1, w), lambda bi, gi, ci: (0, base + gi))

    gates = [pl.BlockSpec((1, L, UNIT), lambda bi, gi, ci, u=u: (bi, ci, gb + gi * n_units + u))
             for u in range(n_units)]
    lora_w = pl.BlockSpec((wdu.shape[0], w), lambda bi, gi, ci: (0, gi))
    mu_rkv = mu[:3 * c].reshape(1, 3 * c)
    mu_lora = mu[3 * c:].reshape(1, 2 * LANES)
    row = lambda z: z.reshape(1, c)
    return pl.pallas_call(
        functools.partial(_rwkv_scan_kernel, n_units=n_units, L=RWKV_CHUNK),
        grid=(b, ng, t // L),
        in_specs=[col(rb), col(rb + ng), col(rb + 2 * ng),
                  pl.BlockSpec((1, L, WIDE), lambda bi, gi, ci: (bi, ci, lb))] + gates
                 + [vec(0), vec(ng), vec(2 * ng), pl.BlockSpec((1, 2 * LANES), lambda bi, gi, ci: (0, 0)),
                    vec(), lora_w, vec(), lora_w, vec(), vec(), vec(), vec(), vec()],
        out_specs=pl.BlockSpec((1, L, w), lambda bi, gi, ci: (bi, ci, gi)),
        out_shape=jax.ShapeDtypeStruct((b, t, c), BF16),
        scratch_shapes=[pltpu.VMEM((n_units, UNIT, UNIT), F32)] + [pltpu.VMEM((1, w), F32)] * 3
                       + [pltpu.VMEM((1, 2 * LANES), F32)],
        compiler_params=_params("parallel", "parallel", "arbitrary"),
        name="rwkv_scan",
    )(pa3, pa3, pa3, pa3, *([pa3] * n_units), mu_rkv, mu_rkv, mu_rkv, mu_lora,
      row(w0), wdu, row(a0), wiu, row(k_k), row(k_a), row(r_k), row(ln_w), row(ln_b))


def _fox_prefix_kernel(f_ref, bf_ref, k_ref, k0_ref, k1_ref, carry_ref):
    @pl.when(pl.program_id(1) == 0)
    def _():
        carry_ref[...] = jnp.zeros_like(carry_ref)

    x = f_ref[0][:, 2 * LANES:3 * LANES] + bf_ref[...]
    log_f = -_softplus(-x)
    tc = x.shape[0]
    tril = jnp.where(lax.broadcasted_iota(jnp.int32, (tc, tc), 0) >= lax.broadcasted_iota(jnp.int32, (tc, tc), 1),
                     1.0, 0.0).astype(BF16)
    c = carry_ref[...]
    for part in _split_bf16(log_f, 3):
        c = c + _dot(tril, part)
    carry_ref[...] = c[tc - 1:tc, :]

    fw = k_ref.shape[2]
    src = lax.broadcasted_iota(jnp.int32, (LANES, fw), 0)
    dst = lax.broadcasted_iota(jnp.int32, (LANES, fw), 1)
    lane = dst % LANES
    head = 2 * (dst // LANES) + jnp.where(lane < HEAD_DIM, 1, 0)
    bias = None
    for i, part in enumerate(_split_bf16(c * LOG2E, FOX_BIAS_PARTS)):
        sel = jnp.where((src == head) & (lane % HEAD_DIM == i), 1.0, 0.0).astype(BF16)
        d = _dot(part, sel)
        bias = d if bias is None else bias + d
    bias = bias.astype(BF16)
    k = k_ref[0]
    first = (lax.broadcasted_iota(jnp.int32, (tc, fw), 1) % LANES) < HEAD_DIM
    k0_ref[0] = jnp.where(first, k, bias)
    k1_ref[0] = jnp.where(first, bias, k)


def _fox_prefix(pt3, off_f, b_f, pb3, off_k, fw):
    b, t, _ = pt3.shape
    fh = b_f.shape[0]
    tc = min(ROWS_PREFIX, t)
    assert t % tc == 0 and fh <= LANES and fw == fh * HEAD_DIM and off_f % WIDE == 0 and off_k % fw == 0
    bf = jnp.zeros((1, LANES), F32).at[0, :fh].set(b_f)
    fb, kb = off_f // WIDE, off_k // fw
    out = pl.BlockSpec((1, tc, fw), lambda bi, i: (bi, i, 0))
    return pl.pallas_call(
        _fox_prefix_kernel,
        grid=(b, t // tc),
        in_specs=[pl.BlockSpec((1, tc, WIDE), lambda bi, i: (bi, i, fb)),
                  pl.BlockSpec((1, LANES), lambda bi, i: (0, 0)),
                  pl.BlockSpec((1, tc, fw), lambda bi, i: (bi, i, kb))],
        out_specs=[out, out],
        out_shape=[jax.ShapeDtypeStruct((b, t, fw), BF16)] * 2,
        scratch_shapes=[pltpu.VMEM((1, LANES), F32)],
        compiler_params=_params("parallel", "arbitrary"),
        name="fox_prefix",
    )(pt3, bf, pb3)


def _fox_attn_kernel(q_ref, k0_ref, k1_ref, v_ref, g_ref, o_ref, m_ref, acc_ref, *, tk):
    qi = pl.program_id(2)
    tq = q_ref.shape[1]
    n_pairs = q_ref.shape[2] // LANES
    n_sub = tq // tk
    q_lane = lax.broadcasted_iota(jnp.int32, (tq, LANES), 1)
    q_head0 = q_lane < HEAD_DIM
    bias0 = jnp.where((q_lane >= HEAD_DIM) & (q_lane < HEAD_DIM + FOX_BIAS_PARTS), -1.0, 0.0)
    bias1 = jnp.where(q_lane < FOX_BIAS_PARTS, -1.0, 0.0)
    k_refs = (k0_ref, k1_ref)
    chains = [(pr, h) for pr in range(n_pairs) for h in range(2)]

    qs = []
    for pr in range(n_pairs):
        q = q_ref[0, :, pr * LANES:(pr + 1) * LANES].astype(F32) * (HEAD_DIM ** -0.5 * LOG2E)
        qs += [jnp.where(q_head0, q, bias0).astype(BF16), jnp.where(q_head0, bias1, q).astype(BF16)]

    m_ref[...] = jnp.full_like(m_ref, -jnp.inf)
    acc_ref[...] = jnp.zeros_like(acc_ref)

    def tile(j, q0, causal):
        start = pl.multiple_of(j * tk, tk)
        cols = slice(q0, tq)
        if causal:
            key = lax.broadcasted_iota(jnp.int32, (tk, tq - q0), 0)
            qry = lax.broadcasted_iota(jnp.int32, (tk, tq - q0), 1)
            valid = key <= qry
        scores = []
        for ci, (pr, h) in enumerate(chains):
            s = _dot(k_refs[h][0, pl.ds(start, tk), pr * LANES:(pr + 1) * LANES], qs[ci][q0:], _NT)
            scores.append(jnp.where(valid, s, -jnp.inf) if causal else s)
        for ci, s in enumerate(scores):
            m = m_ref[ci, :, cols]
            m_new = jnp.maximum(m, jnp.max(s, axis=0, keepdims=True))
            m_ref[ci, :, cols] = m_new
            p = jnp.exp2(s - m_new).astype(BF16)
            pv = _dot(v_ref[ci * FOX_V_ROWS:(ci + 1) * FOX_V_ROWS, pl.ds(start, tk)], p)
            acc_ref[ci, :, cols] = jnp.exp2(m - m_new) * acc_ref[ci, :, cols] + pv

    def body(j, carry):
        tile(j, 0, False)
        return carry

    lax.fori_loop(0, qi * n_sub, body, 0)
    for d in range(n_sub):
        tile(qi * n_sub + d, d * tk, True)

    for pr in range(n_pairs):
        sl = slice(pr * LANES, (pr + 1) * LANES)
        out_t = jnp.concatenate(
            [acc_ref[ci][:HEAD_DIM] * (1.0 / acc_ref[ci][HEAD_DIM:HEAD_DIM + 1]) for ci in (2 * pr, 2 * pr + 1)], axis=0)
        g = g_ref[0, :, sl]
        o_ref[0, :, sl] = (out_t.T * (g * _sigmoid(g))).astype(o_ref.dtype)


def _fox_attn(pb3, off_q, k0, k1, vt, pt3, off_g, fw):
    b, t, _ = pb3.shape
    tq = min(FOX_TQ, t)
    tk = min(FOX_TK, tq)
    n_pairs = min(FOX_PAIRS, fw // LANES)
    w = n_pairs * LANES
    v_rows = 2 * n_pairs * FOX_V_ROWS
    assert t % tq == 0 and tq % tk == 0 and fw % w == 0 and off_q % w == 0 and off_g % w == 0
    qb, gb = off_q // w, off_g // w
    keys = pl.BlockSpec((1, t, w), lambda bi, p, i: (bi, 0, p))
    return pl.pallas_call(
        functools.partial(_fox_attn_kernel, tk=tk),
        grid=(b, fw // w, t // tq),
        in_specs=[pl.BlockSpec((1, tq, w), lambda bi, p, i: (bi, i, qb + p)), keys, keys,
                  pl.BlockSpec((v_rows, t), lambda bi, p, i: (p, bi)),
                  pl.BlockSpec((1, tq, w), lambda bi, p, i: (bi, i, gb + p))],
        out_specs=pl.BlockSpec((1, tq, w), lambda bi, p, i: (bi, i, p)),
        out_shape=jax.ShapeDtypeStruct((b, t, fw), BF16),
        scratch_shapes=[pltpu.VMEM((2 * n_pairs, 1, tq), F32), pltpu.VMEM((2 * n_pairs, FOX_V_ROWS, tq), F32)],
        compiler_params=_params("parallel", "parallel", "arbitrary"),
        name="fox_attn",
    )(pb3, k0, k1, vt, pt3)


def _mem_attn_kernel(q_ref, kv_ref, g_ref, o_ref):
    mw = q_ref.shape[2]
    hd = mw // MEM_HEADS
    scale = hd ** -0.5
    outs = []
    for h in range(MEM_HEADS):
        q = q_ref[0, :, h * hd:(h + 1) * hd]
        mk = kv_ref[0, :, h * hd:(h + 1) * hd]
        mv = kv_ref[0, :, mw + h * hd:mw + (h + 1) * hd]
        s = _dot(q, mk, _NT) * scale
        p = jnp.exp(s - jnp.max(s, axis=1, keepdims=True))
        l = jnp.sum(p, axis=1, keepdims=True)
        outs.append(_dot(p.astype(BF16), mv) / l)
    g = g_ref[0]
    o_ref[0] = (jnp.concatenate(outs, axis=1) * (g * _sigmoid(g))).astype(o_ref.dtype)


def _mem_attn(pq3, off_q, mkv3, pt3, off_g, mw):
    b, t, _ = pq3.shape
    tm = min(ROWS_MEM, t)
    assert t % tm == 0 and off_q % mw == 0 and off_g % mw == 0 and (mw // MEM_HEADS) % LANES == 0
    n_mem = mkv3.shape[1]
    qb, gb = off_q // mw, off_g // mw
    return pl.pallas_call(
        _mem_attn_kernel,
        grid=(b, t // tm),
        in_specs=[pl.BlockSpec((1, tm, mw), lambda bi, i: (bi, i, qb)),
                  pl.BlockSpec((1, n_mem, 2 * mw), lambda bi, i: (bi, 0, 0)),
                  pl.BlockSpec((1, tm, mw), lambda bi, i: (bi, i, gb))],
        out_specs=pl.BlockSpec((1, tm, mw), lambda bi, i: (bi, i, 0)),
        out_shape=jax.ShapeDtypeStruct((b, t, mw), BF16),
        compiler_params=_params("parallel", "parallel"),
        name="mem_attn",
    )(pq3, mkv3, pt3)


def _layer(x, mem, g_pre, w_in, mu_rwkv, w0, w_decay_up, a0, w_iclr_up, k_k, k_a, r_k,
           ln_x_w, ln_x_b, b_f, g_mem, w_mem_kv, w_out, g_post):
    b, t, d = x.shape
    n_mem = mem.shape[1]
    c = w0.shape[0]
    fh = b_f.shape[0]
    fw = fh * HEAD_DIM
    mw = w_mem_kv.shape[1] // 2
    lora = w_decay_up.shape[0]
    assert lora == LANES and w_iclr_up.shape[0] == LANES and fh <= WIDE
    assert w_in.shape[1] == 4 * c + 2 * lora + 4 * fw + fh + 2 * mw
    assert c % WIDE == 0 and fw % WIDE == 0 and mw % WIDE == 0

    o_g = 3 * c + 2 * lora
    o_fq = o_g + c
    o_fv = o_fq + 2 * fw
    o_f = o_fv + fw
    o_gfox = o_f + fh
    o_mq = o_gfox + fw
    o_gmq = o_mq + mw

    x2 = x.reshape(b * t, d)
    h = _rmsnorm(x2, g_pre, BF16)

    lora_f = WIDE - 2 * lora
    assert o_f % LANES == 0 and o_f + lora_f <= w_in.shape[1]
    pieces_a = ((o_g, c), (0, 3 * c), (o_gmq, mw), (o_gfox, fw), (3 * c, 2 * lora), (o_f, lora_f))
    pieces_b = ((o_fq, 2 * fw), (o_mq, mw))
    w_a, w_b, w_vt = _weight_prep(w_in.T, pieces_a, pieces_b, (o_fv, fw))
    off_a = {"g_rwkv": 0, "r": c, "g_mq": 4 * c, "g_fox": 4 * c + mw, "lora": 4 * c + mw + fw}
    off_b = {"fq": 0, "fk": fw, "mq": 2 * fw}

    pa3 = _matmul(h, w_a, F32, 2 * WIDE, "in_proj_a").reshape(b, t, -1)
    pb3 = _matmul(h, w_b, BF16, 2 * WIDE, "in_proj_b").reshape(b, t, -1)
    vt = _proj_t(w_vt, h)

    y_rwkv = _rwkv_scan(pa3, off_a["r"], off_a["lora"], off_a["g_rwkv"], mu_rwkv, w0, w_decay_up, a0, w_iclr_up,
                        k_k, k_a, r_k.reshape(-1), ln_x_w, ln_x_b)

    k0, k1 = _fox_prefix(pa3, off_a["lora"], b_f, pb3, off_b["fk"], fw)
    y_fox = _fox_attn(pb3, off_b["fq"], k0, k1, vt, pa3, off_a["g_fox"], fw)

    hm = _rmsnorm(mem.reshape(b * n_mem, d), g_mem, BF16)
    mkv3 = _proj([hm], [(w_mem_kv, (d, WIDE), lambda j: (0, j), (0, 0))], 2 * mw, WIDE, BF16,
                 "mem_kv_proj").reshape(b, n_mem, 2 * mw)
    y_mem = _mem_attn(pb3, off_b["mq"], mkv3, pa3, off_a["g_mq"], mw)

    ys = [y_rwkv.reshape(b * t, c), y_fox.reshape(b * t, fw), y_mem.reshape(b * t, mw)]
    return _out_proj(ys, _cast(w_out, BF16, "w_out_cast"), x2, g_post).reshape(b, t, d)


def kernel(x, mem, g_pre, w_in, mu_rwkv, w0, w_decay_up, a0, w_iclr_up, k_k, k_a, r_k, ln_x_w, ln_x_b, b_f,
           g_mem, w_mem_kv, w_out, g_post):
    for l in range(g_pre.shape[0]):
        x = _layer(x, mem, g_pre[l], w_in[l], mu_rwkv[l], w0[l], w_decay_up[l], a0[l], w_iclr_up[l], k_k[l],
                   k_a[l], r_k[l], ln_x_w[l], ln_x_b[l], b_f[l], g_mem[l], w_mem_kv[l], w_out[l], g_post[l])
    return x
```

```python
import functools

import jax
import jax.numpy as jnp
from jax import lax
from jax.experimental import pallas as pl
from jax.experimental.pallas import tpu as pltpu

HEAD_DIM = 64
MEM_HEADS = 4
RMS_EPS = 1e-6
GN_EPS = 64e-5
LOG2E = 1.4426950408889634

LANES = 128
MXU = 256
UNIT_HEADS = MXU // HEAD_DIM
UNIT = UNIT_HEADS * HEAD_DIM
RWKV_CHUNK = 64
WIDE = 512
VMEM_LIMIT_BYTES = 56 * 1024 * 1024
FOX_BIAS_PARTS = 3
FOX_V_ROWS = HEAD_DIM + 16

ROWS_ELEMENTWISE = 512
ROWS_PROJ = 1024
ROWS_PROJ_T = 1024
ROWS_OUT = 512
COLS_WEIGHT_PREP = 256
RWKV_CHUNKS_PER_STEP = 4
ROWS_PREFIX = 512
ROWS_MEM = 512
FOX_TQ = 512
FOX_TK = 256
FOX_PAIRS = 4
RWKV_UNITS = 6

F32 = jnp.float32
BF16 = jnp.bfloat16

_NN = (((1,), (0,)), ((), ()))
_NT = (((1,), (1,)), ((), ()))
_TN = (((0,), (0,)), ((), ()))


def _dot(a, b, dims=_NN):
    return lax.dot_general(a, b, dims, preferred_element_type=F32)


def _split_bf16(x, n):
    parts = []
    rem = x
    for _ in range(n):
        p = rem.astype(BF16)
        parts.append(p)
        rem = rem - p.astype(F32)
    return parts


def _dot_f32(a, b):
    a_hi, a_lo = _split_bf16(a, 2)
    b_hi, b_lo = _split_bf16(b, 2)
    return _dot(a_hi, b_hi) + (_dot(a_lo, b_hi) + _dot(a_hi, b_lo))


def _sigmoid(x):
    return 1.0 / (1.0 + jnp.exp(-x))


def _softplus(x):
    return jnp.maximum(x, 0.0) + jnp.log(1.0 + jnp.exp(-jnp.abs(x)))


def _params(*semantics):
    return pltpu.CompilerParams(dimension_semantics=semantics, vmem_limit_bytes=VMEM_LIMIT_BYTES)


def _rmsnorm_kernel(x_ref, g_ref, o_ref):
    x = x_ref[...]
    ms = jnp.mean(x * x, axis=-1, keepdims=True)
    o_ref[...] = (x * lax.rsqrt(ms + RMS_EPS) * g_ref[...]).astype(o_ref.dtype)


def _rmsnorm(x2d, g, out_dtype):
    m, d = x2d.shape
    tm = min(ROWS_ELEMENTWISE, m)
    assert m % tm == 0
    return pl.pallas_call(
        _rmsnorm_kernel,
        grid=(m // tm,),
        in_specs=[pl.BlockSpec((tm, d), lambda i: (i, 0)), pl.BlockSpec((1, d), lambda i: (0, 0))],
        out_specs=pl.BlockSpec((tm, d), lambda i: (i, 0)),
        out_shape=jax.ShapeDtypeStruct((m, d), out_dtype),
        compiler_params=_params("parallel"),
        name="rmsnorm",
    )(x2d, g.reshape(1, d))


def _proj_kernel(*refs, n_x, placement):
    xs = refs[:n_x]
    ws = refs[n_x:n_x + len(placement)]
    o_ref, wb_ref = refs[n_x + len(placement):]

    @pl.when(pl.program_id(1) == 0)
    def _():
        for w_ref, (r0, c0) in zip(ws, placement):
            wb_ref[r0:r0 + w_ref.shape[0], c0:c0 + w_ref.shape[1]] = w_ref[...].astype(BF16)

    acc = None
    r0 = 0
    for x_ref in xs:
        kx = x_ref.shape[1]
        d = _dot(x_ref[...], wb_ref[r0:r0 + kx, :])
        acc = d if acc is None else acc + d
        r0 += kx
    o_ref[...] = acc.astype(o_ref.dtype)


def _proj(xs, weights, n_cols, tn, out_dtype, name):
    m = xs[0].shape[0]
    k_total = sum(x.shape[1] for x in xs)
    tm = min(ROWS_PROJ, m)
    tn = min(tn, n_cols)
    assert m % tm == 0 and n_cols % tn == 0
    in_specs = [pl.BlockSpec((tm, x.shape[1]), lambda j, i: (i, 0)) for x in xs]
    for _, shape, index_fn, _ in weights:
        in_specs.append(pl.BlockSpec(shape, lambda j, i, f=index_fn: f(j)))
    return pl.pallas_call(
        functools.partial(_proj_kernel, n_x=len(xs), placement=tuple(p for _, _, _, p in weights)),
        grid=(n_cols // tn, m // tm),
        in_specs=in_specs,
        out_specs=pl.BlockSpec((tm, tn), lambda j, i: (i, j)),
        out_shape=jax.ShapeDtypeStruct((m, n_cols), out_dtype),
        scratch_shapes=[pltpu.VMEM((k_total, tn), BF16)],
        compiler_params=_params("parallel", "arbitrary"),
        name=name,
    )(*xs, *[w for w, _, _, _ in weights])


def _matmul_kernel(x_ref, w_ref, o_ref):
    o_ref[...] = _dot(x_ref[...], w_ref[...]).astype(o_ref.dtype)


def _matmul(x, w, out_dtype, tn, name):
    m, k = x.shape
    n = w.shape[1]
    tm = min(ROWS_PROJ, m)
    tn = min(tn, n)
    assert m % tm == 0 and n % tn == 0
    return pl.pallas_call(
        _matmul_kernel,
        grid=(n // tn, m // tm),
        in_specs=[pl.BlockSpec((tm, k), lambda j, i: (i, 0)), pl.BlockSpec((k, tn), lambda j, i: (0, j))],
        out_specs=pl.BlockSpec((tm, tn), lambda j, i: (i, j)),
        out_shape=jax.ShapeDtypeStruct((m, n), out_dtype),
        compiler_params=_params("parallel", "parallel"),
        name=name,
    )(x, w)


def _weight_prep_kernel(wt_ref, wa_ref, wb_ref, wvt_ref, *, pieces_a, pieces_b, piece_vt):
    def gather(pieces, o_ref):
        dst = 0
        for src, width in pieces:
            o_ref[:, dst:dst + width] = wt_ref[src:src + width, :].T.astype(o_ref.dtype)
            dst += width

    gather(pieces_a, wa_ref)
    gather(pieces_b, wb_ref)
    src, width = piece_vt
    wvt_ref[...] = wt_ref[src:src + width, :].astype(wvt_ref.dtype)


def _weight_prep(w_in_t, pieces_a, pieces_b, piece_vt):
    n, k = w_in_t.shape
    tc = min(COLS_WEIGHT_PREP, k)
    assert k % tc == 0 and all(s % 8 == 0 and wd % 16 == 0 for s, wd in pieces_a + pieces_b + (piece_vt,))
    na = sum(wd for _, wd in pieces_a)
    nb = sum(wd for _, wd in pieces_b)
    return pl.pallas_call(
        functools.partial(_weight_prep_kernel, pieces_a=pieces_a, pieces_b=pieces_b, piece_vt=piece_vt),
        grid=(k // tc,),
        in_specs=[pl.BlockSpec((n, tc), lambda i: (0, i))],
        out_specs=[pl.BlockSpec((tc, na), lambda i: (i, 0)), pl.BlockSpec((tc, nb), lambda i: (i, 0)),
                   pl.BlockSpec((piece_vt[1], tc), lambda i: (0, i))],
        out_shape=[jax.ShapeDtypeStruct((k, na), BF16), jax.ShapeDtypeStruct((k, nb), BF16),
                   jax.ShapeDtypeStruct((piece_vt[1], k), BF16)],
        compiler_params=_params("parallel"),
        name="weight_prep",
    )(w_in_t)


def _proj_t_kernel(wt_ref, x_ref, v_ref):
    vt = _dot(wt_ref[...], x_ref[...], _NT)
    ones = jnp.ones((FOX_V_ROWS - HEAD_DIM, vt.shape[1]), v_ref.dtype)
    for h in range(vt.shape[0] // HEAD_DIM):
        v_ref[h * FOX_V_ROWS:h * FOX_V_ROWS + HEAD_DIM, :] = vt[h * HEAD_DIM:(h + 1) * HEAD_DIM].astype(v_ref.dtype)
        v_ref[h * FOX_V_ROWS + HEAD_DIM:(h + 1) * FOX_V_ROWS, :] = ones


def _proj_t(wt, x):
    n, k = wt.shape
    m = x.shape[0]
    tm = min(ROWS_PROJ_T, m)
    assert m % tm == 0 and n % HEAD_DIM == 0
    n_aug = n // HEAD_DIM * FOX_V_ROWS
    return pl.pallas_call(
        _proj_t_kernel,
        grid=(m // tm,),
        in_specs=[pl.BlockSpec((n, k), lambda i: (0, 0)), pl.BlockSpec((tm, k), lambda i: (i, 0))],
        out_specs=pl.BlockSpec((n_aug, tm), lambda i: (0, i)),
        out_shape=jax.ShapeDtypeStruct((n_aug, m), BF16),
        compiler_params=_params("parallel"),
        name="in_proj_vt",
    )(wt, x)


def _cast_kernel(x_ref, o_ref):
    o_ref[...] = x_ref[...].astype(o_ref.dtype)


def _cast(x, dtype, name):
    m, n = x.shape
    tm = min(ROWS_OUT, m)
    assert m % tm == 0
    return pl.pallas_call(
        _cast_kernel,
        grid=(m // tm,),
        in_specs=[pl.BlockSpec((tm, n), lambda i: (i, 0))],
        out_specs=pl.BlockSpec((tm, n), lambda i: (i, 0)),
        out_shape=jax.ShapeDtypeStruct((m, n), dtype),
        compiler_params=_params("parallel"),
        name=name,
    )(x)


def _out_proj_kernel(*refs, n_y):
    ys = refs[:n_y]
    w_ref, x_ref, g_ref, o_ref, ss_ref = refs[n_y:]
    j = pl.program_id(1)
    tn = w_ref.shape[1]
    d = o_ref.shape[1]

    acc = None
    r0 = 0
    for y_ref in ys:
        ky = y_ref.shape[1]
        part = _dot(y_ref[...], w_ref[r0:r0 + ky, :])
        acc = part if acc is None else acc + part
        r0 += ky

    @pl.when(j == 0)
    def _():
        ss_ref[...] = jnp.zeros_like(ss_ref)

    ss_ref[...] += jnp.sum(acc * acc, axis=1, keepdims=True)
    o_ref[:, pl.ds(pl.multiple_of(j * tn, tn), tn)] = acc

    @pl.when(j == pl.num_programs(1) - 1)
    def _():
        scale = lax.rsqrt(ss_ref[...] * (1.0 / d) + RMS_EPS)
        for c0 in range(0, d, tn):
            sl = slice(c0, c0 + tn)
            o_ref[:, sl] = x_ref[:, sl] + o_ref[:, sl] * scale * g_ref[:, sl]


def _out_proj(ys, w_bf16, x2d, g):
    m, d = x2d.shape
    tm = min(ROWS_OUT, m)
    tn = min(WIDE, d)
    assert m % tm == 0 and d % tn == 0 and sum(y.shape[1] for y in ys) == w_bf16.shape[0]
    row = pl.BlockSpec((tm, d), lambda i, j: (i, 0))
    return pl.pallas_call(
        functools.partial(_out_proj_kernel, n_y=len(ys)),
        grid=(m // tm, d // tn),
        in_specs=[pl.BlockSpec((tm, y.shape[1]), lambda i, j: (i, 0)) for y in ys]
                 + [pl.BlockSpec((w_bf16.shape[0], tn), lambda i, j: (0, j)), row,
                    pl.BlockSpec((1, d), lambda i, j: (0, 0))],
        out_specs=row,
        out_shape=jax.ShapeDtypeStruct((m, d), x2d.dtype),
        scratch_shapes=[pltpu.VMEM((tm, 1), F32)],
        compiler_params=_params("parallel", "arbitrary"),
        name="out_proj_post",
    )(*ys, w_bf16, x2d, g.reshape(1, d))


def _rwkv_scan_kernel(*refs, n_units, L):
    r_ref, k_ref, v_ref, lora_ref = refs[:4]
    g_refs = refs[4:4 + n_units]
    (mu_r_ref, mu_k_ref, mu_v_ref, mu_lora_ref, w0_ref, wdu_ref, a0_ref, wiu_ref, kk_ref, ka_ref,
     rk_ref, lnw_ref, lnb_ref, o_ref, s_ref, carry_r, carry_k, carry_v, carry_lora) = refs[4 + n_units:]

    @pl.when(pl.program_id(2) == 0)
    def _():
        for z in (s_ref, carry_r, carry_k, carry_v, carry_lora):
            z[...] = jnp.zeros_like(z)

    rows = r_ref.shape[1]
    n_ch = rows // L
    first_row = lax.broadcasted_iota(jnp.int32, (rows, 1), 0) == 0

    def shift(x, carry_ref, mu):
        prev = jnp.where(first_row, carry_ref[...], pltpu.roll(x, 1, 0))
        carry_ref[...] = x[rows - 1:rows, :]
        return x + (prev - x) * mu

    r_all = shift(r_ref[0], carry_r, mu_r_ref[...])
    k_raw = shift(k_ref[0], carry_k, mu_k_ref[...])
    v_all = shift(v_ref[0], carry_v, mu_v_ref[...])
    lora = shift(lora_ref[0][:, :2 * LANES], carry_lora, mu_lora_ref[...])
    w_pre = -_softplus(-(w0_ref[...] + _dot_f32(jnp.tanh(lora[:, :LANES]), wdu_ref[...]))) - 0.5
    lw_all = -jnp.exp(w_pre)
    alpha = _sigmoid(a0_ref[...] + _dot_f32(lora[:, LANES:], wiu_ref[...]))
    k_all = k_raw * (1.0 + (alpha - 1.0) * ka_ref[...])
    kk_all = k_raw * kk_ref[...]
    hl = UNIT_HEADS * L
    n_sq = L.bit_length() - 2
    assert 2 ** (n_sq + 1) == L
    units = range(n_units)

    def blk(shape, d0, d1):
        return (lax.broadcasted_iota(jnp.int32, shape, 0) // d0) == (lax.broadcasted_iota(jnp.int32, shape, 1) // d1)

    same = blk((hl, hl), L, L)
    u_row = lax.broadcasted_iota(jnp.int32, (L, hl), 0)
    u_col = lax.broadcasted_iota(jnp.int32, (L, hl), 1) % L
    strict_u = u_col < u_row
    incl_u = u_col <= u_row
    eye_u = jnp.where(u_col == u_row, 1.0, 0.0)
    bd_rows = blk((hl, UNIT), L, HEAD_DIM)
    bd_state = blk((UNIT, UNIT), HEAD_DIM, HEAD_DIM)
    ones_bd = jnp.where(bd_state, 1.0, 0.0).astype(BF16)
    c_row = lax.broadcasted_iota(jnp.int32, (rows, rows), 0)
    c_col = lax.broadcasted_iota(jnp.int32, (rows, rows), 1)
    tril = jnp.where((c_row // L == c_col // L) & (c_row >= c_col), 1.0, 0.0).astype(BF16)

    def rep(x):
        return jnp.concatenate([x] * UNIT_HEADS, axis=0)

    def stack(x):
        return jnp.where(bd_rows, rep(x), 0.0)

    def headsum(x):
        return _dot(x.astype(BF16), ones_bd)

    def lanes(x, u):
        return x[:, u * UNIT:(u + 1) * UNIT]

    chains = [(ch, u) for ch in range(n_ch) for u in units]
    lw_hi, lw_lo = _split_bf16(lw_all, 2)
    cl_all = _dot(tril, lw_hi) + _dot(tril, lw_lo)
    kk_norm = [headsum(lanes(kk_all, u) * lanes(kk_all, u)) for u in units]
    kk_unit = [lanes(kk_all, u) * lax.rsqrt(jnp.maximum(kk_norm[u], 1e-24)) for u in units]

    ah, rh, e_last, lhs, rhs_b, rhs_k, upd_rhs, v = [], [], [], [], [], [], [], []
    for ch, u in chains:
        rs = slice(ch * L, (ch + 1) * L)
        sl = slice(u * UNIT, (u + 1) * UNIT)
        cl = cl_all[rs, sl]
        e_pos = jnp.exp(cl)
        e_neg = jnp.exp(-cl)
        e_rem = e_pos[L - 1:L, :] * e_neg
        kk = kk_unit[u][rs]
        a, b, k = -kk, kk * alpha[rs, sl], k_all[rs, sl]
        ah.append(a * jnp.exp(cl - lw_all[rs, sl]))
        rh.append(r_all[rs, sl] * e_pos)
        e_last.append(e_pos[L - 1:L, :])
        lhs.append(jnp.concatenate([ah[-1], rh[-1]], axis=0).astype(BF16))
        rhs_b.append(stack(b * e_neg).astype(BF16))
        rhs_k.append(stack(k * e_neg).astype(BF16))
        upd_rhs.append(jnp.concatenate([b * e_rem, k * e_rem], axis=0).astype(BF16))
        v.append(v_all[rs, sl])

    ids = range(len(chains))
    a_b = [_dot(lhs[i], rhs_b[i], _NT) for i in ids]
    a_k = [_dot(lhs[i], rhs_k[i], _NT) for i in ids]
    pw = [jnp.where(strict_u, a_b[i][:L], 0.0) for i in ids]
    a_ak = [jnp.where(strict_u, a_k[i][:L], 0.0).astype(BF16) for i in ids]
    a_rb = [jnp.where(incl_u, a_b[i][L:], 0.0).astype(BF16) for i in ids]
    a_rk = [jnp.where(incl_u, a_k[i][L:], 0.0).astype(BF16) for i in ids]

    def blockdiag(m_u):
        return jnp.where(same, rep(m_u), 0.0).astype(BF16)

    t_u = [eye_u + pw[i] for i in ids]
    pw_bd = [blockdiag(pw[i]) for i in ids]
    for _ in range(n_sq):
        pw = [_dot(pw[i].astype(BF16), pw_bd[i]) for i in ids]
        pw_bd = [blockdiag(pw[i]) for i in ids]
        t_u = [t_u[i] + _dot(t_u[i].astype(BF16), pw_bd[i]) for i in ids]
    t_u = [t_u[i].astype(BF16) for i in ids]
    v_s = [stack(v[i]).astype(BF16) for i in ids]

    s_cur = [s_ref[u] for u in units]
    y = [None] * len(chains)
    for ch in range(n_ch):
        cid = [ch * n_units + u for u in units]
        xr = [_dot(lhs[i], s_cur[u].astype(BF16), _NT) for u, i in zip(units, cid)]
        x = [xr[u][:L] + _dot(a_ak[i], v_s[i]) for u, i in zip(units, cid)]
        uu = [_dot(t_u[i], stack(x[u]).astype(BF16)) for u, i in zip(units, cid)]
        for u, i in zip(units, cid):
            y[i] = xr[u][L:] + _dot(a_rb[i], stack(uu[u]).astype(BF16)) + _dot(a_rk[i], v_s[i])
            upd = _dot(jnp.concatenate([uu[u], v[i]], axis=0).astype(BF16), upd_rhs[i], _TN)
            s_cur[u] = jnp.where(bd_state, s_cur[u] * e_last[i] + upd, 0.0)
    for u in units:
        s_ref[u] = s_cur[u]

    inv_n = 1.0 / HEAD_DIM
    y = [jnp.concatenate([y[ch * n_units + u] for ch in range(n_ch)], axis=0) for u in units]
    d = [y[u] - headsum(y[u]) * inv_n for u in units]
    var = [headsum(d[u] * d[u]) * inv_n for u in units]
    bonus = [headsum(lanes(r_all, u) * lanes(k_all, u) * lanes(rk_ref[...], u)) * lanes(v_all, u) for u in units]
    for u in units:
        sl = slice(u * UNIT, (u + 1) * UNIT)
        yn = d[u] * lax.rsqrt(var[u] + GN_EPS) * lnw_ref[:, sl] + lnb_ref[:, sl]
        g = g_refs[u][0]
        o_ref[0, :, sl] = ((yn + bonus[u]) * (g * _sigmoid(g))).astype(o_ref.dtype)


def _rwkv_scan(pa3, off_r, off_lora, off_g, mu, w0, wdu, a0, wiu, k_k, k_a, r_k, ln_w, ln_b):
    b, t, _ = pa3.shape
    c = w0.shape[0]
    L = RWKV_CHUNK * min(RWKV_CHUNKS_PER_STEP, t // RWKV_CHUNK)
    n_units = min(RWKV_UNITS, c // UNIT)
    w = n_units * UNIT
    assert t % L == 0 and c % w == 0 and off_r % w == 0 and off_g % UNIT == 0 and off_lora % WIDE == 0
    rb, gb, lb, ng = off_r // w, off_g // UNIT, off_lora // WIDE, c // w

    def col(base):
        return pl.BlockSpec((1, L, w), lambda bi, gi, ci: (bi, ci, base + gi))

    def vec(base=0):
        return pl.BlockSpec((1, w), lambda bi, gi, ci: (0, base + gi))

    gates = [pl.BlockSpec((1, L, UNIT), lambda bi, gi, ci, u=u: (bi, ci, gb + gi * n_units + u))
             for u in range(n_units)]
    lora_w = pl.BlockSpec((wdu.shape[0], w), lambda bi, gi, ci: (0, gi))
    mu_rkv = mu[:3 * c].reshape(1, 3 * c)
    mu_lora = mu[3 * c:].reshape(1, 2 * LANES)
    row = lambda z: z.reshape(1, c)
    return pl.pallas_call(
        functools.partial(_rwkv_scan_kernel, n_units=n_units, L=RWKV_CHUNK),
        grid=(b, ng, t // L),
        in_specs=[col(rb), col(rb + ng), col(rb + 2 * ng),
                  pl.BlockSpec((1, L, WIDE), lambda bi, gi, ci: (bi, ci, lb))] + gates
                 + [vec(0), vec(ng), vec(2 * ng), pl.BlockSpec((1, 2 * LANES), lambda bi, gi, ci: (0, 0)),
                    vec(), lora_w, vec(), lora_w, vec(), vec(), vec(), vec(), vec()],
        out_specs=pl.BlockSpec((1, L, w), lambda bi, gi, ci: (bi, ci, gi)),
        out_shape=jax.ShapeDtypeStruct((b, t, c), BF16),
        scratch_shapes=[pltpu.VMEM((n_units, UNIT, UNIT), F32)] + [pltpu.VMEM((1, w), F32)] * 3
                       + [pltpu.VMEM((1, 2 * LANES), F32)],
        compiler_params=_params("parallel", "parallel", "arbitrary"),
        name="rwkv_scan",
    )(pa3, pa3, pa3, pa3, *([pa3] * n_units), mu_rkv, mu_rkv, mu_rkv, mu_lora,
      row(w0), wdu, row(a0), wiu, row(k_k), row(k_a), row(r_k), row(ln_w), row(ln_b))


def _fox_prefix_kernel(f_ref, bf_ref, k_ref, k0_ref, k1_ref, carry_ref):
    @pl.when(pl.program_id(1) == 0)
    def _():
        carry_ref[...] = jnp.zeros_like(carry_ref)

    x = f_ref[0][:, 2 * LANES:3 * LANES] + bf_ref[...]
    log_f = -_softplus(-x)
    tc = x.shape[0]
    tril = jnp.where(lax.broadcasted_iota(jnp.int32, (tc, tc), 0) >= lax.broadcasted_iota(jnp.int32, (tc, tc), 1),
                     1.0, 0.0).astype(BF16)
    c = carry_ref[...]
    for part in _split_bf16(log_f, 3):
        c = c + _dot(tril, part)
    carry_ref[...] = c[tc - 1:tc, :]

    fw = k_ref.shape[2]
    src = lax.broadcasted_iota(jnp.int32, (FOX_BIAS_PARTS * LANES, fw), 0)
    dst = lax.broadcasted_iota(jnp.int32, (FOX_BIAS_PARTS * LANES, fw), 1)
    lane = dst % LANES
    head = 2 * (dst // LANES) + jnp.where(lane < HEAD_DIM, 1, 0)
    sel = jnp.where((src % LANES == head) & (lane % HEAD_DIM == src // LANES), 1.0, 0.0).astype(BF16)
    parts = jnp.concatenate(_split_bf16(c * LOG2E, FOX_BIAS_PARTS), axis=1)
    bias = _dot(parts, sel).astype(BF16)
    k = k_ref[0]
    first = (lax.broadcasted_iota(jnp.int32, (tc, fw), 1) % LANES) < HEAD_DIM
    k0_ref[0] = jnp.where(first, k, bias)
    k1_ref[0] = jnp.where(first, bias, k)


def _fox_prefix(pt3, off_f, b_f, pb3, off_k, fw):
    b, t, _ = pt3.shape
    fh = b_f.shape[0]
    tc = min(ROWS_PREFIX, t)
    assert t % tc == 0 and fh <= LANES and fw == fh * HEAD_DIM and off_f % WIDE == 0 and off_k % fw == 0
    bf = jnp.zeros((1, LANES), F32).at[0, :fh].set(b_f)
    fb, kb = off_f // WIDE, off_k // fw
    out = pl.BlockSpec((1, tc, fw), lambda bi, i: (bi, i, 0))
    return pl.pallas_call(
        _fox_prefix_kernel,
        grid=(b, t // tc),
        in_specs=[pl.BlockSpec((1, tc, WIDE), lambda bi, i: (bi, i, fb)),
                  pl.BlockSpec((1, LANES), lambda bi, i: (0, 0)),
                  pl.BlockSpec((1, tc, fw), lambda bi, i: (bi, i, kb))],
        out_specs=[out, out],
        out_shape=[jax.ShapeDtypeStruct((b, t, fw), BF16)] * 2,
        scratch_shapes=[pltpu.VMEM((1, LANES), F32)],
        compiler_params=_params("parallel", "arbitrary"),
        name="fox_prefix",
    )(pt3, bf, pb3)


def _fox_attn_kernel(q_ref, k0_ref, k1_ref, v_ref, g_ref, o_ref, m_ref, acc_ref, *, tk):
    qi = pl.program_id(2)
    tq = q_ref.shape[1]
    n_pairs = q_ref.shape[2] // LANES
    n_sub = tq // tk
    q_lane = lax.broadcasted_iota(jnp.int32, (tq, LANES), 1)
    q_head0 = q_lane < HEAD_DIM
    bias0 = jnp.where((q_lane >= HEAD_DIM) & (q_lane < HEAD_DIM + FOX_BIAS_PARTS), -1.0, 0.0)
    bias1 = jnp.where(q_lane < FOX_BIAS_PARTS, -1.0, 0.0)
    k_refs = (k0_ref, k1_ref)
    chains = [(pr, h) for pr in range(n_pairs) for h in range(2)]

    qs = []
    for pr in range(n_pairs):
        q = q_ref[0, :, pr * LANES:(pr + 1) * LANES].astype(F32) * (HEAD_DIM ** -0.5 * LOG2E)
        qs += [jnp.where(q_head0, q, bias0).astype(BF16), jnp.where(q_head0, bias1, q).astype(BF16)]

    m_ref[...] = jnp.full_like(m_ref, -jnp.inf)
    acc_ref[...] = jnp.zeros_like(acc_ref)

    def tile(j, q0, causal):
        start = pl.multiple_of(j * tk, tk)
        cols = slice(q0, tq)
        if causal:
            key = lax.broadcasted_iota(jnp.int32, (tk, tq - q0), 0)
            qry = lax.broadcasted_iota(jnp.int32, (tk, tq - q0), 1)
            valid = key <= qry
        scores = []
        for ci, (pr, h) in enumerate(chains):
            s = _dot(k_refs[h][0, pl.ds(start, tk), pr * LANES:(pr + 1) * LANES], qs[ci][q0:], _NT)
            scores.append(jnp.where(valid, s, -jnp.inf) if causal else s)
        for ci, s in enumerate(scores):
            m = m_ref[ci, :, cols]
            m_new = jnp.maximum(m, jnp.max(s, axis=0, keepdims=True))
            m_ref[ci, :, cols] = m_new
            p = jnp.exp2(s - m_new).astype(BF16)
            pv = _dot(v_ref[ci * FOX_V_ROWS:(ci + 1) * FOX_V_ROWS, pl.ds(start, tk)], p)
            acc_ref[ci, :, cols] = jnp.exp2(m - m_new) * acc_ref[ci, :, cols] + pv

    def body(j, carry):
        tile(j, 0, False)
        return carry

    lax.fori_loop(0, qi * n_sub, body, 0)
    for d in range(n_sub):
        tile(qi * n_sub + d, d * tk, True)

    for pr in range(n_pairs):
        sl = slice(pr * LANES, (pr + 1) * LANES)
        out_t = jnp.concatenate(
            [acc_ref[ci][:HEAD_DIM] * (1.0 / acc_ref[ci][HEAD_DIM:HEAD_DIM + 1]) for ci in (2 * pr, 2 * pr + 1)], axis=0)
        g = g_ref[0, :, sl]
        o_ref[0, :, sl] = (out_t.T * (g * _sigmoid(g))).astype(o_ref.dtype)


def _fox_attn(pb3, off_q, k0, k1, vt, pt3, off_g, fw):
    b, t, _ = pb3.shape
    tq = min(FOX_TQ, t)
    tk = min(FOX_TK, tq)
    n_pairs = min(FOX_PAIRS, fw // LANES)
    w = n_pairs * LANES
    v_rows = 2 * n_pairs * FOX_V_ROWS
    assert t % tq == 0 and tq % tk == 0 and fw % w == 0 and off_q % w == 0 and off_g % w == 0
    qb, gb = off_q // w, off_g // w
    keys = pl.BlockSpec((1, t, w), lambda bi, p, i: (bi, 0, p))
    return pl.pallas_call(
        functools.partial(_fox_attn_kernel, tk=tk),
        grid=(b, fw // w, t // tq),
        in_specs=[pl.BlockSpec((1, tq, w), lambda bi, p, i: (bi, i, qb + p)), keys, keys,
                  pl.BlockSpec((v_rows, t), lambda bi, p, i: (p, bi)),
                  pl.BlockSpec((1, tq, w), lambda bi, p, i: (bi, i, gb + p))],
        out_specs=pl.BlockSpec((1, tq, w), lambda bi, p, i: (bi, i, p)),
        out_shape=jax.ShapeDtypeStruct((b, t, fw), BF16),
        scratch_shapes=[pltpu.VMEM((2 * n_pairs, 1, tq), F32), pltpu.VMEM((2 * n_pairs, FOX_V_ROWS, tq), F32)],
        compiler_params=_params("parallel", "parallel", "arbitrary"),
        name="fox_attn",
    )(pb3, k0, k1, vt, pt3)


def _mem_attn_kernel(q_ref, kv_ref, g_ref, o_ref):
    mw = q_ref.shape[2]
    hd = mw // MEM_HEADS
    scale = hd ** -0.5
    outs = []
    for h in range(MEM_HEADS):
        q = q_ref[0, :, h * hd:(h + 1) * hd]
        mk = kv_ref[0, :, h * hd:(h + 1) * hd]
        mv = kv_ref[0, :, mw + h * hd:mw + (h + 1) * hd]
        s = _dot(q, mk, _NT) * scale
        p = jnp.exp(s - jnp.max(s, axis=1, keepdims=True))
        l = jnp.sum(p, axis=1, keepdims=True)
        outs.append(_dot(p.astype(BF16), mv) / l)
    g = g_ref[0]
    o_ref[0] = (jnp.concatenate(outs, axis=1) * (g * _sigmoid(g))).astype(o_ref.dtype)


def _mem_attn(pq3, off_q, mkv3, pt3, off_g, mw):
    b, t, _ = pq3.shape
    tm = min(ROWS_MEM, t)
    assert t % tm == 0 and off_q % mw == 0 and off_g % mw == 0 and (mw // MEM_HEADS) % LANES == 0
    n_mem = mkv3.shape[1]
    qb, gb = off_q // mw, off_g // mw
    return pl.pallas_call(
        _mem_attn_kernel,
        grid=(b, t // tm),
        in_specs=[pl.BlockSpec((1, tm, mw), lambda bi, i: (bi, i, qb)),
                  pl.BlockSpec((1, n_mem, 2 * mw), lambda bi, i: (bi, 0, 0)),
                  pl.BlockSpec((1, tm, mw), lambda bi, i: (bi, i, gb))],
        out_specs=pl.BlockSpec((1, tm, mw), lambda bi, i: (bi, i, 0)),
        out_shape=jax.ShapeDtypeStruct((b, t, mw), BF16),
        compiler_params=_params("parallel", "parallel"),
        name="mem_attn",
    )(pq3, mkv3, pt3)


def _layer(x, mem, g_pre, w_in, mu_rwkv, w0, w_decay_up, a0, w_iclr_up, k_k, k_a, r_k,
           ln_x_w, ln_x_b, b_f, g_mem, w_mem_kv, w_out, g_post):
    b, t, d = x.shape
    n_mem = mem.shape[1]
    c = w0.shape[0]
    fh = b_f.shape[0]
    fw = fh * HEAD_DIM
    mw = w_mem_kv.shape[1] // 2
    lora = w_decay_up.shape[0]
    assert lora == LANES and w_iclr_up.shape[0] == LANES and fh <= WIDE
    assert w_in.shape[1] == 4 * c + 2 * lora + 4 * fw + fh + 2 * mw
    assert c % WIDE == 0 and fw % WIDE == 0 and mw % WIDE == 0

    o_g = 3 * c + 2 * lora
    o_fq = o_g + c
    o_fv = o_fq + 2 * fw
    o_f = o_fv + fw
    o_gfox = o_f + fh
    o_mq = o_gfox + fw
    o_gmq = o_mq + mw

    x2 = x.reshape(b * t, d)
    h = _rmsnorm(x2, g_pre, BF16)

    lora_f = WIDE - 2 * lora
    assert o_f % LANES == 0 and o_f + lora_f <= w_in.shape[1]
    pieces_a = ((o_g, c), (0, 3 * c), (o_gmq, mw), (o_gfox, fw), (3 * c, 2 * lora), (o_f, lora_f))
    pieces_b = ((o_fq, 2 * fw), (o_mq, mw))
    w_a, w_b, w_vt = _weight_prep(w_in.T, pieces_a, pieces_b, (o_fv, fw))
    off_a = {"g_rwkv": 0, "r": c, "g_mq": 4 * c, "g_fox": 4 * c + mw, "lora": 4 * c + mw + fw}
    off_b = {"fq": 0, "fk": fw, "mq": 2 * fw}

    pa3 = _matmul(h, w_a, F32, 2 * WIDE, "in_proj_a").reshape(b, t, -1)
    pb3 = _matmul(h, w_b, BF16, 2 * WIDE, "in_proj_b").reshape(b, t, -1)
    vt = _proj_t(w_vt, h)

    y_rwkv = _rwkv_scan(pa3, off_a["r"], off_a["lora"], off_a["g_rwkv"], mu_rwkv, w0, w_decay_up, a0, w_iclr_up,
                        k_k, k_a, r_k.reshape(-1), ln_x_w, ln_x_b)

    k0, k1 = _fox_prefix(pa3, off_a["lora"], b_f, pb3, off_b["fk"], fw)
    y_fox = _fox_attn(pb3, off_b["fq"], k0, k1, vt, pa3, off_a["g_fox"], fw)

    hm = _rmsnorm(mem.reshape(b * n_mem, d), g_mem, BF16)
    mkv3 = _proj([hm], [(w_mem_kv, (d, WIDE), lambda j: (0, j), (0, 0))], 2 * mw, WIDE, BF16,
                 "mem_kv_proj").reshape(b, n_mem, 2 * mw)
    y_mem = _mem_attn(pb3, off_b["mq"], mkv3, pa3, off_a["g_mq"], mw)

    ys = [y_rwkv.reshape(b * t, c), y_fox.reshape(b * t, fw), y_mem.reshape(b * t, mw)]
    return _out_proj(ys, _cast(w_out, BF16, "w_out_cast"), x2, g_post).reshape(b, t, d)


def kernel(x, mem, g_pre, w_in, mu_rwkv, w0, w_decay_up, a0, w_iclr_up, k_k, k_a, r_k, ln_x_w, ln_x_b, b_f,
           g_mem, w_mem_kv, w_out, g_post):
    for l in range(g_pre.shape[0]):
        x = _layer(x, mem, g_pre[l], w_in[l], mu_rwkv[l], w0[l], w_decay_up[l], a0[l], w_iclr_up[l], k_k[l],
                   k_a[l], r_k[l], ln_x_w[l], ln_x_b[l], b_f[l], g_mem[l], w_mem_kv[l], w_out[l], g_post[l])
    return x
```

```python
import functools

import jax
import jax.numpy as jnp
from jax import lax
from jax.experimental import pallas as pl
from jax.experimental.pallas import tpu as pltpu

HEAD_DIM = 64
MEM_HEADS = 4
RMS_EPS = 1e-6
GN_EPS = 64e-5
LOG2E = 1.4426950408889634

LANES = 128
MXU = 256
UNIT_HEADS = MXU // HEAD_DIM
UNIT = UNIT_HEADS * HEAD_DIM
RWKV_CHUNK = 64
WIDE = 512
VMEM_LIMIT_BYTES = 56 * 1024 * 1024
FOX_BIAS_PARTS = 3
FOX_V_ROWS = HEAD_DIM + 16

ROWS_ELEMENTWISE = 512
ROWS_PROJ = 1024
ROWS_PROJ_T = 1024
ROWS_OUT = 512
W_RING_DEPTH = 3
COLS_WEIGHT_PREP = 256
RWKV_CHUNKS_PER_STEP = 4
ROWS_PREFIX = 512
ROWS_MEM = 512
FOX_TQ = 512
FOX_TK = 256
FOX_PAIRS = 4
RWKV_UNITS = 6

F32 = jnp.float32
BF16 = jnp.bfloat16

_NN = (((1,), (0,)), ((), ()))
_NT = (((1,), (1,)), ((), ()))
_TN = (((0,), (0,)), ((), ()))


def _dot(a, b, dims=_NN):
    return lax.dot_general(a, b, dims, preferred_element_type=F32)


def _split_bf16(x, n):
    parts = []
    rem = x
    for _ in range(n):
        p = rem.astype(BF16)
        parts.append(p)
        rem = rem - p.astype(F32)
    return parts


def _dot_f32(a, b):
    a_hi, a_lo = _split_bf16(a, 2)
    b_hi, b_lo = _split_bf16(b, 2)
    return _dot(a_hi, b_hi) + (_dot(a_lo, b_hi) + _dot(a_hi, b_lo))


def _sigmoid(x):
    return 1.0 / (1.0 + jnp.exp(-x))


def _softplus(x):
    return jnp.maximum(x, 0.0) + jnp.log(1.0 + jnp.exp(-jnp.abs(x)))


def _params(*semantics):
    return pltpu.CompilerParams(dimension_semantics=semantics, vmem_limit_bytes=VMEM_LIMIT_BYTES)


def _rmsnorm_kernel(x_ref, g_ref, o_ref):
    x = x_ref[...]
    ms = jnp.mean(x * x, axis=-1, keepdims=True)
    o_ref[...] = (x * lax.rsqrt(ms + RMS_EPS) * g_ref[...]).astype(o_ref.dtype)


def _rmsnorm(x2d, g, out_dtype):
    m, d = x2d.shape
    tm = min(ROWS_ELEMENTWISE, m)
    assert m % tm == 0
    return pl.pallas_call(
        _rmsnorm_kernel,
        grid=(m // tm,),
        in_specs=[pl.BlockSpec((tm, d), lambda i: (i, 0)), pl.BlockSpec((1, d), lambda i: (0, 0))],
        out_specs=pl.BlockSpec((tm, d), lambda i: (i, 0)),
        out_shape=jax.ShapeDtypeStruct((m, d), out_dtype),
        compiler_params=_params("parallel"),
        name="rmsnorm",
    )(x2d, g.reshape(1, d))


def _proj_kernel(*refs, n_x, placement):
    xs = refs[:n_x]
    ws = refs[n_x:n_x + len(placement)]
    o_ref, wb_ref = refs[n_x + len(placement):]

    @pl.when(pl.program_id(1) == 0)
    def _():
        for w_ref, (r0, c0) in zip(ws, placement):
            wb_ref[r0:r0 + w_ref.shape[0], c0:c0 + w_ref.shape[1]] = w_ref[...].astype(BF16)

    acc = None
    r0 = 0
    for x_ref in xs:
        kx = x_ref.shape[1]
        d = _dot(x_ref[...], wb_ref[r0:r0 + kx, :])
        acc = d if acc is None else acc + d
        r0 += kx
    o_ref[...] = acc.astype(o_ref.dtype)


def _proj(xs, weights, n_cols, tn, out_dtype, name):
    m = xs[0].shape[0]
    k_total = sum(x.shape[1] for x in xs)
    tm = min(ROWS_PROJ, m)
    tn = min(tn, n_cols)
    assert m % tm == 0 and n_cols % tn == 0
    in_specs = [pl.BlockSpec((tm, x.shape[1]), lambda j, i: (i, 0)) for x in xs]
    for _, shape, index_fn, _ in weights:
        in_specs.append(pl.BlockSpec(shape, lambda j, i, f=index_fn: f(j)))
    return pl.pallas_call(
        functools.partial(_proj_kernel, n_x=len(xs), placement=tuple(p for _, _, _, p in weights)),
        grid=(n_cols // tn, m // tm),
        in_specs=in_specs,
        out_specs=pl.BlockSpec((tm, tn), lambda j, i: (i, j)),
        out_shape=jax.ShapeDtypeStruct((m, n_cols), out_dtype),
        scratch_shapes=[pltpu.VMEM((k_total, tn), BF16)],
        compiler_params=_params("parallel", "arbitrary"),
        name=name,
    )(*xs, *[w for w, _, _, _ in weights])


def _matmul_kernel(x_ref, w_ref, o_ref):
    o_ref[...] = _dot(x_ref[...], w_ref[...]).astype(o_ref.dtype)


def _matmul(x, w, out_dtype, tn, name):
    m, k = x.shape
    n = w.shape[1]
    tm = min(ROWS_PROJ, m)
    tn = min(tn, n)
    assert m % tm == 0 and n % tn == 0
    return pl.pallas_call(
        _matmul_kernel,
        grid=(n // tn, m // tm),
        in_specs=[pl.BlockSpec((tm, k), lambda j, i: (i, 0)), pl.BlockSpec((k, tn), lambda j, i: (0, j))],
        out_specs=pl.BlockSpec((tm, tn), lambda j, i: (i, j)),
        out_shape=jax.ShapeDtypeStruct((m, n), out_dtype),
        compiler_params=_params("parallel", "parallel"),
        name=name,
    )(x, w)


def _weight_prep_kernel(wt_ref, wa_ref, wb_ref, wvt_ref, *, pieces_a, pieces_b, piece_vt):
    def gather(pieces, o_ref):
        dst = 0
        for src, width in pieces:
            o_ref[:, dst:dst + width] = wt_ref[src:src + width, :].T.astype(o_ref.dtype)
            dst += width

    gather(pieces_a, wa_ref)
    gather(pieces_b, wb_ref)
    src, width = piece_vt
    wvt_ref[...] = wt_ref[src:src + width, :].astype(wvt_ref.dtype)


def _weight_prep(w_in_t, pieces_a, pieces_b, piece_vt):
    n, k = w_in_t.shape
    tc = min(COLS_WEIGHT_PREP, k)
    assert k % tc == 0 and all(s % 8 == 0 and wd % 16 == 0 for s, wd in pieces_a + pieces_b + (piece_vt,))
    na = sum(wd for _, wd in pieces_a)
    nb = sum(wd for _, wd in pieces_b)
    return pl.pallas_call(
        functools.partial(_weight_prep_kernel, pieces_a=pieces_a, pieces_b=pieces_b, piece_vt=piece_vt),
        grid=(k // tc,),
        in_specs=[pl.BlockSpec((n, tc), lambda i: (0, i))],
        out_specs=[pl.BlockSpec((tc, na), lambda i: (i, 0)), pl.BlockSpec((tc, nb), lambda i: (i, 0)),
                   pl.BlockSpec((piece_vt[1], tc), lambda i: (0, i))],
        out_shape=[jax.ShapeDtypeStruct((k, na), BF16), jax.ShapeDtypeStruct((k, nb), BF16),
                   jax.ShapeDtypeStruct((piece_vt[1], k), BF16)],
        compiler_params=_params("parallel"),
        name="weight_prep",
    )(w_in_t)


def _proj_t_kernel(wt_ref, x_ref, v_ref):
    vt = _dot(wt_ref[...], x_ref[...], _NT)
    ones = jnp.ones((FOX_V_ROWS - HEAD_DIM, vt.shape[1]), v_ref.dtype)
    for h in range(vt.shape[0] // HEAD_DIM):
        v_ref[h * FOX_V_ROWS:h * FOX_V_ROWS + HEAD_DIM, :] = vt[h * HEAD_DIM:(h + 1) * HEAD_DIM].astype(v_ref.dtype)
        v_ref[h * FOX_V_ROWS + HEAD_DIM:(h + 1) * FOX_V_ROWS, :] = ones


def _proj_t(wt, x):
    n, k = wt.shape
    m = x.shape[0]
    tm = min(ROWS_PROJ_T, m)
    assert m % tm == 0 and n % HEAD_DIM == 0
    n_aug = n // HEAD_DIM * FOX_V_ROWS
    return pl.pallas_call(
        _proj_t_kernel,
        grid=(m // tm,),
        in_specs=[pl.BlockSpec((n, k), lambda i: (0, 0)), pl.BlockSpec((tm, k), lambda i: (i, 0))],
        out_specs=pl.BlockSpec((n_aug, tm), lambda i: (0, i)),
        out_shape=jax.ShapeDtypeStruct((n_aug, m), BF16),
        compiler_params=_params("parallel"),
        name="in_proj_vt",
    )(wt, x)


def _cast_kernel(x_ref, o_ref):
    o_ref[...] = x_ref[...].astype(o_ref.dtype)


def _cast(x, dtype, name):
    m, n = x.shape
    tm = min(ROWS_OUT, m)
    assert m % tm == 0
    return pl.pallas_call(
        _cast_kernel,
        grid=(m // tm,),
        in_specs=[pl.BlockSpec((tm, n), lambda i: (i, 0))],
        out_specs=pl.BlockSpec((tm, n), lambda i: (i, 0)),
        out_shape=jax.ShapeDtypeStruct((m, n), dtype),
        compiler_params=_params("parallel"),
        name=name,
    )(x)


def _out_proj_kernel(*refs, n_y):
    ys = refs[:n_y]
    w_hbm, x_ref, g_ref, o_ref, ss_ref, w_ring, w_sem = refs[n_y:]
    j = pl.program_id(1)
    n_col = pl.num_programs(1)
    tn = w_ring.shape[2]
    d = o_ref.shape[1]
    step = pl.program_id(0) * n_col + j
    n_steps = pl.num_programs(0) * n_col

    def weight_copy(s):
        col = pl.multiple_of(lax.rem(s, n_col) * tn, tn)
        slot = lax.rem(s, W_RING_DEPTH)
        return pltpu.make_async_copy(w_hbm.at[:, pl.ds(col, tn)], w_ring.at[slot], w_sem.at[slot])

    @pl.when(step == 0)
    def _():
        for s in range(W_RING_DEPTH - 1):
            weight_copy(s).start()

    @pl.when(step + (W_RING_DEPTH - 1) < n_steps)
    def _():
        weight_copy(step + (W_RING_DEPTH - 1)).start()

    weight_copy(step).wait()
    w_ref = w_ring.at[lax.rem(step, W_RING_DEPTH)]

    acc = None
    r0 = 0
    for y_ref in ys:
        ky = y_ref.shape[1]
        part = _dot(y_ref[...], w_ref[r0:r0 + ky, :])
        acc = part if acc is None else acc + part
        r0 += ky

    @pl.when(j == 0)
    def _():
        ss_ref[...] = jnp.zeros_like(ss_ref)

    ss_ref[...] += jnp.sum(acc * acc, axis=1, keepdims=True)
    o_ref[:, pl.ds(pl.multiple_of(j * tn, tn), tn)] = acc

    @pl.when(j == pl.num_programs(1) - 1)
    def _():
        scale = lax.rsqrt(ss_ref[...] * (1.0 / d) + RMS_EPS)
        for c0 in range(0, d, tn):
            sl = slice(c0, c0 + tn)
            o_ref[:, sl] = x_ref[:, sl] + o_ref[:, sl] * scale * g_ref[:, sl]


def _out_proj(ys, w_bf16, x2d, g):
    m, d = x2d.shape
    tm = min(ROWS_OUT, m)
    tn = min(WIDE, d)
    k = w_bf16.shape[0]
    assert m % tm == 0 and d % tn == 0 and sum(y.shape[1] for y in ys) == k
    assert (m // tm) * (d // tn) >= W_RING_DEPTH - 1
    row = pl.BlockSpec((tm, d), lambda i, j: (i, 0))
    return pl.pallas_call(
        functools.partial(_out_proj_kernel, n_y=len(ys)),
        grid=(m // tm, d // tn),
        in_specs=[pl.BlockSpec((tm, y.shape[1]), lambda i, j: (i, 0)) for y in ys]
                 + [pl.BlockSpec(memory_space=pl.ANY), row, pl.BlockSpec((1, d), lambda i, j: (0, 0))],
        out_specs=row,
        out_shape=jax.ShapeDtypeStruct((m, d), x2d.dtype),
        scratch_shapes=[pltpu.VMEM((tm, 1), F32), pltpu.VMEM((W_RING_DEPTH, k, tn), BF16),
                        pltpu.SemaphoreType.DMA((W_RING_DEPTH,))],
        compiler_params=_params("arbitrary", "arbitrary"),
        name="out_proj_post",
    )(*ys, w_bf16, x2d, g.reshape(1, d))


def _rwkv_scan_kernel(*refs, n_units, L):
    r_ref, k_ref, v_ref, lora_ref = refs[:4]
    g_refs = refs[4:4 + n_units]
    (mu_r_ref, mu_k_ref, mu_v_ref, mu_lora_ref, w0_ref, wdu_ref, a0_ref, wiu_ref, kk_ref, ka_ref,
     rk_ref, lnw_ref, lnb_ref, o_ref, s_ref, carry_r, carry_k, carry_v, carry_lora) = refs[4 + n_units:]

    @pl.when(pl.program_id(2) == 0)
    def _():
        for z in (s_ref, carry_r, carry_k, carry_v, carry_lora):
            z[...] = jnp.zeros_like(z)

    rows = r_ref.shape[1]
    n_ch = rows // L
    first_row = lax.broadcasted_iota(jnp.int32, (rows, 1), 0) == 0

    def shift(x, carry_ref, mu):
        prev = jnp.where(first_row, carry_ref[...], pltpu.roll(x, 1, 0))
        carry_ref[...] = x[rows - 1:rows, :]
        return x + (prev - x) * mu

    r_all = shift(r_ref[0], carry_r, mu_r_ref[...])
    k_raw = shift(k_ref[0], carry_k, mu_k_ref[...])
    v_all = shift(v_ref[0], carry_v, mu_v_ref[...])
    lora = shift(lora_ref[0][:, :2 * LANES], carry_lora, mu_lora_ref[...])
    w_pre = -_softplus(-(w0_ref[...] + _dot_f32(jnp.tanh(lora[:, :LANES]), wdu_ref[...]))) - 0.5
    lw_all = -jnp.exp(w_pre)
    alpha = _sigmoid(a0_ref[...] + _dot_f32(lora[:, LANES:], wiu_ref[...]))
    k_all = k_raw * (1.0 + (alpha - 1.0) * ka_ref[...])
    kk_all = k_raw * kk_ref[...]
    hl = UNIT_HEADS * L
    n_sq = L.bit_length() - 2
    assert 2 ** (n_sq + 1) == L
    units = range(n_units)

    def blk(shape, d0, d1):
        return (lax.broadcasted_iota(jnp.int32, shape, 0) // d0) == (lax.broadcasted_iota(jnp.int32, shape, 1) // d1)

    same = blk((hl, hl), L, L)
    u_row = lax.broadcasted_iota(jnp.int32, (L, hl), 0)
    u_col = lax.broadcasted_iota(jnp.int32, (L, hl), 1) % L
    strict_u = u_col < u_row
    incl_u = u_col <= u_row
    eye_u = jnp.where(u_col == u_row, 1.0, 0.0)
    bd_rows = blk((hl, UNIT), L, HEAD_DIM)
    bd_state = blk((UNIT, UNIT), HEAD_DIM, HEAD_DIM)
    ones_bd = jnp.where(bd_state, 1.0, 0.0).astype(BF16)
    c_row = lax.broadcasted_iota(jnp.int32, (rows, rows), 0)
    c_col = lax.broadcasted_iota(jnp.int32, (rows, rows), 1)
    tril = jnp.where((c_row // L == c_col // L) & (c_row >= c_col), 1.0, 0.0).astype(BF16)

    def rep(x):
        return jnp.concatenate([x] * UNIT_HEADS, axis=0)

    def stack(x):
        return jnp.where(bd_rows, rep(x), 0.0)

    def headsum(x):
        return _dot(x.astype(BF16), ones_bd)

    def lanes(x, u):
        return x[:, u * UNIT:(u + 1) * UNIT]

    chains = [(ch, u) for ch in range(n_ch) for u in units]
    lw_hi, lw_lo = _split_bf16(lw_all, 2)
    cl_all = _dot(tril, lw_hi) + _dot(tril, lw_lo)
    kk_norm = [headsum(lanes(kk_all, u) * lanes(kk_all, u)) for u in units]
    kk_unit = [lanes(kk_all, u) * lax.rsqrt(jnp.maximum(kk_norm[u], 1e-24)) for u in units]

    ah, rh, e_last, lhs, rhs_b, rhs_k, upd_rhs, v = [], [], [], [], [], [], [], []
    for ch, u in chains:
        rs = slice(ch * L, (ch + 1) * L)
        sl = slice(u * UNIT, (u + 1) * UNIT)
        cl = cl_all[rs, sl]
        e_pos = jnp.exp(cl)
        e_neg = jnp.exp(-cl)
        e_rem = e_pos[L - 1:L, :] * e_neg
        kk = kk_unit[u][rs]
        a, b, k = -kk, kk * alpha[rs, sl], k_all[rs, sl]
        ah.append(a * jnp.exp(cl - lw_all[rs, sl]))
        rh.append(r_all[rs, sl] * e_pos)
        e_last.append(e_pos[L - 1:L, :])
        lhs.append(jnp.concatenate([ah[-1], rh[-1]], axis=0).astype(BF16))
        rhs_b.append(stack(b * e_neg).astype(BF16))
        rhs_k.append(stack(k * e_neg).astype(BF16))
        upd_rhs.append(jnp.concatenate([b * e_rem, k * e_rem], axis=0).astype(BF16))
        v.append(v_all[rs, sl])

    ids = range(len(chains))
    a_b = [_dot(lhs[i], rhs_b[i], _NT) for i in ids]
    a_k = [_dot(lhs[i], rhs_k[i], _NT) for i in ids]
    pw = [jnp.where(strict_u, a_b[i][:L], 0.0) for i in ids]
    a_ak = [jnp.where(strict_u, a_k[i][:L], 0.0).astype(BF16) for i in ids]
    a_rb = [jnp.where(incl_u, a_b[i][L:], 0.0).astype(BF16) for i in ids]
    a_rk = [jnp.where(incl_u, a_k[i][L:], 0.0).astype(BF16) for i in ids]

    def blockdiag(m_u):
        return jnp.where(same, rep(m_u), 0.0).astype(BF16)

    t_u = [eye_u + pw[i] for i in ids]
    pw_bd = [blockdiag(pw[i]) for i in ids]
    for _ in range(n_sq):
        pw = [_dot(pw[i].astype(BF16), pw_bd[i]) for i in ids]
        pw_bd = [blockdiag(pw[i]) for i in ids]
        t_u = [t_u[i] + _dot(t_u[i].astype(BF16), pw_bd[i]) for i in ids]
    t_u = [t_u[i].astype(BF16) for i in ids]
    v_s = [stack(v[i]).astype(BF16) for i in ids]

    s_cur = [s_ref[u] for u in units]
    y = [None] * len(chains)
    for ch in range(n_ch):
        cid = [ch * n_units + u for u in units]
        xr = [_dot(lhs[i], s_cur[u].astype(BF16), _NT) for u, i in zip(units, cid)]
        x = [xr[u][:L] + _dot(a_ak[i], v_s[i]) for u, i in zip(units, cid)]
        uu = [_dot(t_u[i], stack(x[u]).astype(BF16)) for u, i in zip(units, cid)]
        for u, i in zip(units, cid):
            y[i] = xr[u][L:] + _dot(a_rb[i], stack(uu[u]).astype(BF16)) + _dot(a_rk[i], v_s[i])
            upd = _dot(jnp.concatenate([uu[u], v[i]], axis=0).astype(BF16), upd_rhs[i], _TN)
            s_cur[u] = jnp.where(bd_state, s_cur[u] * e_last[i] + upd, 0.0)
    for u in units:
        s_ref[u] = s_cur[u]

    inv_n = 1.0 / HEAD_DIM
    y = [jnp.concatenate([y[ch * n_units + u] for ch in range(n_ch)], axis=0) for u in units]
    d = [y[u] - headsum(y[u]) * inv_n for u in units]
    var = [headsum(d[u] * d[u]) * inv_n for u in units]
    bonus = [headsum(lanes(r_all, u) * lanes(k_all, u) * lanes(rk_ref[...], u)) * lanes(v_all, u) for u in units]
    for u in units:
        sl = slice(u * UNIT, (u + 1) * UNIT)
        yn = d[u] * lax.rsqrt(var[u] + GN_EPS) * lnw_ref[:, sl] + lnb_ref[:, sl]
        g = g_refs[u][0]
        o_ref[0, :, sl] = ((yn + bonus[u]) * (g * _sigmoid(g))).astype(o_ref.dtype)


def _rwkv_scan(pa3, off_r, off_lora, off_g, mu, w0, wdu, a0, wiu, k_k, k_a, r_k, ln_w, ln_b):
    b, t, _ = pa3.shape
    c = w0.shape[0]
    L = RWKV_CHUNK * min(RWKV_CHUNKS_PER_STEP, t // RWKV_CHUNK)
    n_units = min(RWKV_UNITS, c // UNIT)
    w = n_units * UNIT
    assert t % L == 0 and c % w == 0 and off_r % w == 0 and off_g % UNIT == 0 and off_lora % WIDE == 0
    rb, gb, lb, ng = off_r // w, off_g // UNIT, off_lora // WIDE, c // w

    def col(base):
        return pl.BlockSpec((1, L, w), lambda bi, gi, ci: (bi, ci, base + gi))

    def vec(base=0):
        return pl.BlockSpec((1, w), lambda bi, gi, ci: (0, base + gi))

    gates = [pl.BlockSpec((1, L, UNIT), lambda bi, gi, ci, u=u: (bi, ci, gb + gi * n_units + u))
             for u in range(n_units)]
    lora_w = pl.BlockSpec((wdu.shape[0], w), lambda bi, gi, ci: (0, gi))
    mu_rkv = mu[:3 * c].reshape(1, 3 * c)
    mu_lora = mu[3 * c:].reshape(1, 2 * LANES)
    row = lambda z: z.reshape(1, c)
    return pl.pallas_call(
        functools.partial(_rwkv_scan_kernel, n_units=n_units, L=RWKV_CHUNK),
        grid=(b, ng, t // L),
        in_specs=[col(rb), col(rb + ng), col(rb + 2 * ng),
                  pl.BlockSpec((1, L, WIDE), lambda bi, gi, ci: (bi, ci, lb))] + gates
                 + [vec(0), vec(ng), vec(2 * ng), pl.BlockSpec((1, 2 * LANES), lambda bi, gi, ci: (0, 0)),
                    vec(), lora_w, vec(), lora_w, vec(), vec(), vec(), vec(), vec()],
        out_specs=pl.BlockSpec((1, L, w), lambda bi, gi, ci: (bi, ci, gi)),
        out_shape=jax.ShapeDtypeStruct((b, t, c), BF16),
        scratch_shapes=[pltpu.VMEM((n_units, UNIT, UNIT), F32)] + [pltpu.VMEM((1, w), F32)] * 3
                       + [pltpu.VMEM((1, 2 * LANES), F32)],
        compiler_params=_params("parallel", "parallel", "arbitrary"),
        name="rwkv_scan",
    )(pa3, pa3, pa3, pa3, *([pa3] * n_units), mu_rkv, mu_rkv, mu_rkv, mu_lora,
      row(w0), wdu, row(a0), wiu, row(k_k), row(k_a), row(r_k), row(ln_w), row(ln_b))


def _fox_prefix_kernel(f_ref, bf_ref, k_ref, k0_ref, k1_ref, carry_ref):
    @pl.when(pl.program_id(1) == 0)
    def _():
        carry_ref[...] = jnp.zeros_like(carry_ref)

    x = f_ref[0][:, 2 * LANES:3 * LANES] + bf_ref[...]
    log_f = -_softplus(-x)
    tc = x.shape[0]
    tril = jnp.where(lax.broadcasted_iota(jnp.int32, (tc, tc), 0) >= lax.broadcasted_iota(jnp.int32, (tc, tc), 1),
                     1.0, 0.0).astype(BF16)
    c = carry_ref[...]
    for part in _split_bf16(log_f, 3):
        c = c + _dot(tril, part)
    carry_ref[...] = c[tc - 1:tc, :]

    fw = k_ref.shape[2]
    src = lax.broadcasted_iota(jnp.int32, (FOX_BIAS_PARTS * LANES, fw), 0)
    dst = lax.broadcasted_iota(jnp.int32, (FOX_BIAS_PARTS * LANES, fw), 1)
    lane = dst % LANES
    head = 2 * (dst // LANES) + jnp.where(lane < HEAD_DIM, 1, 0)
    sel = jnp.where((src % LANES == head) & (lane % HEAD_DIM == src // LANES), 1.0, 0.0).astype(BF16)
    parts = jnp.concatenate(_split_bf16(c * LOG2E, FOX_BIAS_PARTS), axis=1)
    bias = _dot(parts, sel).astype(BF16)
    k = k_ref[0]
    first = (lax.broadcasted_iota(jnp.int32, (tc, fw), 1) % LANES) < HEAD_DIM
    k0_ref[0] = jnp.where(first, k, bias)
    k1_ref[0] = jnp.where(first, bias, k)


def _fox_prefix(pt3, off_f, b_f, pb3, off_k, fw):
    b, t, _ = pt3.shape
    fh = b_f.shape[0]
    tc = min(ROWS_PREFIX, t)
    assert t % tc == 0 and fh <= LANES and fw == fh * HEAD_DIM and off_f % WIDE == 0 and off_k % fw == 0
    bf = jnp.zeros((1, LANES), F32).at[0, :fh].set(b_f)
    fb, kb = off_f // WIDE, off_k // fw
    out = pl.BlockSpec((1, tc, fw), lambda bi, i: (bi, i, 0))
    return pl.pallas_call(
        _fox_prefix_kernel,
        grid=(b, t // tc),
        in_specs=[pl.BlockSpec((1, tc, WIDE), lambda bi, i: (bi, i, fb)),
                  pl.BlockSpec((1, LANES), lambda bi, i: (0, 0)),
                  pl.BlockSpec((1, tc, fw), lambda bi, i: (bi, i, kb))],
        out_specs=[out, out],
        out_shape=[jax.ShapeDtypeStruct((b, t, fw), BF16)] * 2,
        scratch_shapes=[pltpu.VMEM((1, LANES), F32)],
        compiler_params=_params("parallel", "arbitrary"),
        name="fox_prefix",
    )(pt3, bf, pb3)


def _fox_attn_kernel(q_ref, k0_ref, k1_ref, v_ref, g_ref, o_ref, m_ref, acc_ref, *, tk):
    qi = pl.program_id(2)
    tq = q_ref.shape[1]
    n_pairs = q_ref.shape[2] // LANES
    n_sub = tq // tk
    q_lane = lax.broadcasted_iota(jnp.int32, (tq, LANES), 1)
    q_head0 = q_lane < HEAD_DIM
    bias0 = jnp.where((q_lane >= HEAD_DIM) & (q_lane < HEAD_DIM + FOX_BIAS_PARTS), -1.0, 0.0)
    bias1 = jnp.where(q_lane < FOX_BIAS_PARTS, -1.0, 0.0)
    k_refs = (k0_ref, k1_ref)
    chains = [(pr, h) for pr in range(n_pairs) for h in range(2)]

    qs = []
    for pr in range(n_pairs):
        q = q_ref[0, :, pr * LANES:(pr + 1) * LANES].astype(F32) * (HEAD_DIM ** -0.5 * LOG2E)
        qs += [jnp.where(q_head0, q, bias0).astype(BF16), jnp.where(q_head0, bias1, q).astype(BF16)]

    m_ref[...] = jnp.full_like(m_ref, -jnp.inf)
    acc_ref[...] = jnp.zeros_like(acc_ref)

    def tile(j, q0, causal):
        start = pl.multiple_of(j * tk, tk)
        cols = slice(q0, tq)
        if causal:
            key = lax.broadcasted_iota(jnp.int32, (tk, tq - q0), 0)
            qry = lax.broadcasted_iota(jnp.int32, (tk, tq - q0), 1)
            valid = key <= qry
        scores = []
        for ci, (pr, h) in enumerate(chains):
            s = _dot(k_refs[h][0, pl.ds(start, tk), pr * LANES:(pr + 1) * LANES], qs[ci][q0:], _NT)
            scores.append(jnp.where(valid, s, -jnp.inf) if causal else s)
        for ci, s in enumerate(scores):
            m = m_ref[ci, :, cols]
            m_new = jnp.maximum(m, jnp.max(s, axis=0, keepdims=True))
            m_ref[ci, :, cols] = m_new
            p = jnp.exp2(s - m_new).astype(BF16)
            pv = _dot(v_ref[ci * FOX_V_ROWS:(ci + 1) * FOX_V_ROWS, pl.ds(start, tk)], p)
            acc_ref[ci, :, cols] = jnp.exp2(m - m_new) * acc_ref[ci, :, cols] + pv

    def body(j, carry):
        tile(j, 0, False)
        return carry

    lax.fori_loop(0, qi * n_sub, body, 0)
    for d in range(n_sub):
        tile(qi * n_sub + d, d * tk, True)

    for pr in range(n_pairs):
        sl = slice(pr * LANES, (pr + 1) * LANES)
        out_t = jnp.concatenate(
            [acc_ref[ci][:HEAD_DIM] * (1.0 / acc_ref[ci][HEAD_DIM:HEAD_DIM + 1]) for ci in (2 * pr, 2 * pr + 1)], axis=0)
        g = g_ref[0, :, sl]
        o_ref[0, :, sl] = (out_t.T * (g * _sigmoid(g))).astype(o_ref.dtype)


def _fox_attn(pb3, off_q, k0, k1, vt, pt3, off_g, fw):
    b, t, _ = pb3.shape
    tq = min(FOX_TQ, t)
    tk = min(FOX_TK, tq)
    n_pairs = min(FOX_PAIRS, fw // LANES)
    w = n_pairs * LANES
    v_rows = 2 * n_pairs * FOX_V_ROWS
    assert t % tq == 0 and tq % tk == 0 and fw % w == 0 and off_q % w == 0 and off_g % w == 0
    qb, gb = off_q // w, off_g // w
    keys = pl.BlockSpec((1, t, w), lambda bi, p, i: (bi, 0, p))
    return pl.pallas_call(
        functools.partial(_fox_attn_kernel, tk=tk),
        grid=(b, fw // w, t // tq),
        in_specs=[pl.BlockSpec((1, tq, w), lambda bi, p, i: (bi, i, qb + p)), keys, keys,
                  pl.BlockSpec((v_rows, t), lambda bi, p, i: (p, bi)),
                  pl.BlockSpec((1, tq, w), lambda bi, p, i: (bi, i, gb + p))],
        out_specs=pl.BlockSpec((1, tq, w), lambda bi, p, i: (bi, i, p)),
        out_shape=jax.ShapeDtypeStruct((b, t, fw), BF16),
        scratch_shapes=[pltpu.VMEM((2 * n_pairs, 1, tq), F32), pltpu.VMEM((2 * n_pairs, FOX_V_ROWS, tq), F32)],
        compiler_params=_params("parallel", "parallel", "arbitrary"),
        name="fox_attn",
    )(pb3, k0, k1, vt, pt3)


def _mem_attn_kernel(q_ref, kv_ref, g_ref, o_ref):
    mw = q_ref.shape[2]
    hd = mw // MEM_HEADS
    scale = hd ** -0.5
    outs = []
    for h in range(MEM_HEADS):
        q = q_ref[0, :, h * hd:(h + 1) * hd]
        mk = kv_ref[0, :, h * hd:(h + 1) * hd]
        mv = kv_ref[0, :, mw + h * hd:mw + (h + 1) * hd]
        s = _dot(q, mk, _NT) * scale
        p = jnp.exp(s - jnp.max(s, axis=1, keepdims=True))
        l = jnp.sum(p, axis=1, keepdims=True)
        outs.append(_dot(p.astype(BF16), mv) / l)
    g = g_ref[0]
    o_ref[0] = (jnp.concatenate(outs, axis=1) * (g * _sigmoid(g))).astype(o_ref.dtype)


def _mem_attn(pq3, off_q, mkv3, pt3, off_g, mw):
    b, t, _ = pq3.shape
    tm = min(ROWS_MEM, t)
    assert t % tm == 0 and off_q % mw == 0 and off_g % mw == 0 and (mw // MEM_HEADS) % LANES == 0
    n_mem = mkv3.shape[1]
    qb, gb = off_q // mw, off_g // mw
    return pl.pallas_call(
        _mem_attn_kernel,
        grid=(b, t // tm),
        in_specs=[pl.BlockSpec((1, tm, mw), lambda bi, i: (bi, i, qb)),
                  pl.BlockSpec((1, n_mem, 2 * mw), lambda bi, i: (bi, 0, 0)),
                  pl.BlockSpec((1, tm, mw), lambda bi, i: (bi, i, gb))],
        out_specs=pl.BlockSpec((1, tm, mw), lambda bi, i: (bi, i, 0)),
        out_shape=jax.ShapeDtypeStruct((b, t, mw), BF16),
        compiler_params=_params("parallel", "parallel"),
        name="mem_attn",
    )(pq3, mkv3, pt3)


def _layer(x, mem, g_pre, w_in, mu_rwkv, w0, w_decay_up, a0, w_iclr_up, k_k, k_a, r_k,
           ln_x_w, ln_x_b, b_f, g_mem, w_mem_kv, w_out, g_post):
    b, t, d = x.shape
    n_mem = mem.shape[1]
    c = w0.shape[0]
    fh = b_f.shape[0]
    fw = fh * HEAD_DIM
    mw = w_mem_kv.shape[1] // 2
    lora = w_decay_up.shape[0]
    assert lora == LANES and w_iclr_up.shape[0] == LANES and fh <= WIDE
    assert w_in.shape[1] == 4 * c + 2 * lora + 4 * fw + fh + 2 * mw
    assert c % WIDE == 0 and fw % WIDE == 0 and mw % WIDE == 0

    o_g = 3 * c + 2 * lora
    o_fq = o_g + c
    o_fv = o_fq + 2 * fw
    o_f = o_fv + fw
    o_gfox = o_f + fh
    o_mq = o_gfox + fw
    o_gmq = o_mq + mw

    x2 = x.reshape(b * t, d)
    h = _rmsnorm(x2, g_pre, BF16)

    lora_f = WIDE - 2 * lora
    assert o_f % LANES == 0 and o_f + lora_f <= w_in.shape[1]
    pieces_a = ((o_g, c), (0, 3 * c), (o_gmq, mw), (o_gfox, fw), (3 * c, 2 * lora), (o_f, lora_f))
    pieces_b = ((o_fq, 2 * fw), (o_mq, mw))
    w_a, w_b, w_vt = _weight_prep(w_in.T, pieces_a, pieces_b, (o_fv, fw))
    off_a = {"g_rwkv": 0, "r": c, "g_mq": 4 * c, "g_fox": 4 * c + mw, "lora": 4 * c + mw + fw}
    off_b = {"fq": 0, "fk": fw, "mq": 2 * fw}

    pa3 = _matmul(h, w_a, F32, 2 * WIDE, "in_proj_a").reshape(b, t, -1)
    pb3 = _matmul(h, w_b, BF16, 2 * WIDE, "in_proj_b").reshape(b, t, -1)
    vt = _proj_t(w_vt, h)

    y_rwkv = _rwkv_scan(pa3, off_a["r"], off_a["lora"], off_a["g_rwkv"], mu_rwkv, w0, w_decay_up, a0, w_iclr_up,
                        k_k, k_a, r_k.reshape(-1), ln_x_w, ln_x_b)

    k0, k1 = _fox_prefix(pa3, off_a["lora"], b_f, pb3, off_b["fk"], fw)
    y_fox = _fox_attn(pb3, off_b["fq"], k0, k1, vt, pa3, off_a["g_fox"], fw)

    hm = _rmsnorm(mem.reshape(b * n_mem, d), g_mem, BF16)
    mkv3 = _proj([hm], [(w_mem_kv, (d, WIDE), lambda j: (0, j), (0, 0))], 2 * mw, WIDE, BF16,
                 "mem_kv_proj").reshape(b, n_mem, 2 * mw)
    y_mem = _mem_attn(pb3, off_b["mq"], mkv3, pa3, off_a["g_mq"], mw)

    ys = [y_rwkv.reshape(b * t, c), y_fox.reshape(b * t, fw), y_mem.reshape(b * t, mw)]
    return _out_proj(ys, _cast(w_out, BF16, "w_out_cast"), x2, g_post).reshape(b, t, d)


def kernel(x, mem, g_pre, w_in, mu_rwkv, w0, w_decay_up, a0, w_iclr_up, k_k, k_a, r_k, ln_x_w, ln_x_b, b_f,
           g_mem, w_mem_kv, w_out, g_post):
    for l in range(g_pre.shape[0]):
        x = _layer(x, mem, g_pre[l], w_in[l], mu_rwkv[l], w0[l], w_decay_up[l], a0[l], w_iclr_up[l], k_k[l],
                   k_a[l], r_k[l], ln_x_w[l], ln_x_b[l], b_f[l], g_mem[l], w_mem_kv[l], w_out[l], g_post[l])
    return x
```

```python
import functools

import jax
import jax.numpy as jnp
from jax import lax
from jax.experimental import pallas as pl
from jax.experimental.pallas import tpu as pltpu

HEAD_DIM = 64
MEM_HEADS = 4
RMS_EPS = 1e-6
GN_EPS = 64e-5
LOG2E = 1.4426950408889634

LANES = 128
MXU = 256
UNIT_HEADS = MXU // HEAD_DIM
UNIT = UNIT_HEADS * HEAD_DIM
RWKV_CHUNK = 64
WIDE = 512
VMEM_LIMIT_BYTES = 56 * 1024 * 1024
FOX_BIAS_PARTS = 3
FOX_V_ROWS = HEAD_DIM + 16

ROWS_ELEMENTWISE = 512
ROWS_PROJ = 1024
ROWS_PROJ_T = 1024
ROWS_OUT = 512
W_RING_DEPTH = 3
W_DMA_PRIORITY = 1
COLS_WEIGHT_PREP = 256
RWKV_CHUNKS_PER_STEP = 4
ROWS_PREFIX = 512
ROWS_MEM = 512
FOX_TQ = 512
FOX_TK = 256
FOX_PAIRS = 4
RWKV_UNITS = 6

F32 = jnp.float32
BF16 = jnp.bfloat16

_NN = (((1,), (0,)), ((), ()))
_NT = (((1,), (1,)), ((), ()))
_TN = (((0,), (0,)), ((), ()))


def _dot(a, b, dims=_NN):
    return lax.dot_general(a, b, dims, preferred_element_type=F32)


def _split_bf16(x, n):
    parts = []
    rem = x
    for _ in range(n):
        p = rem.astype(BF16)
        parts.append(p)
        rem = rem - p.astype(F32)
    return parts


def _dot_f32(a, b):
    a_hi, a_lo = _split_bf16(a, 2)
    b_hi, b_lo = _split_bf16(b, 2)
    return _dot(a_hi, b_hi) + (_dot(a_lo, b_hi) + _dot(a_hi, b_lo))


def _sigmoid(x):
    return 1.0 / (1.0 + jnp.exp(-x))


def _softplus(x):
    return jnp.maximum(x, 0.0) + jnp.log(1.0 + jnp.exp(-jnp.abs(x)))


def _params(*semantics):
    return pltpu.CompilerParams(dimension_semantics=semantics, vmem_limit_bytes=VMEM_LIMIT_BYTES)


def _rmsnorm_kernel(x_ref, g_ref, o_ref):
    x = x_ref[...]
    ms = jnp.mean(x * x, axis=-1, keepdims=True)
    o_ref[...] = (x * lax.rsqrt(ms + RMS_EPS) * g_ref[...]).astype(o_ref.dtype)


def _rmsnorm(x2d, g, out_dtype):
    m, d = x2d.shape
    tm = min(ROWS_ELEMENTWISE, m)
    assert m % tm == 0
    return pl.pallas_call(
        _rmsnorm_kernel,
        grid=(m // tm,),
        in_specs=[pl.BlockSpec((tm, d), lambda i: (i, 0)), pl.BlockSpec((1, d), lambda i: (0, 0))],
        out_specs=pl.BlockSpec((tm, d), lambda i: (i, 0)),
        out_shape=jax.ShapeDtypeStruct((m, d), out_dtype),
        compiler_params=_params("parallel"),
        name="rmsnorm",
    )(x2d, g.reshape(1, d))


def _proj_kernel(*refs, n_x, placement):
    xs = refs[:n_x]
    ws = refs[n_x:n_x + len(placement)]
    o_ref, wb_ref = refs[n_x + len(placement):]

    @pl.when(pl.program_id(1) == 0)
    def _():
        for w_ref, (r0, c0) in zip(ws, placement):
            wb_ref[r0:r0 + w_ref.shape[0], c0:c0 + w_ref.shape[1]] = w_ref[...].astype(BF16)

    acc = None
    r0 = 0
    for x_ref in xs:
        kx = x_ref.shape[1]
        d = _dot(x_ref[...], wb_ref[r0:r0 + kx, :])
        acc = d if acc is None else acc + d
        r0 += kx
    o_ref[...] = acc.astype(o_ref.dtype)


def _proj(xs, weights, n_cols, tn, out_dtype, name):
    m = xs[0].shape[0]
    k_total = sum(x.shape[1] for x in xs)
    tm = min(ROWS_PROJ, m)
    tn = min(tn, n_cols)
    assert m % tm == 0 and n_cols % tn == 0
    in_specs = [pl.BlockSpec((tm, x.shape[1]), lambda j, i: (i, 0)) for x in xs]
    for _, shape, index_fn, _ in weights:
        in_specs.append(pl.BlockSpec(shape, lambda j, i, f=index_fn: f(j)))
    return pl.pallas_call(
        functools.partial(_proj_kernel, n_x=len(xs), placement=tuple(p for _, _, _, p in weights)),
        grid=(n_cols // tn, m // tm),
        in_specs=in_specs,
        out_specs=pl.BlockSpec((tm, tn), lambda j, i: (i, j)),
        out_shape=jax.ShapeDtypeStruct((m, n_cols), out_dtype),
        scratch_shapes=[pltpu.VMEM((k_total, tn), BF16)],
        compiler_params=_params("parallel", "arbitrary"),
        name=name,
    )(*xs, *[w for w, _, _, _ in weights])


def _matmul_kernel(x_ref, w_ref, o_ref):
    o_ref[...] = _dot(x_ref[...], w_ref[...]).astype(o_ref.dtype)


def _matmul(x, w, out_dtype, tn, name):
    m, k = x.shape
    n = w.shape[1]
    tm = min(ROWS_PROJ, m)
    tn = min(tn, n)
    assert m % tm == 0 and n % tn == 0
    return pl.pallas_call(
        _matmul_kernel,
        grid=(n // tn, m // tm),
        in_specs=[pl.BlockSpec((tm, k), lambda j, i: (i, 0)), pl.BlockSpec((k, tn), lambda j, i: (0, j))],
        out_specs=pl.BlockSpec((tm, tn), lambda j, i: (i, j)),
        out_shape=jax.ShapeDtypeStruct((m, n), out_dtype),
        compiler_params=_params("parallel", "parallel"),
        name=name,
    )(x, w)


def _weight_prep_kernel(wt_ref, wa_ref, wb_ref, wvt_ref, *, pieces_a, pieces_b, piece_vt):
    def gather(pieces, o_ref):
        dst = 0
        for src, width in pieces:
            o_ref[:, dst:dst + width] = wt_ref[src:src + width, :].T.astype(o_ref.dtype)
            dst += width

    gather(pieces_a, wa_ref)
    gather(pieces_b, wb_ref)
    src, width = piece_vt
    wvt_ref[...] = wt_ref[src:src + width, :].astype(wvt_ref.dtype)


def _weight_prep(w_in_t, pieces_a, pieces_b, piece_vt):
    n, k = w_in_t.shape
    tc = min(COLS_WEIGHT_PREP, k)
    assert k % tc == 0 and all(s % 8 == 0 and wd % 16 == 0 for s, wd in pieces_a + pieces_b + (piece_vt,))
    na = sum(wd for _, wd in pieces_a)
    nb = sum(wd for _, wd in pieces_b)
    return pl.pallas_call(
        functools.partial(_weight_prep_kernel, pieces_a=pieces_a, pieces_b=pieces_b, piece_vt=piece_vt),
        grid=(k // tc,),
        in_specs=[pl.BlockSpec((n, tc), lambda i: (0, i))],
        out_specs=[pl.BlockSpec((tc, na), lambda i: (i, 0)), pl.BlockSpec((tc, nb), lambda i: (i, 0)),
                   pl.BlockSpec((piece_vt[1], tc), lambda i: (0, i))],
        out_shape=[jax.ShapeDtypeStruct((k, na), BF16), jax.ShapeDtypeStruct((k, nb), BF16),
                   jax.ShapeDtypeStruct((piece_vt[1], k), BF16)],
        compiler_params=_params("parallel"),
        name="weight_prep",
    )(w_in_t)


def _proj_t_kernel(wt_ref, x_ref, v_ref):
    vt = _dot(wt_ref[...], x_ref[...], _NT)
    ones = jnp.ones((FOX_V_ROWS - HEAD_DIM, vt.shape[1]), v_ref.dtype)
    for h in range(vt.shape[0] // HEAD_DIM):
        v_ref[h * FOX_V_ROWS:h * FOX_V_ROWS + HEAD_DIM, :] = vt[h * HEAD_DIM:(h + 1) * HEAD_DIM].astype(v_ref.dtype)
        v_ref[h * FOX_V_ROWS + HEAD_DIM:(h + 1) * FOX_V_ROWS, :] = ones


def _proj_t(wt, x):
    n, k = wt.shape
    m = x.shape[0]
    tm = min(ROWS_PROJ_T, m)
    assert m % tm == 0 and n % HEAD_DIM == 0
    n_aug = n // HEAD_DIM * FOX_V_ROWS
    return pl.pallas_call(
        _proj_t_kernel,
        grid=(m // tm,),
        in_specs=[pl.BlockSpec((n, k), lambda i: (0, 0)), pl.BlockSpec((tm, k), lambda i: (i, 0))],
        out_specs=pl.BlockSpec((n_aug, tm), lambda i: (0, i)),
        out_shape=jax.ShapeDtypeStruct((n_aug, m), BF16),
        compiler_params=_params("parallel"),
        name="in_proj_vt",
    )(wt, x)


def _cast_kernel(x_ref, o_ref):
    o_ref[...] = x_ref[...].astype(o_ref.dtype)


def _cast(x, dtype, name):
    m, n = x.shape
    tm = min(ROWS_OUT, m)
    assert m % tm == 0
    return pl.pallas_call(
        _cast_kernel,
        grid=(m // tm,),
        in_specs=[pl.BlockSpec((tm, n), lambda i: (i, 0))],
        out_specs=pl.BlockSpec((tm, n), lambda i: (i, 0)),
        out_shape=jax.ShapeDtypeStruct((m, n), dtype),
        compiler_params=_params("parallel"),
        name=name,
    )(x)


def _out_proj_kernel(*refs, n_y):
    ys = refs[:n_y]
    w_hbm, x_ref, g_ref, o_ref, ss_ref, w_ring, w_sem = refs[n_y:]
    j = pl.program_id(1)
    n_col = pl.num_programs(1)
    tn = w_ring.shape[2]
    d = o_ref.shape[1]
    step = pl.program_id(0) * n_col + j
    n_steps = pl.num_programs(0) * n_col

    def weight_copy(s):
        col = pl.multiple_of(lax.rem(s, n_col) * tn, tn)
        slot = lax.rem(s, W_RING_DEPTH)
        return pltpu.make_async_copy(w_hbm.at[:, pl.ds(col, tn)], w_ring.at[slot], w_sem.at[slot])

    @pl.when(step == 0)
    def _():
        for s in range(W_RING_DEPTH - 1):
            weight_copy(s).start(priority=W_DMA_PRIORITY)

    @pl.when(step + (W_RING_DEPTH - 1) < n_steps)
    def _():
        weight_copy(step + (W_RING_DEPTH - 1)).start(priority=W_DMA_PRIORITY)

    weight_copy(step).wait()
    w_ref = w_ring.at[lax.rem(step, W_RING_DEPTH)]

    acc = None
    r0 = 0
    for y_ref in ys:
        ky = y_ref.shape[1]
        part = _dot(y_ref[...], w_ref[r0:r0 + ky, :])
        acc = part if acc is None else acc + part
        r0 += ky

    @pl.when(j == 0)
    def _():
        ss_ref[...] = jnp.zeros_like(ss_ref)

    ss_ref[...] += jnp.sum(acc * acc, axis=1, keepdims=True)
    o_ref[:, pl.ds(pl.multiple_of(j * tn, tn), tn)] = acc

    @pl.when(j == pl.num_programs(1) - 1)
    def _():
        scale = lax.rsqrt(ss_ref[...] * (1.0 / d) + RMS_EPS)
        for c0 in range(0, d, tn):
            sl = slice(c0, c0 + tn)
            o_ref[:, sl] = x_ref[:, sl] + o_ref[:, sl] * scale * g_ref[:, sl]


def _out_proj(ys, w_bf16, x2d, g):
    m, d = x2d.shape
    tm = min(ROWS_OUT, m)
    tn = min(WIDE, d)
    k = w_bf16.shape[0]
    assert m % tm == 0 and d % tn == 0 and sum(y.shape[1] for y in ys) == k
    assert (m // tm) * (d // tn) >= W_RING_DEPTH - 1
    row = pl.BlockSpec((tm, d), lambda i, j: (i, 0))
    return pl.pallas_call(
        functools.partial(_out_proj_kernel, n_y=len(ys)),
        grid=(m // tm, d // tn),
        in_specs=[pl.BlockSpec((tm, y.shape[1]), lambda i, j: (i, 0)) for y in ys]
                 + [pl.BlockSpec(memory_space=pl.ANY), row, pl.BlockSpec((1, d), lambda i, j: (0, 0))],
        out_specs=row,
        out_shape=jax.ShapeDtypeStruct((m, d), x2d.dtype),
        scratch_shapes=[pltpu.VMEM((tm, 1), F32), pltpu.VMEM((W_RING_DEPTH, k, tn), BF16),
                        pltpu.SemaphoreType.DMA((W_RING_DEPTH,))],
        compiler_params=_params("arbitrary", "arbitrary"),
        name="out_proj_post",
    )(*ys, w_bf16, x2d, g.reshape(1, d))


def _rwkv_scan_kernel(*refs, n_units, L):
    r_ref, k_ref, v_ref, lora_ref = refs[:4]
    g_refs = refs[4:4 + n_units]
    (mu_r_ref, mu_k_ref, mu_v_ref, mu_lora_ref, w0_ref, wdu_ref, a0_ref, wiu_ref, kk_ref, ka_ref,
     rk_ref, lnw_ref, lnb_ref, o_ref, s_ref, carry_r, carry_k, carry_v, carry_lora) = refs[4 + n_units:]

    @pl.when(pl.program_id(2) == 0)
    def _():
        for z in (s_ref, carry_r, carry_k, carry_v, carry_lora):
            z[...] = jnp.zeros_like(z)

    rows = r_ref.shape[1]
    n_ch = rows // L
    first_row = lax.broadcasted_iota(jnp.int32, (rows, 1), 0) == 0

    def shift(x, carry_ref, mu):
        prev = jnp.where(first_row, carry_ref[...], pltpu.roll(x, 1, 0))
        carry_ref[...] = x[rows - 1:rows, :]
        return x + (prev - x) * mu

    r_all = shift(r_ref[0], carry_r, mu_r_ref[...])
    k_raw = shift(k_ref[0], carry_k, mu_k_ref[...])
    v_all = shift(v_ref[0], carry_v, mu_v_ref[...])
    lora = shift(lora_ref[0][:, :2 * LANES], carry_lora, mu_lora_ref[...])
    w_pre = -_softplus(-(w0_ref[...] + _dot_f32(jnp.tanh(lora[:, :LANES]), wdu_ref[...]))) - 0.5
    lw_all = -jnp.exp(w_pre)
    alpha = _sigmoid(a0_ref[...] + _dot_f32(lora[:, LANES:], wiu_ref[...]))
    k_all = k_raw * (1.0 + (alpha - 1.0) * ka_ref[...])
    kk_all = k_raw * kk_ref[...]
    hl = UNIT_HEADS * L
    n_sq = L.bit_length() - 2
    assert 2 ** (n_sq + 1) == L
    units = range(n_units)

    def blk(shape, d0, d1):
        return (lax.broadcasted_iota(jnp.int32, shape, 0) // d0) == (lax.broadcasted_iota(jnp.int32, shape, 1) // d1)

    same = blk((hl, hl), L, L)
    u_row = lax.broadcasted_iota(jnp.int32, (L, hl), 0)
    u_col = lax.broadcasted_iota(jnp.int32, (L, hl), 1) % L
    strict_u = u_col < u_row
    incl_u = u_col <= u_row
    eye_u = jnp.where(u_col == u_row, 1.0, 0.0)
    bd_rows = blk((hl, UNIT), L, HEAD_DIM)
    bd_state = blk((UNIT, UNIT), HEAD_DIM, HEAD_DIM)
    ones_bd = jnp.where(bd_state, 1.0, 0.0).astype(BF16)
    c_row = lax.broadcasted_iota(jnp.int32, (rows, rows), 0)
    c_col = lax.broadcasted_iota(jnp.int32, (rows, rows), 1)
    tril = jnp.where((c_row // L == c_col // L) & (c_row >= c_col), 1.0, 0.0).astype(BF16)

    def rep(x):
        return jnp.concatenate([x] * UNIT_HEADS, axis=0)

    def stack(x):
        return jnp.where(bd_rows, rep(x), 0.0)

    def headsum(x):
        return _dot(x.astype(BF16), ones_bd)

    def lanes(x, u):
        return x[:, u * UNIT:(u + 1) * UNIT]

    chains = [(ch, u) for ch in range(n_ch) for u in units]
    lw_hi, lw_lo = _split_bf16(lw_all, 2)
    cl_all = _dot(tril, lw_hi) + _dot(tril, lw_lo)
    kk_norm = [headsum(lanes(kk_all, u) * lanes(kk_all, u)) for u in units]
    kk_unit = [lanes(kk_all, u) * lax.rsqrt(jnp.maximum(kk_norm[u], 1e-24)) for u in units]

    ah, rh, e_last, lhs, rhs_b, rhs_k, upd_rhs, v = [], [], [], [], [], [], [], []
    for ch, u in chains:
        rs = slice(ch * L, (ch + 1) * L)
        sl = slice(u * UNIT, (u + 1) * UNIT)
        cl = cl_all[rs, sl]
        e_pos = jnp.exp(cl)
        e_neg = jnp.exp(-cl)
        e_rem = e_pos[L - 1:L, :] * e_neg
        kk = kk_unit[u][rs]
        a, b, k = -kk, kk * alpha[rs, sl], k_all[rs, sl]
        ah.append(a * jnp.exp(cl - lw_all[rs, sl]))
        rh.append(r_all[rs, sl] * e_pos)
        e_last.append(e_pos[L - 1:L, :])
        lhs.append(jnp.concatenate([ah[-1], rh[-1]], axis=0).astype(BF16))
        rhs_b.append(stack(b * e_neg).astype(BF16))
        rhs_k.append(stack(k * e_neg).astype(BF16))
        upd_rhs.append(jnp.concatenate([b * e_rem, k * e_rem], axis=0).astype(BF16))
        v.append(v_all[rs, sl])

    ids = range(len(chains))
    a_b = [_dot(lhs[i], rhs_b[i], _NT) for i in ids]
    a_k = [_dot(lhs[i], rhs_k[i], _NT) for i in ids]
    pw = [jnp.where(strict_u, a_b[i][:L], 0.0) for i in ids]
    a_ak = [jnp.where(strict_u, a_k[i][:L], 0.0).astype(BF16) for i in ids]
    a_rb = [jnp.where(incl_u, a_b[i][L:], 0.0).astype(BF16) for i in ids]
    a_rk = [jnp.where(incl_u, a_k[i][L:], 0.0).astype(BF16) for i in ids]

    def blockdiag(m_u):
        return jnp.where(same, rep(m_u), 0.0).astype(BF16)

    t_u = [eye_u + pw[i] for i in ids]
    pw_bd = [blockdiag(pw[i]) for i in ids]
    for _ in range(n_sq):
        pw = [_dot(pw[i].astype(BF16), pw_bd[i]) for i in ids]
        pw_bd = [blockdiag(pw[i]) for i in ids]
        t_u = [t_u[i] + _dot(t_u[i].astype(BF16), pw_bd[i]) for i in ids]
    t_u = [t_u[i].astype(BF16) for i in ids]
    v_s = [stack(v[i]).astype(BF16) for i in ids]

    s_cur = [s_ref[u] for u in units]
    y = [None] * len(chains)
    for ch in range(n_ch):
        cid = [ch * n_units + u for u in units]
        xr = [_dot(lhs[i], s_cur[u].astype(BF16), _NT) for u, i in zip(units, cid)]
        x = [xr[u][:L] + _dot(a_ak[i], v_s[i]) for u, i in zip(units, cid)]
        uu = [_dot(t_u[i], stack(x[u]).astype(BF16)) for u, i in zip(units, cid)]
        for u, i in zip(units, cid):
            y[i] = xr[u][L:] + _dot(a_rb[i], stack(uu[u]).astype(BF16)) + _dot(a_rk[i], v_s[i])
            upd = _dot(jnp.concatenate([uu[u], v[i]], axis=0).astype(BF16), upd_rhs[i], _TN)
            s_cur[u] = jnp.where(bd_state, s_cur[u] * e_last[i] + upd, 0.0)
    for u in units:
        s_ref[u] = s_cur[u]

    inv_n = 1.0 / HEAD_DIM
    y = [jnp.concatenate([y[ch * n_units + u] for ch in range(n_ch)], axis=0) for u in units]
    d = [y[u] - headsum(y[u]) * inv_n for u in units]
    var = [headsum(d[u] * d[u]) * inv_n for u in units]
    bonus = [headsum(lanes(r_all, u) * lanes(k_all, u) * lanes(rk_ref[...], u)) * lanes(v_all, u) for u in units]
    for u in units:
        sl = slice(u * UNIT, (u + 1) * UNIT)
        yn = d[u] * lax.rsqrt(var[u] + GN_EPS) * lnw_ref[:, sl] + lnb_ref[:, sl]
        g = g_refs[u][0]
        o_ref[0, :, sl] = ((yn + bonus[u]) * (g * _sigmoid(g))).astype(o_ref.dtype)


def _rwkv_scan(pa3, off_r, off_lora, off_g, mu, w0, wdu, a0, wiu, k_k, k_a, r_k, ln_w, ln_b):
    b, t, _ = pa3.shape
    c = w0.shape[0]
    L = RWKV_CHUNK * min(RWKV_CHUNKS_PER_STEP, t // RWKV_CHUNK)
    n_units = min(RWKV_UNITS, c // UNIT)
    w = n_units * UNIT
    assert t % L == 0 and c % w == 0 and off_r % w == 0 and off_g % UNIT == 0 and off_lora % WIDE == 0
    rb, gb, lb, ng = off_r // w, off_g // UNIT, off_lora // WIDE, c // w

    def col(base):
        return pl.BlockSpec((1, L, w), lambda bi, gi, ci: (bi, ci, base + gi))

    def vec(base=0):
        return pl.BlockSpec((1, w), lambda bi, gi, ci: (0, base + gi))

    gates = [pl.BlockSpec((1, L, UNIT), lambda bi, gi, ci, u=u: (bi, ci, gb + gi * n_units + u))
             for u in range(n_units)]
    lora_w = pl.BlockSpec((wdu.shape[0], w), lambda bi, gi, ci: (0, gi))
    mu_rkv = mu[:3 * c].reshape(1, 3 * c)
    mu_lora = mu[3 * c:].reshape(1, 2 * LANES)
    row = lambda z: z.reshape(1, c)
    return pl.pallas_call(
        functools.partial(_rwkv_scan_kernel, n_units=n_units, L=RWKV_CHUNK),
        grid=(b, ng, t // L),
        in_specs=[col(rb), col(rb + ng), col(rb + 2 * ng),
                  pl.BlockSpec((1, L, WIDE), lambda bi, gi, ci: (bi, ci, lb))] + gates
                 + [vec(0), vec(ng), vec(2 * ng), pl.BlockSpec((1, 2 * LANES), lambda bi, gi, ci: (0, 0)),
                    vec(), lora_w, vec(), lora_w, vec(), vec(), vec(), vec(), vec()],
        out_specs=pl.BlockSpec((1, L, w), lambda bi, gi, ci: (bi, ci, gi)),
        out_shape=jax.ShapeDtypeStruct((b, t, c), BF16),
        scratch_shapes=[pltpu.VMEM((n_units, UNIT, UNIT), F32)] + [pltpu.VMEM((1, w), F32)] * 3
                       + [pltpu.VMEM((1, 2 * LANES), F32)],
        compiler_params=_params("parallel", "parallel", "arbitrary"),
        name="rwkv_scan",
    )(pa3, pa3, pa3, pa3, *([pa3] * n_units), mu_rkv, mu_rkv, mu_rkv, mu_lora,
      row(w0), wdu, row(a0), wiu, row(k_k), row(k_a), row(r_k), row(ln_w), row(ln_b))


def _fox_prefix_kernel(f_ref, bf_ref, k_ref, k0_ref, k1_ref, carry_ref):
    @pl.when(pl.program_id(1) == 0)
    def _():
        carry_ref[...] = jnp.zeros_like(carry_ref)

    x = f_ref[0][:, 2 * LANES:3 * LANES] + bf_ref[...]
    log_f = -_softplus(-x)
    tc = x.shape[0]
    tril = jnp.where(lax.broadcasted_iota(jnp.int32, (tc, tc), 0) >= lax.broadcasted_iota(jnp.int32, (tc, tc), 1),
                     1.0, 0.0).astype(BF16)
    c = carry_ref[...]
    for part in _split_bf16(log_f, 3):
        c = c + _dot(tril, part)
    carry_ref[...] = c[tc - 1:tc, :]

    fw = k_ref.shape[2]
    src = lax.broadcasted_iota(jnp.int32, (FOX_BIAS_PARTS * LANES, fw), 0)
    dst = lax.broadcasted_iota(jnp.int32, (FOX_BIAS_PARTS * LANES, fw), 1)
    lane = dst % LANES
    head = 2 * (dst // LANES) + jnp.where(lane < HEAD_DIM, 1, 0)
    sel = jnp.where((src % LANES == head) & (lane % HEAD_DIM == src // LANES), 1.0, 0.0).astype(BF16)
    parts = jnp.concatenate(_split_bf16(c * LOG2E, FOX_BIAS_PARTS), axis=1)
    bias = _dot(parts, sel).astype(BF16)
    k = k_ref[0]
    first = (lax.broadcasted_iota(jnp.int32, (tc, fw), 1) % LANES) < HEAD_DIM
    k0_ref[0] = jnp.where(first, k, bias)
    k1_ref[0] = jnp.where(first, bias, k)


def _fox_prefix(pt3, off_f, b_f, pb3, off_k, fw):
    b, t, _ = pt3.shape
    fh = b_f.shape[0]
    tc = min(ROWS_PREFIX, t)
    assert t % tc == 0 and fh <= LANES and fw == fh * HEAD_DIM and off_f % WIDE == 0 and off_k % fw == 0
    bf = jnp.zeros((1, LANES), F32).at[0, :fh].set(b_f)
    fb, kb = off_f // WIDE, off_k // fw
    out = pl.BlockSpec((1, tc, fw), lambda bi, i: (bi, i, 0))
    return pl.pallas_call(
        _fox_prefix_kernel,
        grid=(b, t // tc),
        in_specs=[pl.BlockSpec((1, tc, WIDE), lambda bi, i: (bi, i, fb)),
                  pl.BlockSpec((1, LANES), lambda bi, i: (0, 0)),
                  pl.BlockSpec((1, tc, fw), lambda bi, i: (bi, i, kb))],
        out_specs=[out, out],
        out_shape=[jax.ShapeDtypeStruct((b, t, fw), BF16)] * 2,
        scratch_shapes=[pltpu.VMEM((1, LANES), F32)],
        compiler_params=_params("parallel", "arbitrary"),
        name="fox_prefix",
    )(pt3, bf, pb3)


def _fox_attn_kernel(q_ref, k0_ref, k1_ref, v_ref, g_ref, o_ref, m_ref, acc_ref, *, tk):
    qi = pl.program_id(2)
    tq = q_ref.shape[1]
    n_pairs = q_ref.shape[2] // LANES
    n_sub = tq // tk
    q_lane = lax.broadcasted_iota(jnp.int32, (tq, LANES), 1)
    q_head0 = q_lane < HEAD_DIM
    bias0 = jnp.where((q_lane >= HEAD_DIM) & (q_lane < HEAD_DIM + FOX_BIAS_PARTS), -1.0, 0.0)
    bias1 = jnp.where(q_lane < FOX_BIAS_PARTS, -1.0, 0.0)
    k_refs = (k0_ref, k1_ref)
    chains = [(pr, h) for pr in range(n_pairs) for h in range(2)]

    qs = []
    for pr in range(n_pairs):
        q = q_ref[0, :, pr * LANES:(pr + 1) * LANES].astype(F32) * (HEAD_DIM ** -0.5 * LOG2E)
        qs += [jnp.where(q_head0, q, bias0).astype(BF16), jnp.where(q_head0, bias1, q).astype(BF16)]

    m_ref[...] = jnp.full_like(m_ref, -jnp.inf)
    acc_ref[...] = jnp.zeros_like(acc_ref)

    def tile(j, q0, causal):
        start = pl.multiple_of(j * tk, tk)
        cols = slice(q0, tq)
        if causal:
            key = lax.broadcasted_iota(jnp.int32, (tk, tq - q0), 0)
            qry = lax.broadcasted_iota(jnp.int32, (tk, tq - q0), 1)
            valid = key <= qry
        scores = []
        for ci, (pr, h) in enumerate(chains):
            s = _dot(k_refs[h][0, pl.ds(start, tk), pr * LANES:(pr + 1) * LANES], qs[ci][q0:], _NT)
            scores.append(jnp.where(valid, s, -jnp.inf) if causal else s)
        for ci, s in enumerate(scores):
            m = m_ref[ci, :, cols]
            m_new = jnp.maximum(m, jnp.max(s, axis=0, keepdims=True))
            m_ref[ci, :, cols] = m_new
            p = jnp.exp2(s - m_new).astype(BF16)
            pv = _dot(v_ref[ci * FOX_V_ROWS:(ci + 1) * FOX_V_ROWS, pl.ds(start, tk)], p)
            acc_ref[ci, :, cols] = jnp.exp2(m - m_new) * acc_ref[ci, :, cols] + pv

    def body(j, carry):
        tile(j, 0, False)
        return carry

    lax.fori_loop(0, qi * n_sub, body, 0)
    for d in range(n_sub):
        tile(qi * n_sub + d, d * tk, True)

    for pr in range(n_pairs):
        sl = slice(pr * LANES, (pr + 1) * LANES)
        out_t = jnp.concatenate(
            [acc_ref[ci][:HEAD_DIM] * (1.0 / acc_ref[ci][HEAD_DIM:HEAD_DIM + 1]) for ci in (2 * pr, 2 * pr + 1)], axis=0)
        g = g_ref[0, :, sl]
        o_ref[0, :, sl] = (out_t.T * (g * _sigmoid(g))).astype(o_ref.dtype)


def _fox_attn(pb3, off_q, k0, k1, vt, pt3, off_g, fw):
    b, t, _ = pb3.shape
    tq = min(FOX_TQ, t)
    tk = min(FOX_TK, tq)
    n_pairs = min(FOX_PAIRS, fw // LANES)
    w = n_pairs * LANES
    v_rows = 2 * n_pairs * FOX_V_ROWS
    assert t % tq == 0 and tq % tk == 0 and fw % w == 0 and off_q % w == 0 and off_g % w == 0
    qb, gb = off_q // w, off_g // w
    keys = pl.BlockSpec((1, t, w), lambda bi, p, i: (bi, 0, p))
    return pl.pallas_call(
        functools.partial(_fox_attn_kernel, tk=tk),
        grid=(b, fw // w, t // tq),
        in_specs=[pl.BlockSpec((1, tq, w), lambda bi, p, i: (bi, i, qb + p)), keys, keys,
                  pl.BlockSpec((v_rows, t), lambda bi, p, i: (p, bi)),
                  pl.BlockSpec((1, tq, w), lambda bi, p, i: (bi, i, gb + p))],
        out_specs=pl.BlockSpec((1, tq, w), lambda bi, p, i: (bi, i, p)),
        out_shape=jax.ShapeDtypeStruct((b, t, fw), BF16),
        scratch_shapes=[pltpu.VMEM((2 * n_pairs, 1, tq), F32), pltpu.VMEM((2 * n_pairs, FOX_V_ROWS, tq), F32)],
        compiler_params=_params("parallel", "parallel", "arbitrary"),
        name="fox_attn",
    )(pb3, k0, k1, vt, pt3)


def _mem_attn_kernel(q_ref, kv_ref, g_ref, o_ref):
    mw = q_ref.shape[2]
    hd = mw // MEM_HEADS
    scale = hd ** -0.5
    outs = []
    for h in range(MEM_HEADS):
        q = q_ref[0, :, h * hd:(h + 1) * hd]
        mk = kv_ref[0, :, h * hd:(h + 1) * hd]
        mv = kv_ref[0, :, mw + h * hd:mw + (h + 1) * hd]
        s = _dot(q, mk, _NT) * scale
        p = jnp.exp(s - jnp.max(s, axis=1, keepdims=True))
        l = jnp.sum(p, axis=1, keepdims=True)
        outs.append(_dot(p.astype(BF16), mv) / l)
    g = g_ref[0]
    o_ref[0] = (jnp.concatenate(outs, axis=1) * (g * _sigmoid(g))).astype(o_ref.dtype)


def _mem_attn(pq3, off_q, mkv3, pt3, off_g, mw):
    b, t, _ = pq3.shape
    tm = min(ROWS_MEM, t)
    assert t % tm == 0 and off_q % mw == 0 and off_g % mw == 0 and (mw // MEM_HEADS) % LANES == 0
    n_mem = mkv3.shape[1]
    qb, gb = off_q // mw, off_g // mw
    return pl.pallas_call(
        _mem_attn_kernel,
        grid=(b, t // tm),
        in_specs=[pl.BlockSpec((1, tm, mw), lambda bi, i: (bi, i, qb)),
                  pl.BlockSpec((1, n_mem, 2 * mw), lambda bi, i: (bi, 0, 0)),
                  pl.BlockSpec((1, tm, mw), lambda bi, i: (bi, i, gb))],
        out_specs=pl.BlockSpec((1, tm, mw), lambda bi, i: (bi, i, 0)),
        out_shape=jax.ShapeDtypeStruct((b, t, mw), BF16),
        compiler_params=_params("parallel", "parallel"),
        name="mem_attn",
    )(pq3, mkv3, pt3)


def _layer(x, mem, g_pre, w_in, mu_rwkv, w0, w_decay_up, a0, w_iclr_up, k_k, k_a, r_k,
           ln_x_w, ln_x_b, b_f, g_mem, w_mem_kv, w_out, g_post):
    b, t, d = x.shape
    n_mem = mem.shape[1]
    c = w0.shape[0]
    fh = b_f.shape[0]
    fw = fh * HEAD_DIM
    mw = w_mem_kv.shape[1] // 2
    lora = w_decay_up.shape[0]
    assert lora == LANES and w_iclr_up.shape[0] == LANES and fh <= WIDE
    assert w_in.shape[1] == 4 * c + 2 * lora + 4 * fw + fh + 2 * mw
    assert c % WIDE == 0 and fw % WIDE == 0 and mw % WIDE == 0

    o_g = 3 * c + 2 * lora
    o_fq = o_g + c
    o_fv = o_fq + 2 * fw
    o_f = o_fv + fw
    o_gfox = o_f + fh
    o_mq = o_gfox + fw
    o_gmq = o_mq + mw

    x2 = x.reshape(b * t, d)
    h = _rmsnorm(x2, g_pre, BF16)

    lora_f = WIDE - 2 * lora
    assert o_f % LANES == 0 and o_f + lora_f <= w_in.shape[1]
    pieces_a = ((o_g, c), (0, 3 * c), (o_gmq, mw), (o_gfox, fw), (3 * c, 2 * lora), (o_f, lora_f))
    pieces_b = ((o_fq, 2 * fw), (o_mq, mw))
    w_a, w_b, w_vt = _weight_prep(w_in.T, pieces_a, pieces_b, (o_fv, fw))
    off_a = {"g_rwkv": 0, "r": c, "g_mq": 4 * c, "g_fox": 4 * c + mw, "lora": 4 * c + mw + fw}
    off_b = {"fq": 0, "fk": fw, "mq": 2 * fw}

    pa3 = _matmul(h, w_a, F32, 2 * WIDE, "in_proj_a").reshape(b, t, -1)
    pb3 = _matmul(h, w_b, BF16, 2 * WIDE, "in_proj_b").reshape(b, t, -1)
    vt = _proj_t(w_vt, h)

    y_rwkv = _rwkv_scan(pa3, off_a["r"], off_a["lora"], off_a["g_rwkv"], mu_rwkv, w0, w_decay_up, a0, w_iclr_up,
                        k_k, k_a, r_k.reshape(-1), ln_x_w, ln_x_b)

    k0, k1 = _fox_prefix(pa3, off_a["lora"], b_f, pb3, off_b["fk"], fw)
    y_fox = _fox_attn(pb3, off_b["fq"], k0, k1, vt, pa3, off_a["g_fox"], fw)

    hm = _rmsnorm(mem.reshape(b * n_mem, d), g_mem, BF16)
    mkv3 = _proj([hm], [(w_mem_kv, (d, WIDE), lambda j: (0, j), (0, 0))], 2 * mw, WIDE, BF16,
                 "mem_kv_proj").reshape(b, n_mem, 2 * mw)
    y_mem = _mem_attn(pb3, off_b["mq"], mkv3, pa3, off_a["g_mq"], mw)

    ys = [y_rwkv.reshape(b * t, c), y_fox.reshape(b * t, fw), y_mem.reshape(b * t, mw)]
    return _out_proj(ys, _cast(w_out, BF16, "w_out_cast"), x2, g_post).reshape(b, t, d)


def kernel(x, mem, g_pre, w_in, mu_rwkv, w0, w_decay_up, a0, w_iclr_up, k_k, k_a, r_k, ln_x_w, ln_x_b, b_f,
           g_mem, w_mem_kv, w_out, g_post):
    for l in range(g_pre.shape[0]):
        x = _layer(x, mem, g_pre[l], w_in[l], mu_rwkv[l], w0[l], w_decay_up[l], a0[l], w_iclr_up[l], k_k[l],
                   k_a[l], r_k[l], ln_x_w[l], ln_x_b[l], b_f[l], g_mem[l], w_mem_kv[l], w_out[l], g_post[l])
    return x
```
